```python
import math
import jax, jax.numpy as jnp
from jax import lax
import numpy as np

D_MODEL = 4096
BATCH = 4
SEQ = 2048
DEPTH = 4

N_EVEN = (DEPTH + 1) // 2
N_ODD = DEPTH // 2
RWKV_WIDTH = D_MODEL // 2
RWKV_HEAD_DIM = 64
RWKV_HEADS = RWKV_WIDTH // RWKV_HEAD_DIM
DECAY_LORA = max(32, int(round(1.8 * D_MODEL ** 0.5 / 32)) * 32)
ICLR_LORA = max(32, int(round(1.8 * D_MODEL ** 0.5 / 32)) * 32)
GATE_LORA = max(32, int(round(0.6 * D_MODEL ** 0.8 / 32)) * 32)
RWKV_COLS = 3 * RWKV_WIDTH + DECAY_LORA + ICLR_LORA + GATE_LORA
RWKV_LN_EPS = 64e-5
MOBA_WIDTH = D_MODEL - RWKV_WIDTH
MOBA_HEAD_DIM = 128
MOBA_HEADS = MOBA_WIDTH // MOBA_HEAD_DIM
MOBA_BLOCK = 256
MOBA_TOPK = 3
MOBA_QCHUNK = 8
ROPE_THETA = 10000.0
N_IN = RWKV_COLS + 3 * MOBA_WIDTH
POOL_WINDOWS = (2, 4, 8, 16)
POOL_GROUPS = len(POOL_WINDOWS)
POOL_GROUP_DIM = D_MODEL // POOL_GROUPS
FFN_DIM = 256 * math.ceil(8 * D_MODEL / 3 / 256)
N_EXPERTS = 8
MOE_TOPK = 2
EXPERT_DIM = 7 * D_MODEL // 16
N_MOD = 6
LN_EPS = 1e-5
DEEPNORM_ALPHA = (2 * DEPTH) ** 0.25
DEEPNORM_BETA = (8 * DEPTH) ** -0.25
NEG_INF = -1e30

kernel_name = "hybrid_rwkv7_moba_pool_moe_trunk"


def layer_norm(x, g, b):
    xf = x.astype(jnp.float32)
    mu = jnp.mean(xf, axis=-1, keepdims=True)
    var = jnp.mean(jnp.square(xf - mu), axis=-1, keepdims=True)
    return ((xf - mu) * lax.rsqrt(var + LN_EPS) * g + b).astype(x.dtype)


def rope(t):
    T_, Dh = t.shape[1], t.shape[-1]
    inv = jnp.power(ROPE_THETA, -jnp.arange(0, Dh, 2, dtype=jnp.float32) / Dh)
    ang = jnp.arange(T_, dtype=jnp.float32)[:, None] * inv[None, :]
    cos = jnp.cos(ang)[None, :, None, :]
    sin = jnp.sin(ang)[None, :, None, :]
    t1 = t[..., : Dh // 2].astype(jnp.float32)
    t2 = t[..., Dh // 2:].astype(jnp.float32)
    return jnp.concatenate([t1 * cos - t2 * sin, t2 * cos + t1 * sin], axis=-1).astype(t.dtype)


def wkv7_scan(r, w, k, v, a, b):
    B_, _, H, N = r.shape

    def step(S, inp):
        r_t, w_t, k_t, v_t, a_t, b_t = inp
        sa = jnp.einsum('bhij,bhj->bhi', S, a_t)
        S = S * w_t[:, :, None, :] + sa[..., None] * b_t[:, :, None, :] + v_t[..., None] * k_t[:, :, None, :]
        return S, jnp.einsum('bhij,bhj->bhi', S, r_t)

    xs = tuple(jnp.swapaxes(t, 0, 1) for t in (r, w, k, v, a, b))
    S0 = jnp.zeros((B_, H, N, N), jnp.float32)
    _, y = lax.scan(step, S0, xs)
    return jnp.swapaxes(y, 0, 1)


def rwkv7_time_mix(zA, mu, w0, w2, a0, a2, g2, k_k, k_a, r_k, lnx_g, lnx_b):
    B_, T_, _ = zA.shape
    z_prev = jnp.pad(zA, ((0, 0), (1, 0), (0, 0)))[:, :-1]
    zA = zA + (z_prev - zA) * mu
    DA = RWKV_WIDTH
    r, k, v, wl, al, gl = jnp.split(
        zA, [DA, 2 * DA, 3 * DA, 3 * DA + DECAY_LORA, 3 * DA + DECAY_LORA + ICLR_LORA], axis=-1)
    w = -jax.nn.softplus(-(w0 + jnp.tanh(wl) @ w2)) - 0.5
    a = jax.nn.sigmoid(a0 + al @ a2)
    g = jax.nn.sigmoid(gl) @ g2
    heads = lambda t: t.reshape(B_, T_, RWKV_HEADS, RWKV_HEAD_DIM).astype(jnp.float32)
    kk = heads(k * k_k)
    kk = kk / jnp.maximum(jnp.sqrt(jnp.sum(kk * kk, axis=-1, keepdims=True)), 1e-12)
    k = k * (1 + (a - 1) * k_a)
    decay = jnp.exp(-jnp.exp(heads(w)))
    rh, kh, vh = heads(r), heads(k), heads(v)
    y = wkv7_scan(rh, decay, kh, vh, -kk, kk * heads(a))
    mean = jnp.mean(y, axis=-1, keepdims=True)
    var = jnp.mean(jnp.square(y - mean), axis=-1, keepdims=True)
    y = (y - mean) * lax.rsqrt(var + RWKV_LN_EPS)
    y = y * lnx_g.reshape(RWKV_HEADS, RWKV_HEAD_DIM) + lnx_b.reshape(RWKV_HEADS, RWKV_HEAD_DIM)
    bonus = jnp.sum(rh * kh * r_k.astype(jnp.float32), axis=-1, keepdims=True) * vh
    y = (y + bonus).reshape(B_, T_, RWKV_WIDTH).astype(zA.dtype)
    return y * g


def moba_attention(q, k, v):
    B_, T_, H, Dh = q.shape
    BS, QC = MOBA_BLOCK, MOBA_QCHUNK
    nb = -(-T_ // BS)
    Tp = nb * BS
    pad = ((0, 0), (0, Tp - T_), (0, 0), (0, 0))
    q, k, v = [jnp.swapaxes(jnp.pad(t, pad), 1, 2) for t in (q, k, v)]
    kb = k.reshape(B_, H, nb, BS, Dh)
    vb = v.reshape(B_, H, nb, BS, Dh)
    kmean = jnp.mean(kb.astype(jnp.float32), axis=3)
    n_sel = min(MOBA_TOPK, nb - 1)
    n_chunks = Tp // QC
    qc = jnp.moveaxis(q.reshape(B_, H, n_chunks, QC, Dh), 2, 0)
    bi = jnp.arange(B_)[:, None, None, None]
    hi = jnp.arange(H)[None, :, None, None]
    scale = Dh ** -0.5

    def chunk(args):
        ci, q_c = args
        q0 = ci * QC
        blk = q0 // BS
        qpos = q0 + jnp.arange(QC)
        kpos = blk * BS + jnp.arange(BS)
        k_own = lax.dynamic_index_in_dim(kb, blk, axis=2, keepdims=False)
        v_own = lax.dynamic_index_in_dim(vb, blk, axis=2, keepdims=False)
        s_own = jnp.einsum('bhqd,bhkd->bhqk', q_c, k_own).astype(jnp.float32) * scale
        s_own = jnp.where(kpos[None, :] <= qpos[:, None], s_own, NEG_INF)
        if n_sel == 0:
            p = jax.nn.softmax(s_own, axis=-1).astype(v_own.dtype)
            return jnp.einsum('bhqk,bhkd->bhqd', p, v_own)
        gate = jnp.einsum('bhqd,bhnd->bhqn', q_c.astype(jnp.float32), kmean)
        gate = jnp.where(jnp.arange(nb) < blk, gate, -jnp.inf)
        _, idx = lax.top_k(gate, n_sel)
        k_sel = kb[bi, hi, idx]
        v_sel = vb[bi, hi, idx]
        s_sel = jnp.einsum('bhqd,bhqskd->bhqsk', q_c, k_sel).astype(jnp.float32) * scale
        s_sel = jnp.where((jnp.arange(n_sel) < blk)[:, None], s_sel, NEG_INF)
        s = jnp.concatenate([s_own, s_sel.reshape(B_, H, QC, n_sel * BS)], axis=-1)
        p = jax.nn.softmax(s, axis=-1).astype(v_own.dtype)
        p_own = p[..., :BS]
        p_sel = p[..., BS:].reshape(B_, H, QC, n_sel, BS)
        return (jnp.einsum('bhqk,bhkd->bhqd', p_own, v_own)
                + jnp.einsum('bhqsk,bhqskd->bhqd', p_sel, v_sel))

    out = lax.map(chunk, (jnp.arange(n_chunks), qc))
    out = jnp.moveaxis(out, 0, 2).reshape(B_, H, Tp, Dh)[:, :, :T_]
    return jnp.swapaxes(out, 1, 2)


def rwkv_moba_mixer(h, w_in, mu, w0, w2, a0, a2, g2, k_k, k_a, r_k, lnx_g, lnx_b, w_out):
    B_, T_, _ = h.shape
    z = jnp.einsum('btd,dn->btn', h, w_in)
    zA = z[..., :RWKV_COLS]
    zq, zk, zv = jnp.split(z[..., RWKV_COLS:], 3, axis=-1)
    yA = rwkv7_time_mix(zA, mu, w0, w2, a0, a2, g2, k_k, k_a, r_k, lnx_g, lnx_b)
    hsplit = lambda t: t.reshape(B_, T_, MOBA_HEADS, MOBA_HEAD_DIM)
    yB = moba_attention(rope(hsplit(zq)), rope(hsplit(zk)), hsplit(zv)).reshape(B_, T_, MOBA_WIDTH)
    return jnp.einsum('btc,cd->btd', jnp.concatenate([yA, yB], axis=-1), w_out)


def multiscale_pool_mixer(h, pool_w, pool_scale):
    B_, T_, _ = h.shape
    hg = h.reshape(B_, T_, POOL_GROUPS, POOL_GROUP_DIM)
    cs = jnp.cumsum(hg.astype(jnp.float32), axis=1)
    t_cnt = jnp.arange(1, T_ + 1, dtype=jnp.float32)
    pooled = []
    for g, win in enumerate(POOL_WINDOWS):
        c_g = cs[:, :, g]
        lag = jnp.pad(c_g, ((0, 0), (win, 0), (0, 0)))[:, :T_]
        cnt = jnp.minimum(t_cnt, float(win))[None, :, None]
        pooled.append((c_g - lag) / cnt - hg[:, :, g].astype(jnp.float32))
    p = jnp.stack(pooled, axis=2).astype(h.dtype)
    y = jnp.einsum('btgc,gce->btge', p, pool_w).reshape(B_, T_, D_MODEL)
    return y * pool_scale


def swiglu(h, w_gate, w_up, w_down):
    return (jax.nn.silu(h @ w_gate) * (h @ w_up)) @ w_down


def moe_swiglu(h, router_w, router_b, w_gate, w_up, w_down):
    logits = (h @ router_w).astype(jnp.float32) + router_b.astype(jnp.float32)
    top_val, top_idx = lax.top_k(logits, MOE_TOPK)
    gates = jax.nn.softmax(top_val, axis=-1)
    combine = jnp.sum(jax.nn.one_hot(top_idx, N_EXPERTS, dtype=jnp.float32) * gates[..., None], axis=-2)
    combine = combine.astype(h.dtype)
    out = jnp.zeros_like(h)
    for e in range(N_EXPERTS):
        out = out + combine[..., e:e + 1] * swiglu(h, w_gate[e], w_up[e], w_down[e])
    return out


def setup_inputs(seed: int = 0) -> dict:
    key = jax.random.key(seed)
    ks = iter(jax.random.split(key, 48))
    f32 = jnp.float32

    def nrm(shape, scale):
        return jax.random.normal(next(ks), shape, f32) * scale

    D = D_MODEL
    gate_rows = jnp.array([0.0, 0.0, 1.0, 0.0, 0.0, 1.0], f32)[None, :, None]
    return {
        "x": nrm((BATCH, SEQ, D), 1.0),
        "c": nrm((BATCH, D), 1.0),
        "ada_w": nrm((D, N_MOD * D), 0.1 * D ** -0.5),
        "ada_b": nrm((N_MOD * D,), 0.02),
        "ada_table": nrm((DEPTH, N_MOD, D), 0.02) + gate_rows,
        "ln_g": 1.0 + nrm((DEPTH, 2, D), 0.02),
        "ln_b": nrm((DEPTH, 2, D), 0.02),
        "mix_w_in": nrm((N_EVEN, D, N_IN), D ** -0.5),
        "mix_mu": jax.random.uniform(next(ks), (N_EVEN, RWKV_COLS), f32),
        "rwkv_w0": jax.random.uniform(next(ks), (N_EVEN, RWKV_WIDTH), f32, -6.0, 1.0),
        "rwkv_w2": nrm((N_EVEN, DECAY_LORA, RWKV_WIDTH), 0.5 * DECAY_LORA ** -0.5),
        "rwkv_a0": nrm((N_EVEN, RWKV_WIDTH), 0.1),
        "rwkv_a2": nrm((N_EVEN, ICLR_LORA, RWKV_WIDTH), 0.5 * ICLR_LORA ** -0.5),
        "rwkv_g2": nrm((N_EVEN, GATE_LORA, RWKV_WIDTH), 2.0 * GATE_LORA ** -0.5),
        "rwkv_kk": 0.85 + nrm((N_EVEN, RWKV_WIDTH), 0.02),
        "rwkv_ka": 1.0 + nrm((N_EVEN, RWKV_WIDTH), 0.02),
        "rwkv_rk": nrm((N_EVEN, RWKV_HEADS, RWKV_HEAD_DIM), 0.1),
        "rwkv_lnx_g": 1.0 + nrm((N_EVEN, RWKV_WIDTH), 0.02),
        "rwkv_lnx_b": nrm((N_EVEN, RWKV_WIDTH), 0.02),
        "mix_w_out": nrm((N_EVEN, D, D), DEEPNORM_BETA * D ** -0.5),
        "ffn_w_gate": nrm((N_EVEN, D, FFN_DIM), D ** -0.5),
        "ffn_w_up": nrm((N_EVEN, D, FFN_DIM), D ** -0.5),
        "ffn_w_down": nrm((N_EVEN, FFN_DIM, D), DEEPNORM_BETA * FFN_DIM ** -0.5),
        "pool_w": nrm((N_ODD, POOL_GROUPS, POOL_GROUP_DIM, POOL_GROUP_DIM), DEEPNORM_BETA * POOL_GROUP_DIM ** -0.5),
        "pool_scale": 1.0 + nrm((N_ODD, D), 0.02),
        "moe_router_w": nrm((N_ODD, D, N_EXPERTS), D ** -0.5),
        "moe_router_b": nrm((N_ODD, N_EXPERTS), 0.01),
        "moe_w_gate": nrm((N_ODD, N_EXPERTS, D, EXPERT_DIM), D ** -0.5),
        "moe_w_up": nrm((N_ODD, N_EXPERTS, D, EXPERT_DIM), D ** -0.5),
        "moe_w_down": nrm((N_ODD, N_EXPERTS, EXPERT_DIM, D), DEEPNORM_BETA * EXPERT_DIM ** -0.5),
    }


def reference(x, c, ada_w, ada_b, ada_table, ln_g, ln_b,
              mix_w_in, mix_mu, rwkv_w0, rwkv_w2, rwkv_a0, rwkv_a2, rwkv_g2,
              rwkv_kk, rwkv_ka, rwkv_rk, rwkv_lnx_g, rwkv_lnx_b, mix_w_out,
              ffn_w_gate, ffn_w_up, ffn_w_down,
              pool_w, pool_scale,
              moe_router_w, moe_router_b, moe_w_gate, moe_w_up, moe_w_down):
    B_ = x.shape[0]
    mod = (jax.nn.silu(c) @ ada_w + ada_b).reshape(B_, N_MOD, D_MODEL)
    for l in range(DEPTH):
        m = mod + ada_table[l][None]
        sh1, sc1, g1, sh2, sc2, g2 = [m[:, i, None, :] for i in range(N_MOD)]
        i = l // 2
        h = x * (1 + sc1) + sh1
        if l % 2 == 0:
            y = rwkv_moba_mixer(h, mix_w_in[i], mix_mu[i], rwkv_w0[i], rwkv_w2[i], rwkv_a0[i],
                                rwkv_a2[i], rwkv_g2[i], rwkv_kk[i], rwkv_ka[i], rwkv_rk[i],
                                rwkv_lnx_g[i], rwkv_lnx_b[i], mix_w_out[i])
        else:
            y = multiscale_pool_mixer(h, pool_w[i], pool_scale[i])
        x = layer_norm(DEEPNORM_ALPHA * x + g1 * y, ln_g[l, 0], ln_b[l, 0])
        h = x * (1 + sc2) + sh2
        if l % 2 == 0:
            y = swiglu(h, ffn_w_gate[i], ffn_w_up[i], ffn_w_down[i])
        else:
            y = moe_swiglu(h, moe_router_w[i], moe_router_b[i], moe_w_gate[i], moe_w_up[i], moe_w_down[i])
        x = layer_norm(DEEPNORM_ALPHA * x + g2 * y, ln_g[l, 1], ln_b[l, 1])
    return x
```

```python
import functools
import math

import jax
import jax.numpy as jnp
from jax import lax
from jax.experimental import pallas as pl
from jax.experimental.pallas import tpu as pltpu

F32 = jnp.float32
BF16 = jnp.bfloat16
HI = lax.Precision.HIGHEST

RWKV_HEAD_DIM = 64
DECAY_LORA = 128
ICLR_LORA = 128
GATE_LORA_PAD = 512
RWKV_LN_EPS = 64e-5
MOBA_HEAD_DIM = 128
MOBA_BLOCK = 256
MOBA_TOPK = 3
ROPE_THETA = 10000.0
POOL_WINDOWS = (2, 4, 8, 16)
POOL_HALO = 16
MOE_TOPK = 2
N_MOD = 6
LN_EPS = 1e-5
NEG_INF = -1e30

LANES = 128
RWKV_CHUNK = 64
VMEM_LIMIT = 56 * 1024 * 1024


def _cparams(sem):
    return pltpu.CompilerParams(dimension_semantics=sem, vmem_limit_bytes=VMEM_LIMIT)


def _dot(a, b, precision=None):
    return jnp.dot(a, b, preferred_element_type=F32, precision=precision)


def _dot_nt(a, b, precision=None):
    return lax.dot_general(a, b, (((1,), (1,)), ((), ())), preferred_element_type=F32, precision=precision)


def _dot_tn(a, b, precision=None):
    return lax.dot_general(a, b, (((0,), (0,)), ((), ())), preferred_element_type=F32, precision=precision)


def _ada_kernel(c_ref, w_ref, b_ref, tab_ref, o_ref):
    c = c_ref[...]
    sc = (c * jax.nn.sigmoid(c)).astype(BF16)
    acc = _dot(sc, w_ref[...].astype(BF16)) + b_ref[...]
    o_ref[...] = acc[None, :, :] + tab_ref[...][:, None, :]


def ada_mod(c, ada_w, ada_b, ada_table):
    B, D = c.shape
    N = ada_w.shape[1]
    depth = ada_table.shape[0]
    BP = 8
    cp = jnp.zeros((BP, D), F32).at[:B].set(c)
    tn = 512
    out = pl.pallas_call(
        _ada_kernel,
        grid=(N // tn,),
        in_specs=[
            pl.BlockSpec((BP, D), lambda j: (0, 0)),
            pl.BlockSpec((D, tn), lambda j: (0, j)),
            pl.BlockSpec((1, tn), lambda j: (0, j)),
            pl.BlockSpec((depth, tn), lambda j: (0, j)),
        ],
        out_specs=pl.BlockSpec((depth, BP, tn), lambda j: (0, 0, j)),
        out_shape=jax.ShapeDtypeStruct((depth, BP, N), F32),
        compiler_params=_cparams(("parallel",)),
        name="ada_mod",
    )(cp, ada_w, ada_b.reshape(1, N), ada_table.reshape(depth, N))
    return out[:, :B].reshape(depth, B, N_MOD, 1, D)


def _modulate_kernel(x_ref, sc_ref, sh_ref, o_ref):
    o_ref[...] = (x_ref[...] * (1.0 + sc_ref[...]) + sh_ref[...]).astype(o_ref.dtype)


def modulate(x, sc, sh, T):
    M, D = x.shape
    tm = 512
    tpb = T // tm
    vec = pl.BlockSpec((None, 1, D), lambda i: (i // tpb, 0, 0))
    return pl.pallas_call(
        _modulate_kernel,
        grid=(M // tm,),
        in_specs=[pl.BlockSpec((tm, D), lambda i: (i, 0)), vec, vec],
        out_specs=pl.BlockSpec((tm, D), lambda i: (i, 0)),
        out_shape=jax.ShapeDtypeStruct((M, D), BF16),
        compiler_params=_cparams(("parallel",)),
        name="modulate",
    )(x, sc, sh)


def _mm_kernel(a_ref, w_ref, o_ref):
    o_ref[...] = _dot(a_ref[...], w_ref[...]).astype(o_ref.dtype)


def matmul(a, w, tm, tn, out_dtype=F32):
    M, K = a.shape
    N = w.shape[1]
    return pl.pallas_call(
        _mm_kernel,
        grid=(N // tn, M // tm),
        in_specs=[pl.BlockSpec((tm, K), lambda j, i: (i, 0)),
                  pl.BlockSpec((K, tn), lambda j, i: (0, j))],
        out_specs=pl.BlockSpec((tm, tn), lambda j, i: (i, j)),
        out_shape=jax.ShapeDtypeStruct((M, N), out_dtype),
        compiler_params=_cparams(("parallel", "parallel")),
        name="matmul",
    )(a, w)


def _resid_ln_mod(x, y, gate, lng, lnb, sc, sh, alpha):
    r = alpha * x + gate * y
    mu = jnp.mean(r, axis=-1, keepdims=True)
    d = r - mu
    var = jnp.mean(d * d, axis=-1, keepdims=True)
    xn = d * lax.rsqrt(var + LN_EPS) * lng + lnb
    return xn, xn * (1.0 + sc) + sh


def _mm_ln_kernel(a_ref, w_ref, x_ref, gate_ref, lng_ref, lnb_ref, sc_ref, sh_ref,
                  xo_ref, ho_ref, acc_ref, *, alpha):
    k = pl.program_id(1)

    @pl.when(k == 0)
    def _():
        acc_ref[...] = jnp.zeros_like(acc_ref)

    acc_ref[...] += _dot(a_ref[...], w_ref[...])

    @pl.when(k == pl.num_programs(1) - 1)
    def _():
        xn, h = _resid_ln_mod(x_ref[...], acc_ref[...], gate_ref[...], lng_ref[...], lnb_ref[...],
                              sc_ref[...], sh_ref[...], alpha)
        xo_ref[...] = xn
        ho_ref[...] = h.astype(ho_ref.dtype)


def matmul_resid_ln(a, w, x, gate, lng, lnb, sc, sh, T, alpha, tm, tk):
    M, K = a.shape
    D = w.shape[1]
    tpb = T // tm
    vec_b = pl.BlockSpec((None, 1, D), lambda i, k: (i // tpb, 0, 0))
    vec = pl.BlockSpec((1, D), lambda i, k: (0, 0))
    row = pl.BlockSpec((tm, D), lambda i, k: (i, 0))
    return pl.pallas_call(
        functools.partial(_mm_ln_kernel, alpha=alpha),
        grid=(M // tm, K // tk),
        in_specs=[pl.BlockSpec((tm, tk), lambda i, k: (i, k)),
                  pl.BlockSpec((tk, D), lambda i, k: (k, 0)),
                  row, vec_b, vec, vec, vec_b, vec_b],
        out_specs=[row, row],
        out_shape=[jax.ShapeDtypeStruct((M, D), F32), jax.ShapeDtypeStruct((M, D), BF16)],
        scratch_shapes=[pltpu.VMEM((tm, D), F32)],
        compiler_params=_cparams(("parallel", "arbitrary")),
        name="matmul_resid_ln",
    )(a, w, x, gate, lng, lnb, sc, sh)


def _swiglu_up_kernel(a_ref, wg_ref, wu_ref, o_ref):
    a = a_ref[...]
    g = _dot(a, wg_ref[...])
    u = _dot(a, wu_ref[...])
    o_ref[...] = (g * jax.nn.sigmoid(g) * u).astype(o_ref.dtype)


def swiglu_up(a, wg, wu, tm, tn):
    M, K = a.shape
    N = wg.shape[1]
    wspec = pl.BlockSpec((K, tn), lambda i, j: (0, j))
    return pl.pallas_call(
        _swiglu_up_kernel,
        grid=(M // tm, N // tn),
        in_specs=[pl.BlockSpec((tm, K), lambda i, j: (i, 0)), wspec, wspec],
        out_specs=pl.BlockSpec((tm, tn), lambda i, j: (i, j)),
        out_shape=jax.ShapeDtypeStruct((M, N), BF16),
        compiler_params=_cparams(("parallel", "parallel")),
        name="swiglu_up",
    )(a, wg, wu)


def _router_kernel(x_ref, sc_ref, sh_ref, rw_ref, rb_ref, o_ref, *, n_experts):
    h = x_ref[...] * (1.0 + sc_ref[...]) + sh_ref[...]
    logits = _dot(h, rw_ref[...], HI) + rb_ref[...]
    lane = lax.broadcasted_iota(jnp.int32, logits.shape, 1)
    logits = jnp.where(lane < n_experts, logits, -jnp.inf)
    m1 = jnp.max(logits, axis=-1, keepdims=True)
    i1 = jnp.min(jnp.where(logits == m1, lane, LANES), axis=-1, keepdims=True)
    oh1 = lane == i1
    rest = jnp.where(oh1, -jnp.inf, logits)
    m2 = jnp.max(rest, axis=-1, keepdims=True)
    i2 = jnp.min(jnp.where(rest == m2, lane, LANES), axis=-1, keepdims=True)
    oh2 = lane == i2
    e = jnp.exp(m2 - m1)
    g1 = 1.0 / (1.0 + e)
    g2 = e / (1.0 + e)
    o_ref[...] = jnp.where(oh1, g1, 0.0) + jnp.where(oh2, g2, 0.0)


def moe_router(x, sc, sh, router_w, router_b, T):
    M, D = x.shape
    E = router_w.shape[1]
    tm = 512
    tpb = T // tm
    rw = jnp.zeros((D, LANES), F32).at[:, :E].set(router_w)
    rb = jnp.zeros((1, LANES), F32).at[0, :E].set(router_b)
    vec = pl.BlockSpec((None, 1, D), lambda i: (i // tpb, 0, 0))
    return pl.pallas_call(
        functools.partial(_router_kernel, n_experts=E),
        grid=(M // tm,),
        in_specs=[pl.BlockSpec((tm, D), lambda i: (i, 0)), vec, vec,
                  pl.BlockSpec((D, LANES), lambda i: (0, 0)),
                  pl.BlockSpec((1, LANES), lambda i: (0, 0))],
        out_specs=pl.BlockSpec((tm, LANES), lambda i: (i, 0)),
        out_shape=jax.ShapeDtypeStruct((M, LANES), F32),
        compiler_params=_cparams(("parallel",)),
        name="moe_router",
    )(x, sc, sh, rw, rb)


def _moe_up_kernel(a_ref, comb_ref, wg_ref, wu_ref, o_ref):
    e = pl.program_id(1)
    a = a_ref[...]
    g = _dot(a, wg_ref[...])
    u = _dot(a, wu_ref[...])
    comb = comb_ref[...]
    lane = lax.broadcasted_iota(jnp.int32, comb.shape, 1)
    ce = jnp.sum(jnp.where(lane == e, comb, 0.0), axis=-1, keepdims=True)
    o_ref[...] = (g * jax.nn.sigmoid(g) * u * ce).astype(o_ref.dtype)


def moe_up_dense(a, comb, wg, wu, tm, tn):
    M, K = a.shape
    E, _, NE = wg.shape
    nt = NE // tn
    wspec = pl.BlockSpec((None, K, tn), lambda i, e, j: (e, 0, j))
    return pl.pallas_call(
        _moe_up_kernel,
        grid=(M // tm, E, nt),
        in_specs=[pl.BlockSpec((tm, K), lambda i, e, j: (i, 0)),
                  pl.BlockSpec((tm, LANES), lambda i, e, j: (i, 0)), wspec, wspec],
        out_specs=pl.BlockSpec((tm, tn), lambda i, e, j: (i, e * nt + j)),
        out_shape=jax.ShapeDtypeStruct((M, E * NE), BF16),
        compiler_params=_cparams(("parallel", "parallel", "parallel")),
        name="moe_up",
    )(a, comb, wg, wu)


def _pool_kernel(x_ref, xh_ref, sc_ref, sh_ref, w_ref, ps_ref, gate_ref, lng_ref, lnb_ref,
                 sc2_ref, sh2_ref, xo_ref, ho_ref, hh_ref, y_ref, *, tm, tpb, alpha):
    i = pl.program_id(0)
    D = x_ref.shape[1]
    dg = D // len(POOL_WINDOWS)
    sc = sc_ref[...]
    sh = sh_ref[...]
    x = x_ref[...]
    first = (i % tpb) == 0
    halo = xh_ref[...] * (1.0 + sc) + sh
    hh_ref[0:POOL_HALO, :] = jnp.where(first, 0.0, halo)
    hh_ref[POOL_HALO:, :] = x * (1.0 + sc) + sh
    t = (i % tpb) * tm + lax.broadcasted_iota(jnp.int32, (tm, 1), 0)
    tcnt = (t + 1).astype(F32)
    for g, win in enumerate(POOL_WINDOWS):
        cols = slice(g * dg, (g + 1) * dg)
        h = hh_ref[POOL_HALO:, cols]
        s = h
        for d in range(1, win):
            s = s + hh_ref[pl.ds(POOL_HALO - d, tm), cols]
        inv = 1.0 / jnp.minimum(tcnt, float(win))
        p = (s * inv - h).astype(BF16)
        y_ref[:, cols] = _dot(p, w_ref[g])
    y = y_ref[...] * ps_ref[...]
    xn, hn = _resid_ln_mod(x, y, gate_ref[...], lng_ref[...], lnb_ref[...], sc2_ref[...], sh2_ref[...], alpha)
    xo_ref[...] = xn
    ho_ref[...] = hn.astype(ho_ref.dtype)


def pool_mixer_ln(x, sc, sh, pool_w, pool_scale, gate, lng, lnb, sc2, sh2, T, alpha):
    M, D = x.shape
    G, dg, _ = pool_w.shape
    tm = 256
    tpb = T // tm
    hb = tm // POOL_HALO
    vec_b = pl.BlockSpec((None, 1, D), lambda i: (i // tpb, 0, 0))
    vec = pl.BlockSpec((1, D), lambda i: (0, 0))
    row = pl.BlockSpec((tm, D), lambda i: (i, 0))
    return pl.pallas_call(
        functools.partial(_pool_kernel, tm=tm, tpb=tpb, alpha=alpha),
        grid=(M // tm,),
        in_specs=[row,
                  pl.BlockSpec((POOL_HALO, D), lambda i: (jnp.maximum(i * hb - 1, 0), 0)),
                  vec_b, vec_b,
                  pl.BlockSpec((G, dg, dg), lambda i: (0, 0, 0)),
                  vec, vec_b, vec, vec, vec_b, vec_b],
        out_specs=[row, row],
        out_shape=[jax.ShapeDtypeStruct((M, D), F32), jax.ShapeDtypeStruct((M, D), BF16)],
        scratch_shapes=[pltpu.VMEM((tm + POOL_HALO, D), F32), pltpu.VMEM((tm, D), F32)],
        compiler_params=_cparams(("parallel",)),
        name="pool_mixer",
    )(x, x, sc, sh, pool_w, pool_scale, gate, lng, lnb, sc2, sh2)


def _moba_kernel(q_ref, k_ref, v_ref, cos_ref, sin_ref, o_ref, *, nb):
    bs = MOBA_BLOCK
    cos = cos_ref[...]
    sin = sin_ref[...]

    def rope(t):
        return t * cos + pltpu.roll(t, MOBA_HEAD_DIM // 2, axis=1) * sin

    qr = rope(q_ref[...])
    kr = rope(k_ref[...])
    T = qr.shape[0]
    prow = lax.broadcasted_iota(jnp.int32, (LANES, T), 0)
    pcol = lax.broadcasted_iota(jnp.int32, (LANES, T), 1)
    pool = jnp.where(pcol // bs == prow, 1.0 / bs, 0.0).astype(F32)
    kmean = _dot(pool, kr, HI)
    qb = qr.astype(BF16)
    kb = kr.astype(BF16)
    vb = v_ref[...].astype(BF16)
    scale = MOBA_HEAD_DIM ** -0.5
    lane = lax.broadcasted_iota(jnp.int32, (bs, LANES), 1)
    rowi = lax.broadcasted_iota(jnp.int32, (bs, bs), 0)
    coli = lax.broadcasted_iota(jnp.int32, (bs, bs), 1)
    for i in range(nb):
        rows = slice(i * bs, (i + 1) * bs)
        qi = qb[rows]
        s_list = [jnp.where(coli <= rowi, _dot_nt(qi, kb[rows]) * scale, NEG_INF)]
        if i > 0:
            gate = _dot_nt(qr[rows], kmean, HI)
            valid = lane < i
            gate = jnp.where(valid, gate, -jnp.inf)
            rank = jnp.zeros((bs, LANES), jnp.int32)
            for j in range(i):
                gj = gate[:, j:j + 1]
                ahead = (gj > gate) | ((gj == gate) & (j < lane))
                rank = rank + ahead.astype(jnp.int32)
            sel = valid & (rank < MOBA_TOPK)
            for j in range(i):
                sj = _dot_nt(qi, kb[j * bs:(j + 1) * bs]) * scale
                s_list.append(jnp.where(sel[:, j:j + 1], sj, NEG_INF))
        m = s_list[0].max(axis=-1, keepdims=True)
        for s in s_list[1:]:
            m = jnp.maximum(m, s.max(axis=-1, keepdims=True))
        l = jnp.zeros((bs, 1), F32)
        acc = jnp.zeros((bs, MOBA_HEAD_DIM), F32)
        for jj, s in enumerate(s_list):
            j = i if jj == 0 else jj - 1
            p = jnp.exp(s - m)
            l = l + p.sum(axis=-1, keepdims=True)
            acc = acc + _dot(p.astype(BF16), vb[j * bs:(j + 1) * bs])
        o_ref[rows, :] = (acc / l).astype(o_ref.dtype)


def moba_attention(z, q_col, n_heads, B, T):
    assert T % MOBA_BLOCK == 0
    nb = T // MOBA_BLOCK
    dh = MOBA_HEAD_DIM
    inv = jnp.power(ROPE_THETA, -jnp.arange(0, dh, 2, dtype=F32) / dh)
    ang = jnp.arange(T, dtype=F32)[:, None] * inv[None, :]
    cos = jnp.concatenate([jnp.cos(ang), jnp.cos(ang)], axis=-1)
    sin = jnp.concatenate([-jnp.sin(ang), jnp.sin(ang)], axis=-1)

    def col(off):
        return pl.BlockSpec((None, T, dh), lambda b, h: (b, 0, q_col + off * n_heads + h))

    tab = pl.BlockSpec((T, dh), lambda b, h: (0, 0))
    return pl.pallas_call(
        functools.partial(_moba_kernel, nb=nb),
        grid=(B, n_heads),
        in_specs=[col(0), col(1), col(2), tab, tab],
        out_specs=pl.BlockSpec((None, T, dh), lambda b, h: (b, 0, h)),
        out_shape=jax.ShapeDtypeStruct((B, T, n_heads * dh), BF16),
        compiler_params=_cparams(("parallel", "parallel")),
        name="moba",
    )(z, z, z, cos, sin)


def _tri_inv(n, L):
    rowi = lax.broadcasted_iota(jnp.int32, (L, L), 0)
    coli = lax.broadcasted_iota(jnp.int32, (L, L), 1)
    t = jnp.where(rowi == coli, 1.0, 0.0) + n
    p = n
    for _ in range(int(math.log2(L)) - 1):
        p = _dot(p, p, HI)
        t = t + _dot(t, p, HI)
    return t


def _rwkv_kernel(zr_ref, zk_ref, zv_ref, zgl_ref, zwl_ref, zal_ref,
                 mur_ref, muk_ref, muv_ref, mugl_ref, muwl_ref, mual_ref,
                 w0_ref, a0_ref, kk_ref, ka_ref, rk_ref, lng_ref, lnb_ref,
                 w2_ref, a2_ref, g2_ref,
                 o_ref,
                 s_ref, pr_ref, pk_ref, pv_ref, pgl_ref, pwl_ref, pal_ref):
    L = zr_ref.shape[0]
    hd = RWKV_HEAD_DIM
    c = pl.program_id(2)

    @pl.when(c == 0)
    def _():
        s_ref[...] = jnp.zeros_like(s_ref)
        for p in (pr_ref, pk_ref, pv_ref, pgl_ref, pwl_ref, pal_ref):
            p[...] = jnp.zeros_like(p)

    def shift(z_ref, p_ref, mu_ref):
        z = z_ref[...]
        row = lax.broadcasted_iota(jnp.int32, z.shape, 0)
        zp = jnp.where(row == 0, p_ref[0:1, :], pltpu.roll(z, 1, axis=0))
        p_ref[0:1, :] = z[L - 1:L, :]
        return z + (zp - z) * mu_ref[...]

    r = shift(zr_ref, pr_ref, mur_ref)
    k = shift(zk_ref, pk_ref, muk_ref)
    v = shift(zv_ref, pv_ref, muv_ref)
    gl = shift(zgl_ref, pgl_ref, mugl_ref)
    wl = shift(zwl_ref, pwl_ref, muwl_ref)
    al = shift(zal_ref, pal_ref, mual_ref)

    lane_r = lax.broadcasted_iota(jnp.int32, (LANES, LANES), 0)
    lane_c = lax.broadcasted_iota(jnp.int32, (LANES, LANES), 1)
    same_head = (lane_r // hd) == (lane_c // hd)
    ones_bd = jnp.where(same_head, 1.0, 0.0).astype(F32)

    wpre = w0_ref[...] + _dot(jnp.tanh(wl), w2_ref[...], HI)
    nw = -wpre
    softplus = jnp.maximum(nw, 0.0) + jnp.log(1.0 + jnp.exp(-jnp.abs(nw)))
    wlog = -softplus - 0.5
    a = jax.nn.sigmoid(a0_ref[...] + _dot(al, a2_ref[...], HI))
    g = _dot(jax.nn.sigmoid(gl), g2_ref[...], HI)
    kkv = k * kk_ref[...]
    kkn = kkv / jnp.maximum(jnp.sqrt(_dot(kkv * kkv, ones_bd, HI)), 1e-12)
    k2 = k * (1.0 + (a - 1.0) * ka_ref[...])
    ld = -jnp.exp(wlog)

    rowi = lax.broadcasted_iota(jnp.int32, (L, L), 0)
    coli = lax.broadcasted_iota(jnp.int32, (L, L), 1)
    strict = rowi > coli
    incl = rowi >= coli
    cum = _dot(jnp.where(incl, 1.0, 0.0).astype(F32), ld, HI)
    mid = L // 2 - 1
    cmid = cum[mid:mid + 1, :]
    cc = cum - cmid
    e_pos = jnp.exp(cc)
    e_neg = jnp.exp(-cc)
    at = -kkn * jnp.exp(cc - ld)
    bt = kkn * a * e_neg
    kt = k2 * e_neg
    rt = r * e_pos
    sr = s_ref[...] * jnp.exp(cmid)
    head0 = lax.broadcasted_iota(jnp.int32, (1, LANES), 1) < hd

    rhs0 = _dot_nt(at, sr, HI)
    y0 = _dot_nt(rt, sr, HI)
    us, ys = [], []
    for hmask in (head0, jnp.logical_not(head0)):
        ah = jnp.where(hmask, at, 0.0)
        rh = jnp.where(hmask, rt, 0.0)
        n_ab = jnp.where(strict, _dot_nt(ah, bt, HI), 0.0)
        n_ak = jnp.where(strict, _dot_nt(ah, kt, HI), 0.0)
        m_rb = jnp.where(incl, _dot_nt(rh, bt, HI), 0.0)
        m_rk = jnp.where(incl, _dot_nt(rh, kt, HI), 0.0)
        u_h = _dot(_tri_inv(n_ab, L), rhs0 + _dot(n_ak, v, HI), HI)
        us.append(u_h)
        ys.append(y0 + _dot(jnp.concatenate([m_rb, m_rk], axis=1), jnp.concatenate([u_h, v], axis=0), HI))
    u = jnp.where(head0, us[0], us[1])
    y = jnp.where(head0, ys[0], ys[1])

    s_inc = _dot_tn(jnp.concatenate([u, v], axis=0), jnp.concatenate([bt, kt], axis=0), HI)
    s_new = (sr + s_inc) * jnp.exp(cc[L - 1:L, :])
    s_ref[...] = jnp.where(same_head, s_new, 0.0)

    mean_bd = ones_bd * (1.0 / hd)
    mu = _dot(y, mean_bd, HI)
    d = y - mu
    var = _dot(d * d, mean_bd, HI)
    yn = d * lax.rsqrt(var + RWKV_LN_EPS) * lng_ref[...] + lnb_ref[...]
    bonus = _dot(r * k2 * rk_ref[...], ones_bd, HI) * v
    o_ref[...] = ((yn + bonus) * g).astype(o_ref.dtype)


def rwkv_time_mix(z, mu_p, w0, w2, a0, a2, g2p, k_k, k_a, r_k, lnx_g, lnx_b, B, T, RW):
    L = RWKV_CHUNK
    W = LANES
    npair = RW // W
    c_gl = 3 * RW // GATE_LORA_PAD
    c_wl = (3 * RW + GATE_LORA_PAD) // W
    c_al = c_wl + 1

    def zcol(base, width=W):
        return pl.BlockSpec((None, L, width), lambda b, p, c: (b, c, base + p))

    def zfix(idx, width):
        return pl.BlockSpec((None, L, width), lambda b, p, c: (b, c, idx))

    def mucol(base):
        return pl.BlockSpec((1, W), lambda b, p, c: (0, base + p))

    def mufix(idx, width):
        return pl.BlockSpec((1, width), lambda b, p, c: (0, idx))

    pvec = pl.BlockSpec((1, W), lambda b, p, c: (0, p))

    def lora(rank):
        return pl.BlockSpec((rank, W), lambda b, p, c: (0, p))

    row = lambda a: a.reshape(1, -1)
    return pl.pallas_call(
        _rwkv_kernel,
        grid=(B, npair, T // L),
        in_specs=[zcol(0), zcol(npair), zcol(2 * npair),
                  zfix(c_gl, GATE_LORA_PAD), zfix(c_wl, W), zfix(c_al, W),
                  mucol(0), mucol(npair), mucol(2 * npair),
                  mufix(c_gl, GATE_LORA_PAD), mufix(c_wl, W), mufix(c_al, W),
                  pvec, pvec, pvec, pvec, pvec, pvec, pvec,
                  lora(DECAY_LORA), lora(ICLR_LORA), lora(GATE_LORA_PAD)],
        out_specs=pl.BlockSpec((None, L, W), lambda b, p, c: (b, c, p)),
        out_shape=jax.ShapeDtypeStruct((B, T, RW), BF16),
        scratch_shapes=[pltpu.VMEM((W, W), F32)] + [pltpu.VMEM((8, W), F32)] * 3
        + [pltpu.VMEM((8, GATE_LORA_PAD), F32), pltpu.VMEM((8, W), F32), pltpu.VMEM((8, W), F32)],
        compiler_params=_cparams(("parallel", "parallel", "arbitrary")),
        name="rwkv7",
    )(z, z, z, z, z, z, mu_p, mu_p, mu_p, mu_p, mu_p, mu_p,
      row(w0), row(a0), row(k_k), row(k_a), row(r_k), row(lnx_g), row(lnx_b), w2, a2, g2p)


def _pad_cols(w, n):
    return jnp.pad(w, ((0, 0), (0, n - w.shape[1])))


def kernel(x, c, ada_w, ada_b, ada_table, ln_g, ln_b, mix_w_in, mix_mu, rwkv_w0, rwkv_w2, rwkv_a0, rwkv_a2,
           rwkv_g2, rwkv_kk, rwkv_ka, rwkv_rk, rwkv_lnx_g, rwkv_lnx_b, mix_w_out, ffn_w_gate, ffn_w_up,
           ffn_w_down, pool_w, pool_scale, moe_router_w, moe_router_b, moe_w_gate, moe_w_up, moe_w_down):
    B, T, D = x.shape
    M = B * T
    depth = ada_table.shape[0]
    RW = rwkv_w0.shape[1]
    MW = D - RW
    n_moba = MW // MOBA_HEAD_DIM
    gate_lora = rwkv_g2.shape[1]
    alpha = (2 * depth) ** 0.25

    mods = ada_mod(c, ada_w, ada_b, ada_table)
    xf = x.reshape(M, D)
    h = modulate(xf, mods[0, :, 1], mods[0, :, 0], T)
    lnrow = lambda a: a.reshape(1, D)

    za = 3 * RW + GATE_LORA_PAD + DECAY_LORA + ICLR_LORA
    n_in = za + 3 * MW
    n_pad = -(-n_in // 512) * 512
    o_wl = 3 * RW
    o_al = o_wl + DECAY_LORA
    o_gl = o_al + ICLR_LORA
    o_q = o_gl + gate_lora

    def permute_cols(w):
        return jnp.concatenate(
            [w[..., :3 * RW], _pad_cols(w[..., o_gl:o_q], GATE_LORA_PAD), w[..., o_wl:o_al], w[..., o_al:o_gl],
             _pad_cols(w[..., o_q:], n_pad - za)], axis=-1)

    for l in range(depth):
        i = l // 2
        m = mods[l]
        g1, sh2, sc2, g2 = m[:, 2], m[:, 3], m[:, 4], m[:, 5]
        if l + 1 < depth:
            sh_n, sc_n = mods[l + 1, :, 0], mods[l + 1, :, 1]
        else:
            sh_n, sc_n = jnp.zeros_like(sh2), jnp.zeros_like(sc2)
        if l % 2 == 0:
            w_in = permute_cols(mix_w_in[i]).astype(BF16)
            mu_p = permute_cols(jnp.pad(mix_mu[i], (0, mix_w_in.shape[2] - mix_mu.shape[1]))[None, :])
            z = matmul(h, w_in, 512, 512).reshape(B, T, n_pad)
            g2p = jnp.pad(rwkv_g2[i], ((0, GATE_LORA_PAD - gate_lora), (0, 0)))
            ya = rwkv_time_mix(z, mu_p, rwkv_w0[i], rwkv_w2[i], rwkv_a0[i], rwkv_a2[i], g2p, rwkv_kk[i],
                               rwkv_ka[i], rwkv_rk[i], rwkv_lnx_g[i], rwkv_lnx_b[i], B, T, RW)
            yb = moba_attention(z, za // LANES, n_moba, B, T)
            ycat = jnp.concatenate([ya, yb], axis=-1).reshape(M, D)
            xf, h = matmul_resid_ln(ycat, mix_w_out[i].astype(BF16), xf, g1, lnrow(ln_g[l, 0]), lnrow(ln_b[l, 0]),
                                    sc2, sh2, T, alpha, 256, 512)
            F = ffn_w_gate.shape[2]
            fp = -(-F // 512) * 512
            act = swiglu_up(h, _pad_cols(ffn_w_gate[i], fp).astype(BF16), _pad_cols(ffn_w_up[i], fp).astype(BF16),
                            1024, 512)
            wd = jnp.pad(ffn_w_down[i], ((0, fp - F), (0, 0))).astype(BF16)
            xf, h = matmul_resid_ln(act, wd, xf, g2, lnrow(ln_g[l, 1]), lnrow(ln_b[l, 1]), sc_n, sh_n, T, alpha,
                                    256, 512)
        else:
            xf, h = pool_mixer_ln(xf, m[:, 1], m[:, 0], pool_w[i].astype(BF16), lnrow(pool_scale[i]), g1,
                                  lnrow(ln_g[l, 0]), lnrow(ln_b[l, 0]), sc2, sh2, T, alpha)
            comb = moe_router(xf, sc2, sh2, moe_router_w[i], moe_router_b[i], T)
            E, _, NE = moe_w_gate.shape[1:]
            act = moe_up_dense(h, comb, moe_w_gate[i].astype(BF16), moe_w_up[i].astype(BF16), 1024, NE // 7 if NE % 7 == 0 and (NE // 7) % LANES == 0 else NE)
            wd = moe_w_down[i].reshape(E * NE, D).astype(BF16)
            tk = NE // 2 if (NE // 2) % LANES == 0 else NE
            xf, h = matmul_resid_ln(act, wd, xf, g2, lnrow(ln_g[l, 1]), lnrow(ln_b[l, 1]), sc_n, sh_n, T, alpha,
                                    256, tk)
    return xf.reshape(B, T, D)
```

```python
import functools
import math

import jax
import jax.numpy as jnp
from jax import lax
from jax.experimental import pallas as pl
from jax.experimental.pallas import tpu as pltpu

F32 = jnp.float32
BF16 = jnp.bfloat16
HI = lax.Precision.HIGHEST

RWKV_HEAD_DIM = 64
DECAY_LORA = 128
ICLR_LORA = 128
GATE_LORA_PAD = 512
RWKV_LN_EPS = 64e-5
MOBA_HEAD_DIM = 128
MOBA_BLOCK = 256
MOBA_TOPK = 3
ROPE_THETA = 10000.0
POOL_WINDOWS = (2, 4, 8, 16)
POOL_HALO = 16
MOE_TOPK = 2
N_MOD = 6
LN_EPS = 1e-5
NEG_INF = -1e30

LANES = 128
LN_ROWS = 64
RWKV_CHUNK = 64
RWKV_BLOCK_CHUNKS = 4
RWKV_BLOCK_PAIRS = 2
P_LORA_W, P_LORA, P_SCORE, P_TRI, P_MAIN, P_STATE = 1, 1, 1, 3, 1, 1
VMEM_LIMIT = 56 * 1024 * 1024


def _cparams(sem):
    return pltpu.CompilerParams(dimension_semantics=sem, vmem_limit_bytes=VMEM_LIMIT)


def _dot(a, b, precision=None):
    return jnp.dot(a, b, preferred_element_type=F32, precision=precision)


def _dot_nt(a, b, precision=None):
    return lax.dot_general(a, b, (((1,), (1,)), ((), ())), preferred_element_type=F32, precision=precision)


def _dot_tn(a, b, precision=None):
    return lax.dot_general(a, b, (((0,), (0,)), ((), ())), preferred_element_type=F32, precision=precision)


def _ada_kernel(c_ref, w_ref, b_ref, tab_ref, o_ref):
    c = c_ref[...]
    sc = (c * jax.nn.sigmoid(c)).astype(BF16)
    acc = _dot(sc, w_ref[...].astype(BF16)) + b_ref[...]
    o_ref[...] = acc[None, :, :] + tab_ref[...][:, None, :]


def ada_mod(c, ada_w, ada_b, ada_table):
    B, D = c.shape
    N = ada_w.shape[1]
    depth = ada_table.shape[0]
    BP = 8
    cp = jnp.zeros((BP, D), F32).at[:B].set(c)
    tn = 512
    out = pl.pallas_call(
        _ada_kernel,
        grid=(N // tn,),
        in_specs=[
            pl.BlockSpec((BP, D), lambda j: (0, 0)),
            pl.BlockSpec((D, tn), lambda j: (0, j)),
            pl.BlockSpec((1, tn), lambda j: (0, j)),
            pl.BlockSpec((depth, tn), lambda j: (0, j)),
        ],
        out_specs=pl.BlockSpec((depth, BP, tn), lambda j: (0, 0, j)),
        out_shape=jax.ShapeDtypeStruct((depth, BP, N), F32),
        compiler_params=_cparams(("parallel",)),
        name="ada_mod",
    )(cp, ada_w, ada_b.reshape(1, N), ada_table.reshape(depth, N))
    return out[:, :B].reshape(depth, B, N_MOD, 1, D)


def _modulate_kernel(x_ref, sc_ref, sh_ref, o_ref):
    o_ref[...] = (x_ref[...] * (1.0 + sc_ref[...]) + sh_ref[...]).astype(o_ref.dtype)


def modulate(x, sc, sh, T):
    M, D = x.shape
    tm = 512
    tpb = T // tm
    vec = pl.BlockSpec((None, 1, D), lambda i: (i // tpb, 0, 0))
    return pl.pallas_call(
        _modulate_kernel,
        grid=(M // tm,),
        in_specs=[pl.BlockSpec((tm, D), lambda i: (i, 0)), vec, vec],
        out_specs=pl.BlockSpec((tm, D), lambda i: (i, 0)),
        out_shape=jax.ShapeDtypeStruct((M, D), BF16),
        compiler_params=_cparams(("parallel",)),
        name="modulate",
    )(x, sc, sh)


def _mm_kernel(a_ref, w_ref, o_ref):
    o_ref[...] = _dot(a_ref[...], w_ref[...]).astype(o_ref.dtype)


def matmul(a, w, tm, tn, out_dtype=F32):
    M, K = a.shape
    N = w.shape[1]
    return pl.pallas_call(
        _mm_kernel,
        grid=(N // tn, M // tm),
        in_specs=[pl.BlockSpec((tm, K), lambda j, i: (i, 0)),
                  pl.BlockSpec((K, tn), lambda j, i: (0, j))],
        out_specs=pl.BlockSpec((tm, tn), lambda j, i: (i, j)),
        out_shape=jax.ShapeDtypeStruct((M, N), out_dtype),
        compiler_params=_cparams(("parallel", "parallel")),
        name="matmul",
    )(a, w)


def _resid_ln_mod(x, y, gate, lng, lnb, sc, sh, alpha):
    r = alpha * x + gate * y
    mu = jnp.mean(r, axis=-1, keepdims=True)
    d = r - mu
    var = jnp.mean(d * d, axis=-1, keepdims=True)
    xn = d * lax.rsqrt(var + LN_EPS) * lng + lnb
    return xn, xn * (1.0 + sc) + sh


def _mm_ln_kernel(a_ref, w_ref, x_ref, gate_ref, lng_ref, lnb_ref, sc_ref, sh_ref,
                  xo_ref, ho_ref, *, alpha):
    k = pl.program_id(1)

    @pl.when(k == 0)
    def _():
        xo_ref[...] = _dot(a_ref[...], w_ref[...])

    @pl.when(k > 0)
    def _():
        xo_ref[...] += _dot(a_ref[...], w_ref[...])

    @pl.when(k == pl.num_programs(1) - 1)
    def _():
        for r0 in range(0, xo_ref.shape[0], LN_ROWS):
            rows = pl.ds(r0, LN_ROWS)
            xn, h = _resid_ln_mod(x_ref[rows, :], xo_ref[rows, :], gate_ref[...], lng_ref[...], lnb_ref[...],
                                  sc_ref[...], sh_ref[...], alpha)
            xo_ref[rows, :] = xn
            ho_ref[rows, :] = h.astype(ho_ref.dtype)


def matmul_resid_ln(a, w, x, gate, lng, lnb, sc, sh, T, alpha, tm, tk):
    M, K = a.shape
    D = w.shape[1]
    tpb = T // tm
    vec_b = pl.BlockSpec((None, 1, D), lambda i, k: (i // tpb, 0, 0))
    vec = pl.BlockSpec((1, D), lambda i, k: (0, 0))
    row = pl.BlockSpec((tm, D), lambda i, k: (i, 0))
    return pl.pallas_call(
        functools.partial(_mm_ln_kernel, alpha=alpha),
        grid=(M // tm, K // tk),
        in_specs=[pl.BlockSpec((tm, tk), lambda i, k: (i, k)),
                  pl.BlockSpec((tk, D), lambda i, k: (k, 0)),
                  pl.BlockSpec((tm, D), lambda i, k: (i, 0), pipeline_mode=pl.Buffered(1)),
                  vec_b, vec, vec, vec_b, vec_b],
        out_specs=[row, row],
        out_shape=[jax.ShapeDtypeStruct((M, D), F32), jax.ShapeDtypeStruct((M, D), BF16)],
        compiler_params=_cparams(("parallel", "arbitrary")),
        name="matmul_resid_ln",
    )(a, w, x, gate, lng, lnb, sc, sh)


def _swiglu_up_kernel(a_ref, wg_ref, wu_ref, o_ref):
    a = a_ref[...]
    g = _dot(a, wg_ref[...])
    u = _dot(a, wu_ref[...])
    o_ref[...] = (g * jax.nn.sigmoid(g) * u).astype(o_ref.dtype)


def swiglu_up(a, wg, wu, tm, tn):
    M, K = a.shape
    N = wg.shape[1]
    wspec = pl.BlockSpec((K, tn), lambda i, j: (0, j))
    return pl.pallas_call(
        _swiglu_up_kernel,
        grid=(M // tm, N // tn),
        in_specs=[pl.BlockSpec((tm, K), lambda i, j: (i, 0)), wspec, wspec],
        out_specs=pl.BlockSpec((tm, tn), lambda i, j: (i, j)),
        out_shape=jax.ShapeDtypeStruct((M, N), BF16),
        compiler_params=_cparams(("parallel", "parallel")),
        name="swiglu_up",
    )(a, wg, wu)


def _router_kernel(x_ref, sc_ref, sh_ref, rw_ref, rb_ref, o_ref, *, n_experts):
    h = x_ref[...] * (1.0 + sc_ref[...]) + sh_ref[...]
    logits = _dot(h, rw_ref[...], HI) + rb_ref[...]
    lane = lax.broadcasted_iota(jnp.int32, logits.shape, 1)
    logits = jnp.where(lane < n_experts, logits, -jnp.inf)
    m1 = jnp.max(logits, axis=-1, keepdims=True)
    i1 = jnp.min(jnp.where(logits == m1, lane, LANES), axis=-1, keepdims=True)
    oh1 = lane == i1
    rest = jnp.where(oh1, -jnp.inf, logits)
    m2 = jnp.max(rest, axis=-1, keepdims=True)
    i2 = jnp.min(jnp.where(rest == m2, lane, LANES), axis=-1, keepdims=True)
    oh2 = lane == i2
    e = jnp.exp(m2 - m1)
    g1 = 1.0 / (1.0 + e)
    g2 = e / (1.0 + e)
    o_ref[...] = jnp.where(oh1, g1, 0.0) + jnp.where(oh2, g2, 0.0)


def moe_router(x, sc, sh, router_w, router_b, T):
    M, D = x.shape
    E = router_w.shape[1]
    tm = 512
    tpb = T // tm
    rw = jnp.zeros((D, LANES), F32).at[:, :E].set(router_w)
    rb = jnp.zeros((1, LANES), F32).at[0, :E].set(router_b)
    vec = pl.BlockSpec((None, 1, D), lambda i: (i // tpb, 0, 0))
    return pl.pallas_call(
        functools.partial(_router_kernel, n_experts=E),
        grid=(M // tm,),
        in_specs=[pl.BlockSpec((tm, D), lambda i: (i, 0)), vec, vec,
                  pl.BlockSpec((D, LANES), lambda i: (0, 0)),
                  pl.BlockSpec((1, LANES), lambda i: (0, 0))],
        out_specs=pl.BlockSpec((tm, LANES), lambda i: (i, 0)),
        out_shape=jax.ShapeDtypeStruct((M, LANES), F32),
        compiler_params=_cparams(("parallel",)),
        name="moe_router",
    )(x, sc, sh, rw, rb)


def _moe_up_kernel(a_ref, comb_ref, wg_ref, wu_ref, o_ref):
    e = pl.program_id(1)
    a = a_ref[...]
    g = _dot(a, wg_ref[...])
    u = _dot(a, wu_ref[...])
    comb = comb_ref[...]
    lane = lax.broadcasted_iota(jnp.int32, comb.shape, 1)
    ce = jnp.sum(jnp.where(lane == e, comb, 0.0), axis=-1, keepdims=True)
    o_ref[...] = (g * jax.nn.sigmoid(g) * u * ce).astype(o_ref.dtype)


def moe_up_dense(a, comb, wg, wu, tm, tn):
    M, K = a.shape
    E, _, NE = wg.shape
    nt = NE // tn
    wspec = pl.BlockSpec((None, K, tn), lambda i, e, j: (e, 0, j))
    return pl.pallas_call(
        _moe_up_kernel,
        grid=(M // tm, E, nt),
        in_specs=[pl.BlockSpec((tm, K), lambda i, e, j: (i, 0)),
                  pl.BlockSpec((tm, LANES), lambda i, e, j: (i, 0)), wspec, wspec],
        out_specs=pl.BlockSpec((tm, tn), lambda i, e, j: (i, e * nt + j)),
        out_shape=jax.ShapeDtypeStruct((M, E * NE), BF16),
        compiler_params=_cparams(("parallel", "parallel", "parallel")),
        name="moe_up",
    )(a, comb, wg, wu)


def _pool_kernel(x_ref, xh_ref, sc_ref, sh_ref, w_ref, ps_ref, gate_ref, lng_ref, lnb_ref,
                 sc2_ref, sh2_ref, xo_ref, ho_ref, hh_ref, y_ref, *, tm, tpb, alpha):
    i = pl.program_id(0)
    D = x_ref.shape[1]
    dg = D // len(POOL_WINDOWS)
    sc = sc_ref[...]
    sh = sh_ref[...]
    x = x_ref[...]
    first = (i % tpb) == 0
    halo = xh_ref[...] * (1.0 + sc) + sh
    hh_ref[0:POOL_HALO, :] = jnp.where(first, 0.0, halo)
    hh_ref[POOL_HALO:, :] = x * (1.0 + sc) + sh
    t = (i % tpb) * tm + lax.broadcasted_iota(jnp.int32, (tm, 1), 0)
    tcnt = (t + 1).astype(F32)
    for g, win in enumerate(POOL_WINDOWS):
        cols = slice(g * dg, (g + 1) * dg)
        h = hh_ref[POOL_HALO:, cols]
        s = h
        for d in range(1, win):
            s = s + hh_ref[pl.ds(POOL_HALO - d, tm), cols]
        inv = 1.0 / jnp.minimum(tcnt, float(win))
        p = (s * inv - h).astype(BF16)
        y_ref[:, cols] = _dot(p, w_ref[g])
    y = y_ref[...] * ps_ref[...]
    xn, hn = _resid_ln_mod(x, y, gate_ref[...], lng_ref[...], lnb_ref[...], sc2_ref[...], sh2_ref[...], alpha)
    xo_ref[...] = xn
    ho_ref[...] = hn.astype(ho_ref.dtype)


def pool_mixer_ln(x, sc, sh, pool_w, pool_scale, gate, lng, lnb, sc2, sh2, T, alpha):
    M, D = x.shape
    G, dg, _ = pool_w.shape
    tm = 256
    tpb = T // tm
    hb = tm // POOL_HALO
    vec_b = pl.BlockSpec((None, 1, D), lambda i: (i // tpb, 0, 0))
    vec = pl.BlockSpec((1, D), lambda i: (0, 0))
    row = pl.BlockSpec((tm, D), lambda i: (i, 0))
    return pl.pallas_call(
        functools.partial(_pool_kernel, tm=tm, tpb=tpb, alpha=alpha),
        grid=(M // tm,),
        in_specs=[row,
                  pl.BlockSpec((POOL_HALO, D), lambda i: (jnp.maximum(i * hb - 1, 0), 0)),
                  vec_b, vec_b,
                  pl.BlockSpec((G, dg, dg), lambda i: (0, 0, 0)),
                  vec, vec_b, vec, vec, vec_b, vec_b],
        out_specs=[row, row],
        out_shape=[jax.ShapeDtypeStruct((M, D), F32), jax.ShapeDtypeStruct((M, D), BF16)],
        scratch_shapes=[pltpu.VMEM((tm + POOL_HALO, D), F32), pltpu.VMEM((tm, D), F32)],
        compiler_params=_cparams(("parallel",)),
        name="pool_mixer",
    )(x, x, sc, sh, pool_w, pool_scale, gate, lng, lnb, sc2, sh2)


def _moba_kernel(q_ref, k_ref, v_ref, cos_ref, sin_ref, o_ref, *, nb):
    bs = MOBA_BLOCK
    cos = cos_ref[...]
    sin = sin_ref[...]

    def rope(t):
        return t * cos + pltpu.roll(t, MOBA_HEAD_DIM // 2, axis=1) * sin

    qr = rope(q_ref[...])
    kr = rope(k_ref[...])
    T = qr.shape[0]
    prow = lax.broadcasted_iota(jnp.int32, (LANES, T), 0)
    pcol = lax.broadcasted_iota(jnp.int32, (LANES, T), 1)
    pool = jnp.where(pcol // bs == prow, 1.0 / bs, 0.0).astype(F32)
    kmean = _dot(pool, kr, HI)
    qb = qr.astype(BF16)
    kb = kr.astype(BF16)
    vb = v_ref[...].astype(BF16)
    scale = MOBA_HEAD_DIM ** -0.5
    lane = lax.broadcasted_iota(jnp.int32, (bs, LANES), 1)
    rowi = lax.broadcasted_iota(jnp.int32, (bs, bs), 0)
    coli = lax.broadcasted_iota(jnp.int32, (bs, bs), 1)
    for i in range(nb):
        rows = slice(i * bs, (i + 1) * bs)
        qi = qb[rows]
        s_list = [jnp.where(coli <= rowi, _dot_nt(qi, kb[rows]) * scale, NEG_INF)]
        if i > 0:
            gate = _dot_nt(qr[rows], kmean, HI)
            valid = lane < i
            gate = jnp.where(valid, gate, -jnp.inf)
            rank = jnp.zeros((bs, LANES), jnp.int32)
            for j in range(i):
                gj = gate[:, j:j + 1]
                ahead = (gj > gate) | ((gj == gate) & (j < lane))
                rank = rank + ahead.astype(jnp.int32)
            sel = valid & (rank < MOBA_TOPK)
            for j in range(i):
                sj = _dot_nt(qi, kb[j * bs:(j + 1) * bs]) * scale
                s_list.append(jnp.where(sel[:, j:j + 1], sj, NEG_INF))
        m = s_list[0].max(axis=-1, keepdims=True)
        for s in s_list[1:]:
            m = jnp.maximum(m, s.max(axis=-1, keepdims=True))
        l = jnp.zeros((bs, 1), F32)
        acc = jnp.zeros((bs, MOBA_HEAD_DIM), F32)
        for jj, s in enumerate(s_list):
            j = i if jj == 0 else jj - 1
            p = jnp.exp(s - m)
            l = l + p.sum(axis=-1, keepdims=True)
            acc = acc + _dot(p.astype(BF16), vb[j * bs:(j + 1) * bs])
        o_ref[rows, :] = (acc / l).astype(o_ref.dtype)


def moba_attention(z, q_col, n_heads, B, T):
    assert T % MOBA_BLOCK == 0
    nb = T // MOBA_BLOCK
    dh = MOBA_HEAD_DIM
    inv = jnp.power(ROPE_THETA, -jnp.arange(0, dh, 2, dtype=F32) / dh)
    ang = jnp.arange(T, dtype=F32)[:, None] * inv[None, :]
    cos = jnp.concatenate([jnp.cos(ang), jnp.cos(ang)], axis=-1)
    sin = jnp.concatenate([-jnp.sin(ang), jnp.sin(ang)], axis=-1)

    def col(off):
        return pl.BlockSpec((None, T, dh), lambda b, h: (b, 0, q_col + off * n_heads + h))

    tab = pl.BlockSpec((T, dh), lambda b, h: (0, 0))
    return pl.pallas_call(
        functools.partial(_moba_kernel, nb=nb),
        grid=(B, n_heads),
        in_specs=[col(0), col(1), col(2), tab, tab],
        out_specs=pl.BlockSpec((None, T, dh), lambda b, h: (b, 0, h)),
        out_shape=jax.ShapeDtypeStruct((B, T, n_heads * dh), BF16),
        compiler_params=_cparams(("parallel", "parallel")),
        name="moba",
    )(z, z, z, cos, sin)


def _split_bf16(x, n):
    parts, rem = [], x
    for i in range(n):
        p = rem.astype(BF16)
        parts.append(p)
        if i + 1 < n:
            rem = rem - p.astype(F32)
    return parts


_DIMS = {"nn": ((1,), (0,)), "nt": ((1,), (1,)), "tn": ((0,), (0,))}


def _mm(a, b, form="nn", passes=1):
    if passes == 1:
        a3, b3 = a.astype(BF16), b.astype(BF16)
    else:
        ah, al = _split_bf16(a, 2)
        bh, bl = _split_bf16(b, 2)
        a3 = jnp.concatenate([ah, ah, al], axis=0 if form == "tn" else 1)
        b3 = jnp.concatenate([bh, bl, bh], axis=1 if form == "nt" else 0)
    return lax.dot_general(a3, b3, (_DIMS[form], ((), ())), preferred_element_type=F32)


def _mm_exact_rhs(a, b_bf16):
    a3 = jnp.concatenate(_split_bf16(a, 3), axis=1)
    b3 = jnp.concatenate([b_bf16, b_bf16, b_bf16], axis=0)
    return _dot(a3, b3)


def _rwkv_kernel(zr_ref, zk_ref, zv_ref, zgl_ref, zwl_ref, zal_ref,
                 mur_ref, muk_ref, muv_ref, mugl_ref, muwl_ref, mual_ref,
                 w0_ref, a0_ref, kk_ref, ka_ref, rk_ref, lng_ref, lnb_ref,
                 w2_ref, a2_ref, g2_ref,
                 o_ref,
                 s_ref, pr_ref, pk_ref, pv_ref, pgl_ref, pwl_ref, pal_ref, *, L):
    R = zr_ref.shape[0]
    hd = RWKV_HEAD_DIM
    c = pl.program_id(2)

    @pl.when(c == 0)
    def _():
        s_ref[...] = jnp.zeros_like(s_ref)
        for p in (pr_ref, pk_ref, pv_ref, pgl_ref, pwl_ref, pal_ref):
            p[...] = jnp.zeros_like(p)

    def shift(z_ref, p_ref, mu_ref):
        z = z_ref[...]
        row = lax.broadcasted_iota(jnp.int32, z.shape, 0)
        zp = jnp.where(row == 0, p_ref[0:1, :], pltpu.roll(z, 1, axis=0))
        p_ref[0:1, :] = z[R - 1:R, :]
        return z + (zp - z) * mu_ref[...]

    r = shift(zr_ref, pr_ref, mur_ref)
    k = shift(zk_ref, pk_ref, muk_ref)
    v = shift(zv_ref, pv_ref, muv_ref)
    gl = shift(zgl_ref, pgl_ref, mugl_ref)
    wl = shift(zwl_ref, pwl_ref, muwl_ref)
    al = shift(zal_ref, pal_ref, mual_ref)

    WB = zr_ref.shape[1]
    lane_r = lax.broadcasted_iota(jnp.int32, (WB, WB), 0)
    lane_c = lax.broadcasted_iota(jnp.int32, (WB, WB), 1)
    same_head_b = (lane_r // hd) == (lane_c // hd)
    ones_bd = jnp.where(same_head_b, 1.0, 0.0).astype(BF16)
    mean_bd = jnp.where(same_head_b, 1.0 / hd, 0.0).astype(BF16)
    same_head = same_head_b[0:LANES, 0:LANES]
    head0 = lax.broadcasted_iota(jnp.int32, (1, LANES), 1) < hd
    not_head0 = jnp.logical_not(head0)

    wpre = w0_ref[...] + _mm(jnp.tanh(wl), w2_ref[...], "nn", P_LORA_W)
    nw = -wpre
    softplus = jnp.maximum(nw, 0.0) + jnp.log(1.0 + jnp.exp(-jnp.abs(nw)))
    wlog = -softplus - 0.5
    a = jax.nn.sigmoid(a0_ref[...] + _mm(al, a2_ref[...], "nn", P_LORA))
    g = _mm(jax.nn.sigmoid(gl), g2_ref[...], "nn", P_LORA)
    kkv = k * kk_ref[...]
    kkn = kkv / jnp.maximum(jnp.sqrt(_mm_exact_rhs(kkv * kkv, ones_bd)), 1e-12)
    k2 = k * (1.0 + (a - 1.0) * ka_ref[...])
    ld = -jnp.exp(wlog)
    bv = kkn * a

    rr = lax.broadcasted_iota(jnp.int32, (R, R), 0)
    rc = lax.broadcasted_iota(jnp.int32, (R, R), 1)
    ltri = jnp.where((rr // L == rc // L) & (rr >= rc), 1.0, 0.0).astype(BF16)
    cum3 = _dot(ltri, jnp.concatenate(_split_bf16(ld, 3), axis=1))
    cum = cum3[:, 0:WB] + cum3[:, WB:2 * WB] + cum3[:, 2 * WB:3 * WB]

    S2 = 2 * L
    srow = lax.broadcasted_iota(jnp.int32, (S2, S2), 0)
    scol = lax.broadcasted_iota(jnp.int32, (S2, S2), 1)
    same_blk = (srow // L) == (scol // L)
    strict_bd = same_blk & (srow > scol)
    incl_bd = same_blk & (srow >= scol)
    tri_masks = []
    s = 1
    while s < L:
        tri_masks.append((srow // (2 * s) == scol // (2 * s)) & ((srow // s) % 2 == 1) & ((scol // s) % 2 == 0))
        s *= 2
    mid = L // 2 - 1
    zeros_l = jnp.zeros((L, LANES), F32)

    npb = WB // LANES
    nch = R // L
    streams = [(pp, j) for pp in range(npb) for j in range(nch)]
    eye = jnp.where(srow == scol, 1.0, 0.0)

    def piece(x, pp, j):
        return x[j * L:(j + 1) * L, pp * LANES:(pp + 1) * LANES]

    at_l, bt_l, kt_l, rt_l, v_l, rho_l, gam_l = [], [], [], [], [], [], []
    for pp, j in streams:
        cumj = piece(cum, pp, j)
        cmid = cumj[mid:mid + 1, :]
        cc = cumj - cmid
        e_neg = jnp.exp(-cc)
        at_l.append(-piece(kkn, pp, j) * jnp.exp(cc - piece(ld, pp, j)))
        bt_l.append(piece(bv, pp, j) * e_neg)
        kt_l.append(piece(k2, pp, j) * e_neg)
        rt_l.append(piece(r, pp, j) * jnp.exp(cc))
        v_l.append(piece(v, pp, j))
        rho_l.append(jnp.exp(cmid))
        gam_l.append(jnp.exp(cc[L - 1:L, :]))

    def stack_heads(x):
        return jnp.concatenate([jnp.where(head0, x, 0.0), jnp.where(not_head0, x, 0.0)], axis=0)

    ats_l = [stack_heads(x) for x in at_l]
    rts_l = [stack_heads(x) for x in rt_l]
    vs_l = [jnp.concatenate([x, x], axis=0) for x in v_l]
    sc_l = [_mm(jnp.concatenate([a_, r_], axis=0), jnp.concatenate([b_, b_, k_, k_], axis=0), "nt", P_SCORE)
            for a_, r_, b_, k_ in zip(ats_l, rts_l, bt_l, kt_l)]
    nab_l = [jnp.where(strict_bd, s_[0:S2, 0:S2], 0.0) for s_ in sc_l]
    nak_l = [jnp.where(strict_bd, s_[0:S2, S2:2 * S2], 0.0) for s_ in sc_l]
    mrb_l = [jnp.where(incl_bd, s_[S2:2 * S2, 0:S2], 0.0) for s_ in sc_l]
    mrk_l = [jnp.where(incl_bd, s_[S2:2 * S2, S2:2 * S2], 0.0) for s_ in sc_l]
    t_l = [eye + jnp.where(tri_masks[0], n_, 0.0) for n_ in nab_l]
    for m in tri_masks[1:]:
        ot_l = [_mm(jnp.where(m, n_, 0.0), t_, "nn", P_TRI) for n_, t_ in zip(nab_l, t_l)]
        t_l = [t_ + _mm(t_, ot_, "nn", P_TRI) for t_, ot_ in zip(t_l, ot_l)]
    nv_l = [_mm(n_, v_, "nn", P_MAIN) for n_, v_ in zip(nak_l, vs_l)]
    au_l = [_mm(t_, jnp.concatenate([a_, nv_], axis=1), "nn", P_MAIN) for t_, a_, nv_ in zip(t_l, ats_l, nv_l)]
    ry_l = [_mm(jnp.concatenate([mb_, mk_], axis=1),
                jnp.concatenate([au_, jnp.concatenate([jnp.zeros_like(v_), v_], axis=1)], axis=0), "nn", P_MAIN)
            for mb_, mk_, au_, v_ in zip(mrb_l, mrk_l, au_l, vs_l)]
    rbar_l = [rt_ + ry_[0:L, 0:LANES] + ry_[L:S2, 0:LANES] for rt_, ry_ in zip(rt_l, ry_l)]
    ybase_l = [jnp.where(head0, ry_[0:L, LANES:2 * LANES], ry_[L:S2, LANES:2 * LANES]) for ry_ in ry_l]
    gq_l = [_mm(jnp.concatenate(
                [jnp.concatenate([au_[0:L, 0:LANES] + au_[L:S2, 0:LANES],
                                  jnp.where(head0, au_[0:L, LANES:2 * LANES], au_[L:S2, LANES:2 * LANES])], axis=1),
                 jnp.concatenate([zeros_l, v_], axis=1)], axis=0),
                jnp.concatenate([b_, k_], axis=0), "tn", P_MAIN)
            for au_, v_, b_, k_ in zip(au_l, v_l, bt_l, kt_l)]
    gmat_l = [jnp.where(same_head, gq_[0:LANES], 0.0) for gq_ in gq_l]
    q0_l = [jnp.where(same_head, gq_[LANES:2 * LANES], 0.0) for gq_ in gq_l]
    st = [s_ref[pp] for pp in range(npb)]
    ys = [[None] * nch for _ in range(npb)]
    for j in range(nch):
        for pp in range(npb):
            i = pp * nch + j
            sr = st[pp] * rho_l[i]
            ys[pp][j] = _mm(rbar_l[i], sr, "nt", P_STATE) + ybase_l[i]
            st[pp] = (sr + _mm(sr, gmat_l[i], "nn", P_STATE) + q0_l[i]) * gam_l[i]
    for pp in range(npb):
        s_ref[pp] = st[pp]
    y = jnp.concatenate([jnp.concatenate(ys[pp], axis=0) for pp in range(npb)], axis=1)

    mu = _mm_exact_rhs(y, mean_bd)
    d = y - mu
    var = _mm_exact_rhs(d * d, mean_bd)
    yn = d * lax.rsqrt(var + RWKV_LN_EPS) * lng_ref[...] + lnb_ref[...]
    bonus = _mm_exact_rhs(r * k2 * rk_ref[...], ones_bd) * v
    o_ref[...] = ((yn + bonus) * g).astype(o_ref.dtype)


def rwkv_time_mix(z, mu_p, w0, w2, a0, a2, g2p, k_k, k_a, r_k, lnx_g, lnx_b, B, T, RW):
    L = RWKV_CHUNK
    R = L * RWKV_BLOCK_CHUNKS
    W = LANES
    WB = W * RWKV_BLOCK_PAIRS
    ngrp = RW // WB
    c_gl = 3 * RW // GATE_LORA_PAD
    c_wl = (3 * RW + GATE_LORA_PAD) // W
    c_al = c_wl + 1

    def zcol(base):
        return pl.BlockSpec((None, R, WB), lambda b, p, c: (b, c, base + p))

    def zfix(idx, width):
        return pl.BlockSpec((None, R, width), lambda b, p, c: (b, c, idx))

    def mucol(base):
        return pl.BlockSpec((1, WB), lambda b, p, c: (0, base + p))

    def mufix(idx, width):
        return pl.BlockSpec((1, width), lambda b, p, c: (0, idx))

    pvec = pl.BlockSpec((1, WB), lambda b, p, c: (0, p))

    def lora(rank):
        return pl.BlockSpec((rank, WB), lambda b, p, c: (0, p))

    row = lambda a: a.reshape(1, -1)
    return pl.pallas_call(
        functools.partial(_rwkv_kernel, L=L),
        grid=(B, ngrp, T // R),
        in_specs=[zcol(0), zcol(ngrp), zcol(2 * ngrp),
                  zfix(c_gl, GATE_LORA_PAD), zfix(c_wl, W), zfix(c_al, W),
                  mucol(0), mucol(ngrp), mucol(2 * ngrp),
                  mufix(c_gl, GATE_LORA_PAD), mufix(c_wl, W), mufix(c_al, W),
                  pvec, pvec, pvec, pvec, pvec, pvec, pvec,
                  lora(DECAY_LORA), lora(ICLR_LORA), lora(GATE_LORA_PAD)],
        out_specs=pl.BlockSpec((None, R, WB), lambda b, p, c: (b, c, p)),
        out_shape=jax.ShapeDtypeStruct((B, T, RW), BF16),
        scratch_shapes=[pltpu.VMEM((RWKV_BLOCK_PAIRS, W, W), F32)] + [pltpu.VMEM((8, WB), F32)] * 3
        + [pltpu.VMEM((8, GATE_LORA_PAD), F32), pltpu.VMEM((8, W), F32), pltpu.VMEM((8, W), F32)],
        compiler_params=_cparams(("parallel", "parallel", "arbitrary")),
        name="rwkv7",
    )(z, z, z, z, z, z, mu_p, mu_p, mu_p, mu_p, mu_p, mu_p,
      row(w0), row(a0), row(k_k), row(k_a), row(r_k), row(lnx_g), row(lnx_b), w2, a2, g2p)


def _pad_cols(w, n):
    return jnp.pad(w, ((0, 0), (0, n - w.shape[1])))


def kernel(x, c, ada_w, ada_b, ada_table, ln_g, ln_b, mix_w_in, mix_mu, rwkv_w0, rwkv_w2, rwkv_a0, rwkv_a2,
           rwkv_g2, rwkv_kk, rwkv_ka, rwkv_rk, rwkv_lnx_g, rwkv_lnx_b, mix_w_out, ffn_w_gate, ffn_w_up,
           ffn_w_down, pool_w, pool_scale, moe_router_w, moe_router_b, moe_w_gate, moe_w_up, moe_w_down):
    B, T, D = x.shape
    M = B * T
    depth = ada_table.shape[0]
    RW = rwkv_w0.shape[1]
    MW = D - RW
    n_moba = MW // MOBA_HEAD_DIM
    gate_lora = rwkv_g2.shape[1]
    alpha = (2 * depth) ** 0.25

    mods = ada_mod(c, ada_w, ada_b, ada_table)
    xf = x.reshape(M, D)
    h = modulate(xf, mods[0, :, 1], mods[0, :, 0], T)
    lnrow = lambda a: a.reshape(1, D)

    za = 3 * RW + GATE_LORA_PAD + DECAY_LORA + ICLR_LORA
    n_in = za + 3 * MW
    n_pad = -(-n_in // 512) * 512
    o_wl = 3 * RW
    o_al = o_wl + DECAY_LORA
    o_gl = o_al + ICLR_LORA
    o_q = o_gl + gate_lora

    def permute_cols(w):
        return jnp.concatenate(
            [w[..., :3 * RW], _pad_cols(w[..., o_gl:o_q], GATE_LORA_PAD), w[..., o_wl:o_al], w[..., o_al:o_gl],
             _pad_cols(w[..., o_q:], n_pad - za)], axis=-1)

    for l in range(depth):
        i = l // 2
        m = mods[l]
        g1, sh2, sc2, g2 = m[:, 2], m[:, 3], m[:, 4], m[:, 5]
        if l + 1 < depth:
            sh_n, sc_n = mods[l + 1, :, 0], mods[l + 1, :, 1]
        else:
            sh_n, sc_n = jnp.zeros_like(sh2), jnp.zeros_like(sc2)
        if l % 2 == 0:
            w_in = permute_cols(mix_w_in[i]).astype(BF16)
            mu_p = permute_cols(jnp.pad(mix_mu[i], (0, mix_w_in.shape[2] - mix_mu.shape[1]))[None, :])
            z = matmul(h, w_in, 512, 512).reshape(B, T, n_pad)
            g2p = jnp.pad(rwkv_g2[i], ((0, GATE_LORA_PAD - gate_lora), (0, 0)))
            ya = rwkv_time_mix(z, mu_p, rwkv_w0[i], rwkv_w2[i], rwkv_a0[i], rwkv_a2[i], g2p, rwkv_kk[i],
                               rwkv_ka[i], rwkv_rk[i], rwkv_lnx_g[i], rwkv_lnx_b[i], B, T, RW)
            yb = moba_attention(z, za // LANES, n_moba, B, T)
            ycat = jnp.concatenate([ya, yb], axis=-1).reshape(M, D)
            xf, h = matmul_resid_ln(ycat, mix_w_out[i].astype(BF16), xf, g1, lnrow(ln_g[l, 0]), lnrow(ln_b[l, 0]),
                                    sc2, sh2, T, alpha, 512, 512)
            F = ffn_w_gate.shape[2]
            fp = -(-F // 512) * 512
            act = swiglu_up(h, _pad_cols(ffn_w_gate[i], fp).astype(BF16), _pad_cols(ffn_w_up[i], fp).astype(BF16),
                            1024, 512)
            wd = jnp.pad(ffn_w_down[i], ((0, fp - F), (0, 0))).astype(BF16)
            xf, h = matmul_resid_ln(act, wd, xf, g2, lnrow(ln_g[l, 1]), lnrow(ln_b[l, 1]), sc_n, sh_n, T, alpha,
                                    512, 512)
        else:
            xf, h = pool_mixer_ln(xf, m[:, 1], m[:, 0], pool_w[i].astype(BF16), lnrow(pool_scale[i]), g1,
                                  lnrow(ln_g[l, 0]), lnrow(ln_b[l, 0]), sc2, sh2, T, alpha)
            comb = moe_router(xf, sc2, sh2, moe_router_w[i], moe_router_b[i], T)
            E, _, NE = moe_w_gate.shape[1:]
            act = moe_up_dense(h, comb, moe_w_gate[i].astype(BF16), moe_w_up[i].astype(BF16), 1024, NE // 7 if NE % 7 == 0 and (NE // 7) % LANES == 0 else NE)
            wd = moe_w_down[i].reshape(E * NE, D).astype(BF16)
            tk = 512
            xf, h = matmul_resid_ln(act, wd, xf, g2, lnrow(ln_g[l, 1]), lnrow(ln_b[l, 1]), sc_n, sh_n, T, alpha,
                                    512, tk)
    return xf.reshape(B, T, D)
```

```python
import functools
import math

import jax
import jax.numpy as jnp
from jax import lax
from jax.experimental import pallas as pl
from jax.experimental.pallas import tpu as pltpu

F32 = jnp.float32
BF16 = jnp.bfloat16
HI = lax.Precision.HIGHEST

RWKV_HEAD_DIM = 64
DECAY_LORA = 128
ICLR_LORA = 128
GATE_LORA_PAD = 512
RWKV_LN_EPS = 64e-5
MOBA_HEAD_DIM = 128
MOBA_BLOCK = 256
MOBA_TOPK = 3
ROPE_THETA = 10000.0
POOL_WINDOWS = (2, 4, 8, 16)
POOL_HALO = 16
MOE_TOPK = 2
N_MOD = 6
LN_EPS = 1e-5
NEG_INF = -1e30

LANES = 128
LN_ROWS = 64
MOE_TILE = 256
RWKV_CHUNK = 64
RWKV_BLOCK_CHUNKS = 4
RWKV_BLOCK_PAIRS = 2
P_LORA_W, P_LORA, P_SCORE, P_TRI, P_MAIN, P_STATE = 1, 1, 1, 3, 1, 1
VMEM_LIMIT = 56 * 1024 * 1024
VMEM_LIMIT_DOWN = 60 * 1024 * 1024


def _cparams(sem, vmem_limit=VMEM_LIMIT):
    return pltpu.CompilerParams(dimension_semantics=sem, vmem_limit_bytes=vmem_limit)


def _dot(a, b, precision=None):
    return jnp.dot(a, b, preferred_element_type=F32, precision=precision)


def _dot_nt(a, b, precision=None):
    return lax.dot_general(a, b, (((1,), (1,)), ((), ())), preferred_element_type=F32, precision=precision)


def _dot_tn(a, b, precision=None):
    return lax.dot_general(a, b, (((0,), (0,)), ((), ())), preferred_element_type=F32, precision=precision)


def _ada_kernel(c_ref, w_ref, b_ref, tab_ref, o_ref):
    c = c_ref[...]
    sc = (c * jax.nn.sigmoid(c)).astype(BF16)
    acc = _dot(sc, w_ref[...].astype(BF16)) + b_ref[...]
    o_ref[...] = acc[None, :, :] + tab_ref[...][:, None, :]


def ada_mod(c, ada_w, ada_b, ada_table):
    B, D = c.shape
    N = ada_w.shape[1]
    depth = ada_table.shape[0]
    BP = 8
    cp = jnp.zeros((BP, D), F32).at[:B].set(c)
    tn = 512
    out = pl.pallas_call(
        _ada_kernel,
        grid=(N // tn,),
        in_specs=[
            pl.BlockSpec((BP, D), lambda j: (0, 0)),
            pl.BlockSpec((D, tn), lambda j: (0, j)),
            pl.BlockSpec((1, tn), lambda j: (0, j)),
            pl.BlockSpec((depth, tn), lambda j: (0, j)),
        ],
        out_specs=pl.BlockSpec((depth, BP, tn), lambda j: (0, 0, j)),
        out_shape=jax.ShapeDtypeStruct((depth, BP, N), F32),
        compiler_params=_cparams(("parallel",)),
        name="ada_mod",
    )(cp, ada_w, ada_b.reshape(1, N), ada_table.reshape(depth, N))
    return out[:, :B].reshape(depth, B, N_MOD, 1, D)


def _modulate_kernel(x_ref, sc_ref, sh_ref, o_ref):
    o_ref[...] = (x_ref[...] * (1.0 + sc_ref[...]) + sh_ref[...]).astype(o_ref.dtype)


def modulate(x, sc, sh, T):
    M, D = x.shape
    tm = 512
    tpb = T // tm
    vec = pl.BlockSpec((None, 1, D), lambda i: (i // tpb, 0, 0))
    return pl.pallas_call(
        _modulate_kernel,
        grid=(M // tm,),
        in_specs=[pl.BlockSpec((tm, D), lambda i: (i, 0)), vec, vec],
        out_specs=pl.BlockSpec((tm, D), lambda i: (i, 0)),
        out_shape=jax.ShapeDtypeStruct((M, D), BF16),
        compiler_params=_cparams(("parallel",)),
        name="modulate",
    )(x, sc, sh)


def _mm_kernel(a_ref, w_ref, o_ref):
    o_ref[...] = _dot(a_ref[...], w_ref[...]).astype(o_ref.dtype)


def matmul(a, w, tm, tn, out_dtype=F32):
    M, K = a.shape
    N = w.shape[1]
    return pl.pallas_call(
        _mm_kernel,
        grid=(N // tn, M // tm),
        in_specs=[pl.BlockSpec((tm, K), lambda j, i: (i, 0)),
                  pl.BlockSpec((K, tn), lambda j, i: (0, j))],
        out_specs=pl.BlockSpec((tm, tn), lambda j, i: (i, j)),
        out_shape=jax.ShapeDtypeStruct((M, N), out_dtype),
        compiler_params=_cparams(("parallel", "parallel")),
        name="matmul",
    )(a, w)


def _resid_ln_mod(x, y, gate, lng, lnb, sc, sh, alpha):
    r = alpha * x + gate * y
    mu = jnp.mean(r, axis=-1, keepdims=True)
    d = r - mu
    var = jnp.mean(d * d, axis=-1, keepdims=True)
    xn = d * lax.rsqrt(var + LN_EPS) * lng + lnb
    return xn, xn * (1.0 + sc) + sh


def _mm_ln_kernel(a_ref, w_ref, x_ref, gate_ref, lng_ref, lnb_ref, sc_ref, sh_ref,
                  xo_ref, ho_ref, *, alpha):
    k = pl.program_id(1)

    @pl.when(k == 0)
    def _():
        xo_ref[...] = _dot(a_ref[...], w_ref[...])

    @pl.when(k > 0)
    def _():
        xo_ref[...] += _dot(a_ref[...], w_ref[...])

    @pl.when(k == pl.num_programs(1) - 1)
    def _():
        for r0 in range(0, xo_ref.shape[0], LN_ROWS):
            rows = pl.ds(r0, LN_ROWS)
            xn, h = _resid_ln_mod(x_ref[rows, :], xo_ref[rows, :], gate_ref[...], lng_ref[...], lnb_ref[...],
                                  sc_ref[...], sh_ref[...], alpha)
            xo_ref[rows, :] = xn
            ho_ref[rows, :] = h.astype(ho_ref.dtype)


def matmul_resid_ln(a, w, x, gate, lng, lnb, sc, sh, T, alpha, tm, tk):
    M, K = a.shape
    D = w.shape[1]
    tpb = T // tm
    vec_b = pl.BlockSpec((None, 1, D), lambda i, k: (i // tpb, 0, 0))
    vec = pl.BlockSpec((1, D), lambda i, k: (0, 0))
    row = pl.BlockSpec((tm, D), lambda i, k: (i, 0))
    return pl.pallas_call(
        functools.partial(_mm_ln_kernel, alpha=alpha),
        grid=(M // tm, K // tk),
        in_specs=[pl.BlockSpec((tm, tk), lambda i, k: (i, k)),
                  pl.BlockSpec((tk, D), lambda i, k: (k, 0)),
                  pl.BlockSpec((tm, D), lambda i, k: (i, 0), pipeline_mode=pl.Buffered(1)),
                  vec_b, vec, vec, vec_b, vec_b],
        out_specs=[row, row],
        out_shape=[jax.ShapeDtypeStruct((M, D), F32), jax.ShapeDtypeStruct((M, D), BF16)],
        compiler_params=_cparams(("parallel", "arbitrary"), VMEM_LIMIT_DOWN),
        name="matmul_resid_ln",
    )(a, w, x, gate, lng, lnb, sc, sh)


def _swiglu_up_kernel(a_ref, wg_ref, wu_ref, o_ref):
    a = a_ref[...]
    g = _dot(a, wg_ref[...])
    u = _dot(a, wu_ref[...])
    o_ref[...] = (g * jax.nn.sigmoid(g) * u).astype(o_ref.dtype)


def swiglu_up(a, wg, wu, tm, tn):
    M, K = a.shape
    N = wg.shape[1]
    wspec = pl.BlockSpec((K, tn), lambda i, j: (0, j))
    return pl.pallas_call(
        _swiglu_up_kernel,
        grid=(M // tm, N // tn),
        in_specs=[pl.BlockSpec((tm, K), lambda i, j: (i, 0)), wspec, wspec],
        out_specs=pl.BlockSpec((tm, tn), lambda i, j: (i, j)),
        out_shape=jax.ShapeDtypeStruct((M, N), BF16),
        compiler_params=_cparams(("parallel", "parallel")),
        name="swiglu_up",
    )(a, wg, wu)


def _router_kernel(h_ref, rw_ref, rb_ref, idx_ref, gate_ref, *, n_experts):
    logits = _dot(h_ref[...], rw_ref[...], HI) + rb_ref[...]
    lane = lax.broadcasted_iota(jnp.int32, logits.shape, 1)
    logits = jnp.where(lane < n_experts, logits, -jnp.inf)
    m1 = jnp.max(logits, axis=-1, keepdims=True)
    i1 = jnp.min(jnp.where(logits == m1, lane, LANES), axis=-1, keepdims=True)
    oh1 = lane == i1
    rest = jnp.where(oh1, -jnp.inf, logits)
    m2 = jnp.max(rest, axis=-1, keepdims=True)
    i2 = jnp.min(jnp.where(rest == m2, lane, LANES), axis=-1, keepdims=True)
    e = jnp.exp(m2 - m1)
    g1 = 1.0 / (1.0 + e)
    g2 = e / (1.0 + e)
    idx_ref[...] = jnp.where(lane == 0, i1, jnp.where(lane == 1, i2, 0))
    gate_ref[...] = jnp.where(lane == 0, g1, jnp.where(lane == 1, g2, 0.0))


def moe_router(h, router_w, router_b):
    M, D = h.shape
    E = router_w.shape[1]
    tm = 512
    rw = jnp.zeros((D, LANES), F32).at[:, :E].set(router_w)
    rb = jnp.zeros((1, LANES), F32).at[0, :E].set(router_b)
    out = pl.BlockSpec((tm, LANES), lambda i: (i, 0))
    return pl.pallas_call(
        functools.partial(_router_kernel, n_experts=E),
        grid=(M // tm,),
        in_specs=[pl.BlockSpec((tm, D), lambda i: (i, 0)),
                  pl.BlockSpec((D, LANES), lambda i: (0, 0)),
                  pl.BlockSpec((1, LANES), lambda i: (0, 0))],
        out_specs=[out, out],
        out_shape=[jax.ShapeDtypeStruct((M, LANES), jnp.int32), jax.ShapeDtypeStruct((M, LANES), F32)],
        compiler_params=_cparams(("parallel",)),
        name="moe_router",
    )(h, rw, rb)


def _gather_rows_kernel(tok_ref, h_hbm, o_ref, buf_ref, sem):
    i = pl.program_id(0)
    tm = o_ref.shape[0]

    def row_copy(r):
        return pltpu.make_async_copy(h_hbm.at[pl.ds(tok_ref[i * tm + r], 1), :], buf_ref.at[pl.ds(r, 1), :], sem.at[0])

    def start(r, c):
        row_copy(r).start()
        return c

    def wait(r, c):
        row_copy(r).wait()
        return c

    lax.fori_loop(0, tm, start, 0)
    lax.fori_loop(0, tm, wait, 0)
    o_ref[...] = buf_ref[...].astype(o_ref.dtype)


def moe_gather_rows(h, row_token, tm):
    D = h.shape[1]
    NR = row_token.shape[0]
    return pl.pallas_call(
        _gather_rows_kernel,
        grid_spec=pltpu.PrefetchScalarGridSpec(
            num_scalar_prefetch=1,
            grid=(NR // tm,),
            in_specs=[pl.BlockSpec(memory_space=pl.ANY)],
            out_specs=pl.BlockSpec((tm, D), lambda i, tok: (i, 0)),
            scratch_shapes=[pltpu.VMEM((tm, D), F32), pltpu.SemaphoreType.DMA((1,))]),
        out_shape=jax.ShapeDtypeStruct((NR, D), BF16),
        compiler_params=_cparams(("arbitrary",)),
        name="moe_gather",
    )(row_token, h)


def _moe_up_kernel(te_ref, nu_ref, a_ref, gate_ref, wg_ref, wu_ref, o_ref):
    i = pl.program_id(1)

    @pl.when(i < nu_ref[0])
    def _():
        a = a_ref[...]
        g = _dot(a, wg_ref[...])
        u = _dot(a, wu_ref[...])
        o_ref[...] = (g * jax.nn.sigmoid(g) * u * gate_ref[...]).astype(o_ref.dtype)

    @pl.when(i >= nu_ref[0])
    def _():
        o_ref[...] = jnp.zeros_like(o_ref)


def moe_up_grouped(xs, row_gate, tile_e, n_used, wg, wu, tm, tn):
    NR, K = xs.shape
    E, _, NE = wg.shape
    wspec = pl.BlockSpec((None, K, tn), lambda j, i, te, nu: (te[i], 0, j))
    return pl.pallas_call(
        _moe_up_kernel,
        grid_spec=pltpu.PrefetchScalarGridSpec(
            num_scalar_prefetch=2,
            grid=(NE // tn, NR // tm),
            in_specs=[pl.BlockSpec((tm, K), lambda j, i, te, nu: (i, 0)),
                      pl.BlockSpec((tm, 1), lambda j, i, te, nu: (i, 0)), wspec, wspec],
            out_specs=pl.BlockSpec((tm, tn), lambda j, i, te, nu: (i, j))),
        out_shape=jax.ShapeDtypeStruct((NR, NE), BF16),
        compiler_params=_cparams(("parallel", "parallel")),
        name="moe_up",
    )(tile_e, n_used, xs, row_gate, wg, wu)


def _moe_down_kernel(te_ref, nu_ref, a_ref, w_ref, o_ref):
    i = pl.program_id(0)

    @pl.when(i < nu_ref[0])
    def _():
        o_ref[...] = _dot(a_ref[...], w_ref[...])

    @pl.when(i >= nu_ref[0])
    def _():
        o_ref[...] = jnp.zeros_like(o_ref)


def moe_down_grouped(act, tile_e, n_used, wd, tm):
    NR, NE = act.shape
    D = wd.shape[2]
    return pl.pallas_call(
        _moe_down_kernel,
        grid_spec=pltpu.PrefetchScalarGridSpec(
            num_scalar_prefetch=2,
            grid=(NR // tm,),
            in_specs=[pl.BlockSpec((tm, NE), lambda i, te, nu: (i, 0)),
                      pl.BlockSpec((None, NE, D), lambda i, te, nu: (te[i], 0, 0))],
            out_specs=pl.BlockSpec((tm, D), lambda i, te, nu: (i, 0))),
        out_shape=jax.ShapeDtypeStruct((NR, D), F32),
        compiler_params=_cparams(("parallel",)),
        name="moe_down",
    )(tile_e, n_used, act, wd)


def _moe_combine_ln_kernel(p0_ref, p1_ref, ys_hbm, x_ref, gate_ref, lng_ref, lnb_ref, sc_ref, sh_ref,
                           xo_ref, ho_ref, b0_ref, b1_ref, sem, *, alpha):
    i = pl.program_id(0)
    tm = x_ref.shape[0]

    def row_copies(r):
        t = i * tm + r
        return (pltpu.make_async_copy(ys_hbm.at[pl.ds(p0_ref[t], 1), :], b0_ref.at[pl.ds(r, 1), :], sem.at[0]),
                pltpu.make_async_copy(ys_hbm.at[pl.ds(p1_ref[t], 1), :], b1_ref.at[pl.ds(r, 1), :], sem.at[1]))

    def start(r, c):
        for cp in row_copies(r):
            cp.start()
        return c

    def wait(r, c):
        for cp in row_copies(r):
            cp.wait()
        return c

    lax.fori_loop(0, tm, start, 0)
    lax.fori_loop(0, tm, wait, 0)
    for r0 in range(0, tm, LN_ROWS):
        rows = pl.ds(r0, LN_ROWS)
        y = b0_ref[rows, :] + b1_ref[rows, :]
        xn, h = _resid_ln_mod(x_ref[rows, :], y, gate_ref[...], lng_ref[...], lnb_ref[...],
                              sc_ref[...], sh_ref[...], alpha)
        xo_ref[rows, :] = xn
        ho_ref[rows, :] = h.astype(ho_ref.dtype)


def moe_combine_ln(ys, pos0, pos1, x, gate, lng, lnb, sc, sh, T, alpha):
    M, D = x.shape
    tm = 256
    tpb = T // tm
    vec_b = pl.BlockSpec((None, 1, D), lambda i, p0, p1: (i // tpb, 0, 0))
    vec = pl.BlockSpec((1, D), lambda i, p0, p1: (0, 0))
    row = pl.BlockSpec((tm, D), lambda i, p0, p1: (i, 0))
    return pl.pallas_call(
        functools.partial(_moe_combine_ln_kernel, alpha=alpha),
        grid_spec=pltpu.PrefetchScalarGridSpec(
            num_scalar_prefetch=2,
            grid=(M // tm,),
            in_specs=[pl.BlockSpec(memory_space=pl.ANY), row, vec_b, vec, vec, vec_b, vec_b],
            out_specs=[row, row],
            scratch_shapes=[pltpu.VMEM((tm, D), F32), pltpu.VMEM((tm, D), F32), pltpu.SemaphoreType.DMA((2,))]),
        out_shape=[jax.ShapeDtypeStruct((M, D), F32), jax.ShapeDtypeStruct((M, D), BF16)],
        compiler_params=_cparams(("arbitrary",)),
        name="moe_combine_ln",
    )(pos0, pos1, ys, x, gate, lng, lnb, sc, sh)


def moe_group_metadata(idx, gates, n_experts, tm):
    M = idx.shape[0]
    e_flat = idx[:, :MOE_TOPK].reshape(-1)
    g_flat = gates[:, :MOE_TOPK].reshape(-1)
    onehot = (e_flat[:, None] == jnp.arange(n_experts)[None, :]).astype(jnp.int32)
    rank = jnp.take_along_axis(jnp.cumsum(onehot, axis=0) - onehot, e_flat[:, None], axis=1)[:, 0]
    padded = -(-onehot.sum(axis=0) // tm) * tm
    ends = jnp.cumsum(padded)
    pos = (ends - padded)[e_flat] + rank
    n_rows = MOE_TOPK * M + n_experts * tm
    row_token = jnp.zeros((n_rows,), jnp.int32).at[pos].set(jnp.arange(MOE_TOPK * M, dtype=jnp.int32) // MOE_TOPK)
    row_gate = jnp.zeros((n_rows, 1), F32).at[pos, 0].set(g_flat)
    tile_start = jnp.arange(n_rows // tm, dtype=jnp.int32) * tm
    tile_e = jnp.minimum(jnp.searchsorted(ends, tile_start, side="right"), n_experts - 1).astype(jnp.int32)
    n_used = (ends[-1:] // tm).astype(jnp.int32)
    pos = pos.astype(jnp.int32).reshape(M, MOE_TOPK)
    return row_token, row_gate, tile_e, n_used, pos[:, 0], pos[:, 1]


def _pool_kernel(x_ref, xh_ref, sc_ref, sh_ref, w_ref, ps_ref, gate_ref, lng_ref, lnb_ref,
                 sc2_ref, sh2_ref, xo_ref, ho_ref, hh_ref, y_ref, *, tm, tpb, alpha):
    i = pl.program_id(0)
    D = x_ref.shape[1]
    dg = D // len(POOL_WINDOWS)
    sc = sc_ref[...]
    sh = sh_ref[...]
    x = x_ref[...]
    first = (i % tpb) == 0
    halo = xh_ref[...] * (1.0 + sc) + sh
    hh_ref[0:POOL_HALO, :] = jnp.where(first, 0.0, halo)
    hh_ref[POOL_HALO:, :] = x * (1.0 + sc) + sh
    t = (i % tpb) * tm + lax.broadcasted_iota(jnp.int32, (tm, 1), 0)
    tcnt = (t + 1).astype(F32)
    for g, win in enumerate(POOL_WINDOWS):
        cols = slice(g * dg, (g + 1) * dg)
        h = hh_ref[POOL_HALO:, cols]
        s = h
        for d in range(1, win):
            s = s + hh_ref[pl.ds(POOL_HALO - d, tm), cols]
        inv = 1.0 / jnp.minimum(tcnt, float(win))
        p = (s * inv - h).astype(BF16)
        y_ref[:, cols] = _dot(p, w_ref[g])
    y = y_ref[...] * ps_ref[...]
    xn, hn = _resid_ln_mod(x, y, gate_ref[...], lng_ref[...], lnb_ref[...], sc2_ref[...], sh2_ref[...], alpha)
    xo_ref[...] = xn
    ho_ref[...] = hn.astype(ho_ref.dtype)


def pool_mixer_ln(x, sc, sh, pool_w, pool_scale, gate, lng, lnb, sc2, sh2, T, alpha):
    M, D = x.shape
    G, dg, _ = pool_w.shape
    tm = 256
    tpb = T // tm
    hb = tm // POOL_HALO
    vec_b = pl.BlockSpec((None, 1, D), lambda i: (i // tpb, 0, 0))
    vec = pl.BlockSpec((1, D), lambda i: (0, 0))
    row = pl.BlockSpec((tm, D), lambda i: (i, 0))
    return pl.pallas_call(
        functools.partial(_pool_kernel, tm=tm, tpb=tpb, alpha=alpha),
        grid=(M // tm,),
        in_specs=[row,
                  pl.BlockSpec((POOL_HALO, D), lambda i: (jnp.maximum(i * hb - 1, 0), 0)),
                  vec_b, vec_b,
                  pl.BlockSpec((G, dg, dg), lambda i: (0, 0, 0)),
                  vec, vec_b, vec, vec, vec_b, vec_b],
        out_specs=[row, row],
        out_shape=[jax.ShapeDtypeStruct((M, D), F32), jax.ShapeDtypeStruct((M, D), F32)],
        scratch_shapes=[pltpu.VMEM((tm + POOL_HALO, D), F32), pltpu.VMEM((tm, D), F32)],
        compiler_params=_cparams(("parallel",)),
        name="pool_mixer",
    )(x, x, sc, sh, pool_w, pool_scale, gate, lng, lnb, sc2, sh2)


def _moba_kernel(q_ref, k_ref, v_ref, cos_ref, sin_ref, o_ref, *, nb):
    bs = MOBA_BLOCK
    cos = cos_ref[...]
    sin = sin_ref[...]

    def rope(t):
        return t * cos + pltpu.roll(t, MOBA_HEAD_DIM // 2, axis=1) * sin

    qr = rope(q_ref[...])
    kr = rope(k_ref[...])
    T = qr.shape[0]
    prow = lax.broadcasted_iota(jnp.int32, (LANES, T), 0)
    pcol = lax.broadcasted_iota(jnp.int32, (LANES, T), 1)
    pool = jnp.where(pcol // bs == prow, 1.0 / bs, 0.0).astype(F32)
    kmean = _dot(pool, kr, HI)
    qb = qr.astype(BF16)
    kb = kr.astype(BF16)
    vb = v_ref[...].astype(BF16)
    scale = MOBA_HEAD_DIM ** -0.5
    lane = lax.broadcasted_iota(jnp.int32, (bs, LANES), 1)
    rowi = lax.broadcasted_iota(jnp.int32, (bs, bs), 0)
    coli = lax.broadcasted_iota(jnp.int32, (bs, bs), 1)
    for i in range(nb):
        rows = slice(i * bs, (i + 1) * bs)
        qi = qb[rows]
        s_list = [jnp.where(coli <= rowi, _dot_nt(qi, kb[rows]) * scale, NEG_INF)]
        if i > 0:
            gate = _dot_nt(qr[rows], kmean, HI)
            valid = lane < i
            gate = jnp.where(valid, gate, -jnp.inf)
            rank = jnp.zeros((bs, LANES), jnp.int32)
            for j in range(i):
                gj = gate[:, j:j + 1]
                ahead = (gj > gate) | ((gj == gate) & (j < lane))
                rank = rank + ahead.astype(jnp.int32)
            sel = valid & (rank < MOBA_TOPK)
            for j in range(i):
                sj = _dot_nt(qi, kb[j * bs:(j + 1) * bs]) * scale
                s_list.append(jnp.where(sel[:, j:j + 1], sj, NEG_INF))
        m = s_list[0].max(axis=-1, keepdims=True)
        for s in s_list[1:]:
            m = jnp.maximum(m, s.max(axis=-1, keepdims=True))
        l = jnp.zeros((bs, 1), F32)
        acc = jnp.zeros((bs, MOBA_HEAD_DIM), F32)
        for jj, s in enumerate(s_list):
            j = i if jj == 0 else jj - 1
            p = jnp.exp(s - m)
            l = l + p.sum(axis=-1, keepdims=True)
            acc = acc + _dot(p.astype(BF16), vb[j * bs:(j + 1) * bs])
        o_ref[rows, :] = (acc / l).astype(o_ref.dtype)


def moba_attention(z, q_col, n_heads, B, T):
    assert T % MOBA_BLOCK == 0
    nb = T // MOBA_BLOCK
    dh = MOBA_HEAD_DIM
    inv = jnp.power(ROPE_THETA, -jnp.arange(0, dh, 2, dtype=F32) / dh)
    ang = jnp.arange(T, dtype=F32)[:, None] * inv[None, :]
    cos = jnp.concatenate([jnp.cos(ang), jnp.cos(ang)], axis=-1)
    sin = jnp.concatenate([-jnp.sin(ang), jnp.sin(ang)], axis=-1)

    def col(off):
        return pl.BlockSpec((None, T, dh), lambda b, h: (b, 0, q_col + off * n_heads + h))

    tab = pl.BlockSpec((T, dh), lambda b, h: (0, 0))
    return pl.pallas_call(
        functools.partial(_moba_kernel, nb=nb),
        grid=(B, n_heads),
        in_specs=[col(0), col(1), col(2), tab, tab],
        out_specs=pl.BlockSpec((None, T, dh), lambda b, h: (b, 0, h)),
        out_shape=jax.ShapeDtypeStruct((B, T, n_heads * dh), BF16),
        compiler_params=_cparams(("parallel", "parallel")),
        name="moba",
    )(z, z, z, cos, sin)


def _split_bf16(x, n):
    parts, rem = [], x
    for i in range(n):
        p = rem.astype(BF16)
        parts.append(p)
        if i + 1 < n:
            rem = rem - p.astype(F32)
    return parts


_DIMS = {"nn": ((1,), (0,)), "nt": ((1,), (1,)), "tn": ((0,), (0,))}


def _mm(a, b, form="nn", passes=1):
    if passes == 1:
        a3, b3 = a.astype(BF16), b.astype(BF16)
    else:
        ah, al = _split_bf16(a, 2)
        bh, bl = _split_bf16(b, 2)
        a3 = jnp.concatenate([ah, ah, al], axis=0 if form == "tn" else 1)
        b3 = jnp.concatenate([bh, bl, bh], axis=1 if form == "nt" else 0)
    return lax.dot_general(a3, b3, (_DIMS[form], ((), ())), preferred_element_type=F32)


def _mm_exact_rhs(a, b_bf16):
    a3 = jnp.concatenate(_split_bf16(a, 3), axis=1)
    b3 = jnp.concatenate([b_bf16, b_bf16, b_bf16], axis=0)
    return _dot(a3, b3)


def _rwkv_kernel(zr_ref, zk_ref, zv_ref, zgl_ref, zwl_ref, zal_ref,
                 mur_ref, muk_ref, muv_ref, mugl_ref, muwl_ref, mual_ref,
                 w0_ref, a0_ref, kk_ref, ka_ref, rk_ref, lng_ref, lnb_ref,
                 w2_ref, a2_ref, g2_ref,
                 o_ref,
                 s_ref, pr_ref, pk_ref, pv_ref, pgl_ref, pwl_ref, pal_ref, *, L):
    R = zr_ref.shape[0]
    hd = RWKV_HEAD_DIM
    c = pl.program_id(2)

    @pl.when(c == 0)
    def _():
        s_ref[...] = jnp.zeros_like(s_ref)
        for p in (pr_ref, pk_ref, pv_ref, pgl_ref, pwl_ref, pal_ref):
            p[...] = jnp.zeros_like(p)

    def shift(z_ref, p_ref, mu_ref):
        z = z_ref[...]
        row = lax.broadcasted_iota(jnp.int32, z.shape, 0)
        zp = jnp.where(row == 0, p_ref[0:1, :], pltpu.roll(z, 1, axis=0))
        p_ref[0:1, :] = z[R - 1:R, :]
        return z + (zp - z) * mu_ref[...]

    r = shift(zr_ref, pr_ref, mur_ref)
    k = shift(zk_ref, pk_ref, muk_ref)
    v = shift(zv_ref, pv_ref, muv_ref)
    gl = shift(zgl_ref, pgl_ref, mugl_ref)
    wl = shift(zwl_ref, pwl_ref, muwl_ref)
    al = shift(zal_ref, pal_ref, mual_ref)

    WB = zr_ref.shape[1]
    lane_r = lax.broadcasted_iota(jnp.int32, (WB, WB), 0)
    lane_c = lax.broadcasted_iota(jnp.int32, (WB, WB), 1)
    same_head_b = (lane_r // hd) == (lane_c // hd)
    ones_bd = jnp.where(same_head_b, 1.0, 0.0).astype(BF16)
    mean_bd = jnp.where(same_head_b, 1.0 / hd, 0.0).astype(BF16)
    same_head = same_head_b[0:LANES, 0:LANES]
    head0 = lax.broadcasted_iota(jnp.int32, (1, LANES), 1) < hd
    not_head0 = jnp.logical_not(head0)

    wpre = w0_ref[...] + _mm(jnp.tanh(wl), w2_ref[...], "nn", P_LORA_W)
    nw = -wpre
    softplus = jnp.maximum(nw, 0.0) + jnp.log(1.0 + jnp.exp(-jnp.abs(nw)))
    wlog = -softplus - 0.5
    a = jax.nn.sigmoid(a0_ref[...] + _mm(al, a2_ref[...], "nn", P_LORA))
    g = _mm(jax.nn.sigmoid(gl), g2_ref[...], "nn", P_LORA)
    kkv = k * kk_ref[...]
    kkn = kkv / jnp.maximum(jnp.sqrt(_mm_exact_rhs(kkv * kkv, ones_bd)), 1e-12)
    k2 = k * (1.0 + (a - 1.0) * ka_ref[...])
    ld = -jnp.exp(wlog)
    bv = kkn * a

    rr = lax.broadcasted_iota(jnp.int32, (R, R), 0)
    rc = lax.broadcasted_iota(jnp.int32, (R, R), 1)
    ltri = jnp.where((rr // L == rc // L) & (rr >= rc), 1.0, 0.0).astype(BF16)
    cum3 = _dot(ltri, jnp.concatenate(_split_bf16(ld, 3), axis=1))
    cum = cum3[:, 0:WB] + cum3[:, WB:2 * WB] + cum3[:, 2 * WB:3 * WB]

    S2 = 2 * L
    srow = lax.broadcasted_iota(jnp.int32, (S2, S2), 0)
    scol = lax.broadcasted_iota(jnp.int32, (S2, S2), 1)
    same_blk = (srow // L) == (scol // L)
    strict_bd = same_blk & (srow > scol)
    incl_bd = same_blk & (srow >= scol)
    tri_masks = []
    s = 1
    while s < L:
        tri_masks.append((srow // (2 * s) == scol // (2 * s)) & ((srow // s) % 2 == 1) & ((scol // s) % 2 == 0))
        s *= 2
    mid = L // 2 - 1
    zeros_l = jnp.zeros((L, LANES), F32)

    npb = WB // LANES
    nch = R // L
    streams = [(pp, j) for pp in range(npb) for j in range(nch)]
    eye = jnp.where(srow == scol, 1.0, 0.0)

    def piece(x, pp, j):
        return x[j * L:(j + 1) * L, pp * LANES:(pp + 1) * LANES]

    at_l, bt_l, kt_l, rt_l, v_l, rho_l, gam_l = [], [], [], [], [], [], []
    for pp, j in streams:
        cumj = piece(cum, pp, j)
        cmid = cumj[mid:mid + 1, :]
        cc = cumj - cmid
        e_neg = jnp.exp(-cc)
        at_l.append(-piece(kkn, pp, j) * jnp.exp(cc - piece(ld, pp, j)))
        bt_l.append(piece(bv, pp, j) * e_neg)
        kt_l.append(piece(k2, pp, j) * e_neg)
        rt_l.append(piece(r, pp, j) * jnp.exp(cc))
        v_l.append(piece(v, pp, j))
        rho_l.append(jnp.exp(cmid))
        gam_l.append(jnp.exp(cc[L - 1:L, :]))

    def stack_heads(x):
        return jnp.concatenate([jnp.where(head0, x, 0.0), jnp.where(not_head0, x, 0.0)], axis=0)

    ats_l = [stack_heads(x) for x in at_l]
    rts_l = [stack_heads(x) for x in rt_l]
    vs_l = [jnp.concatenate([x, x], axis=0) for x in v_l]
    sc_l = [_mm(jnp.concatenate([a_, r_], axis=0), jnp.concatenate([b_, b_, k_, k_], axis=0), "nt", P_SCORE)
            for a_, r_, b_, k_ in zip(ats_l, rts_l, bt_l, kt_l)]
    nab_l = [jnp.where(strict_bd, s_[0:S2, 0:S2], 0.0) for s_ in sc_l]
    nak_l = [jnp.where(strict_bd, s_[0:S2, S2:2 * S2], 0.0) for s_ in sc_l]
    mrb_l = [jnp.where(incl_bd, s_[S2:2 * S2, 0:S2], 0.0) for s_ in sc_l]
    mrk_l = [jnp.where(incl_bd, s_[S2:2 * S2, S2:2 * S2], 0.0) for s_ in sc_l]
    t_l = [eye + jnp.where(tri_masks[0], n_, 0.0) for n_ in nab_l]
    for m in tri_masks[1:]:
        ot_l = [_mm(jnp.where(m, n_, 0.0), t_, "nn", P_TRI) for n_, t_ in zip(nab_l, t_l)]
        t_l = [t_ + _mm(t_, ot_, "nn", P_TRI) for t_, ot_ in zip(t_l, ot_l)]
    nv_l = [_mm(n_, v_, "nn", P_MAIN) for n_, v_ in zip(nak_l, vs_l)]
    au_l = [_mm(t_, jnp.concatenate([a_, nv_], axis=1), "nn", P_MAIN) for t_, a_, nv_ in zip(t_l, ats_l, nv_l)]
    ry_l = [_mm(jnp.concatenate([mb_, mk_], axis=1),
                jnp.concatenate([au_, jnp.concatenate([jnp.zeros_like(v_), v_], axis=1)], axis=0), "nn", P_MAIN)
            for mb_, mk_, au_, v_ in zip(mrb_l, mrk_l, au_l, vs_l)]
    rbar_l = [rt_ + ry_[0:L, 0:LANES] + ry_[L:S2, 0:LANES] for rt_, ry_ in zip(rt_l, ry_l)]
    ybase_l = [jnp.where(head0, ry_[0:L, LANES:2 * LANES], ry_[L:S2, LANES:2 * LANES]) for ry_ in ry_l]
    gq_l = [_mm(jnp.concatenate(
                [jnp.concatenate([au_[0:L, 0:LANES] + au_[L:S2, 0:LANES],
                                  jnp.where(head0, au_[0:L, LANES:2 * LANES], au_[L:S2, LANES:2 * LANES])], axis=1),
                 jnp.concatenate([zeros_l, v_], axis=1)], axis=0),
                jnp.concatenate([b_, k_], axis=0), "tn", P_MAIN)
            for au_, v_, b_, k_ in zip(au_l, v_l, bt_l, kt_l)]
    gmat_l = [jnp.where(same_head, gq_[0:LANES], 0.0) for gq_ in gq_l]
    q0_l = [jnp.where(same_head, gq_[LANES:2 * LANES], 0.0) for gq_ in gq_l]
    st = [s_ref[pp] for pp in range(npb)]
    ys = [[None] * nch for _ in range(npb)]
    for j in range(nch):
        for pp in range(npb):
            i = pp * nch + j
            sr = st[pp] * rho_l[i]
            ys[pp][j] = _mm(rbar_l[i], sr, "nt", P_STATE) + ybase_l[i]
            st[pp] = (sr + _mm(sr, gmat_l[i], "nn", P_STATE) + q0_l[i]) * gam_l[i]
    for pp in range(npb):
        s_ref[pp] = st[pp]
    y = jnp.concatenate([jnp.concatenate(ys[pp], axis=0) for pp in range(npb)], axis=1)

    mu = _mm_exact_rhs(y, mean_bd)
    d = y - mu
    var = _mm_exact_rhs(d * d, mean_bd)
    yn = d * lax.rsqrt(var + RWKV_LN_EPS) * lng_ref[...] + lnb_ref[...]
    bonus = _mm_exact_rhs(r * k2 * rk_ref[...], ones_bd) * v
    o_ref[...] = ((yn + bonus) * g).astype(o_ref.dtype)


def rwkv_time_mix(z, mu_p, w0, w2, a0, a2, g2p, k_k, k_a, r_k, lnx_g, lnx_b, B, T, RW):
    L = RWKV_CHUNK
    R = L * RWKV_BLOCK_CHUNKS
    W = LANES
    WB = W * RWKV_BLOCK_PAIRS
    ngrp = RW // WB
    c_gl = 3 * RW // GATE_LORA_PAD
    c_wl = (3 * RW + GATE_LORA_PAD) // W
    c_al = c_wl + 1

    def zcol(base):
        return pl.BlockSpec((None, R, WB), lambda b, p, c: (b, c, base + p))

    def zfix(idx, width):
        return pl.BlockSpec((None, R, width), lambda b, p, c: (b, c, idx))

    def mucol(base):
        return pl.BlockSpec((1, WB), lambda b, p, c: (0, base + p))

    def mufix(idx, width):
        return pl.BlockSpec((1, width), lambda b, p, c: (0, idx))

    pvec = pl.BlockSpec((1, WB), lambda b, p, c: (0, p))

    def lora(rank):
        return pl.BlockSpec((rank, WB), lambda b, p, c: (0, p))

    row = lambda a: a.reshape(1, -1)
    return pl.pallas_call(
        functools.partial(_rwkv_kernel, L=L),
        grid=(B, ngrp, T // R),
        in_specs=[zcol(0), zcol(ngrp), zcol(2 * ngrp),
                  zfix(c_gl, GATE_LORA_PAD), zfix(c_wl, W), zfix(c_al, W),
                  mucol(0), mucol(ngrp), mucol(2 * ngrp),
                  mufix(c_gl, GATE_LORA_PAD), mufix(c_wl, W), mufix(c_al, W),
                  pvec, pvec, pvec, pvec, pvec, pvec, pvec,
                  lora(DECAY_LORA), lora(ICLR_LORA), lora(GATE_LORA_PAD)],
        out_specs=pl.BlockSpec((None, R, WB), lambda b, p, c: (b, c, p)),
        out_shape=jax.ShapeDtypeStruct((B, T, RW), BF16),
        scratch_shapes=[pltpu.VMEM((RWKV_BLOCK_PAIRS, W, W), F32)] + [pltpu.VMEM((8, WB), F32)] * 3
        + [pltpu.VMEM((8, GATE_LORA_PAD), F32), pltpu.VMEM((8, W), F32), pltpu.VMEM((8, W), F32)],
        compiler_params=_cparams(("parallel", "parallel", "arbitrary")),
        name="rwkv7",
    )(z, z, z, z, z, z, mu_p, mu_p, mu_p, mu_p, mu_p, mu_p,
      row(w0), row(a0), row(k_k), row(k_a), row(r_k), row(lnx_g), row(lnx_b), w2, a2, g2p)


def _pad_cols(w, n):
    return jnp.pad(w, ((0, 0), (0, n - w.shape[1])))


def kernel(x, c, ada_w, ada_b, ada_table, ln_g, ln_b, mix_w_in, mix_mu, rwkv_w0, rwkv_w2, rwkv_a0, rwkv_a2,
           rwkv_g2, rwkv_kk, rwkv_ka, rwkv_rk, rwkv_lnx_g, rwkv_lnx_b, mix_w_out, ffn_w_gate, ffn_w_up,
           ffn_w_down, pool_w, pool_scale, moe_router_w, moe_router_b, moe_w_gate, moe_w_up, moe_w_down):
    B, T, D = x.shape
    M = B * T
    depth = ada_table.shape[0]
    RW = rwkv_w0.shape[1]
    MW = D - RW
    n_moba = MW // MOBA_HEAD_DIM
    gate_lora = rwkv_g2.shape[1]
    alpha = (2 * depth) ** 0.25

    mods = ada_mod(c, ada_w, ada_b, ada_table)
    xf = x.reshape(M, D)
    h = modulate(xf, mods[0, :, 1], mods[0, :, 0], T)
    lnrow = lambda a: a.reshape(1, D)

    za = 3 * RW + GATE_LORA_PAD + DECAY_LORA + ICLR_LORA
    n_in = za + 3 * MW
    n_pad = -(-n_in // 512) * 512
    o_wl = 3 * RW
    o_al = o_wl + DECAY_LORA
    o_gl = o_al + ICLR_LORA
    o_q = o_gl + gate_lora

    def permute_cols(w):
        return jnp.concatenate(
            [w[..., :3 * RW], _pad_cols(w[..., o_gl:o_q], GATE_LORA_PAD), w[..., o_wl:o_al], w[..., o_al:o_gl],
             _pad_cols(w[..., o_q:], n_pad - za)], axis=-1)

    for l in range(depth):
        i = l // 2
        m = mods[l]
        g1, sh2, sc2, g2 = m[:, 2], m[:, 3], m[:, 4], m[:, 5]
        if l + 1 < depth:
            sh_n, sc_n = mods[l + 1, :, 0], mods[l + 1, :, 1]
        else:
            sh_n, sc_n = jnp.zeros_like(sh2), jnp.zeros_like(sc2)
        if l % 2 == 0:
            w_in = permute_cols(mix_w_in[i]).astype(BF16)
            mu_p = permute_cols(jnp.pad(mix_mu[i], (0, mix_w_in.shape[2] - mix_mu.shape[1]))[None, :])
            z = matmul(h, w_in, 512, 512).reshape(B, T, n_pad)
            g2p = jnp.pad(rwkv_g2[i], ((0, GATE_LORA_PAD - gate_lora), (0, 0)))
            ya = rwkv_time_mix(z, mu_p, rwkv_w0[i], rwkv_w2[i], rwkv_a0[i], rwkv_a2[i], g2p, rwkv_kk[i],
                               rwkv_ka[i], rwkv_rk[i], rwkv_lnx_g[i], rwkv_lnx_b[i], B, T, RW)
            yb = moba_attention(z, za // LANES, n_moba, B, T)
            ycat = jnp.concatenate([ya, yb], axis=-1).reshape(M, D)
            xf, h = matmul_resid_ln(ycat, mix_w_out[i].astype(BF16), xf, g1, lnrow(ln_g[l, 0]), lnrow(ln_b[l, 0]),
                                    sc2, sh2, T, alpha, 512, 1024)
            F = ffn_w_gate.shape[2]
            fp = -(-F // 512) * 512
            act = swiglu_up(h, _pad_cols(ffn_w_gate[i], fp).astype(BF16), _pad_cols(ffn_w_up[i], fp).astype(BF16),
                            1024, 512)
            wd = jnp.pad(ffn_w_down[i], ((0, fp - F), (0, 0))).astype(BF16)
            xf, h = matmul_resid_ln(act, wd, xf, g2, lnrow(ln_g[l, 1]), lnrow(ln_b[l, 1]), sc_n, sh_n, T, alpha,
                                    512, 1024)
        else:
            xf, h = pool_mixer_ln(xf, m[:, 1], m[:, 0], pool_w[i].astype(BF16), lnrow(pool_scale[i]), g1,
                                  lnrow(ln_g[l, 0]), lnrow(ln_b[l, 0]), sc2, sh2, T, alpha)
            E, _, NE = moe_w_gate.shape[1:]
            idx, gates = moe_router(h, moe_router_w[i], moe_router_b[i])
            row_token, row_gate, tile_e, n_used, pos0, pos1 = moe_group_metadata(idx, gates, E, MOE_TILE)
            xs = moe_gather_rows(h, row_token, MOE_TILE)
            act = moe_up_grouped(xs, row_gate, tile_e, n_used, moe_w_gate[i].astype(BF16), moe_w_up[i].astype(BF16),
                                 MOE_TILE, NE // 2)
            ys = moe_down_grouped(act, tile_e, n_used, moe_w_down[i].astype(BF16), MOE_TILE)
            xf, h = moe_combine_ln(ys, pos0, pos1, xf, g2, lnrow(ln_g[l, 1]), lnrow(ln_b[l, 1]), sc_n, sh_n, T, alpha)
    return xf.reshape(B, T, D)
```

```python
import functools
import math

import jax
import jax.numpy as jnp
from jax import lax
from jax.experimental import pallas as pl
from jax.experimental.pallas import tpu as pltpu

F32 = jnp.float32
BF16 = jnp.bfloat16
HI = lax.Precision.HIGHEST

RWKV_HEAD_DIM = 64
DECAY_LORA = 128
ICLR_LORA = 128
GATE_LORA_PAD = 512
RWKV_LN_EPS = 64e-5
MOBA_HEAD_DIM = 128
MOBA_BLOCK = 256
MOBA_TOPK = 3
ROPE_THETA = 10000.0
POOL_WINDOWS = (2, 4, 8, 16)
POOL_HALO = 16
MOE_TOPK = 2
N_MOD = 6
LN_EPS = 1e-5
NEG_INF = -1e30

LANES = 128
LN_ROWS = 64
MOE_TILE = 256
DMA_UNROLL = 8
RWKV_CHUNK = 64
RWKV_BLOCK_CHUNKS = 4
RWKV_BLOCK_PAIRS = 2
P_LORA_W, P_LORA, P_SCORE, P_TRI, P_MAIN, P_STATE = 1, 1, 1, 3, 1, 1
VMEM_LIMIT = 56 * 1024 * 1024
VMEM_LIMIT_DOWN = 60 * 1024 * 1024


def _cparams(sem, vmem_limit=VMEM_LIMIT):
    return pltpu.CompilerParams(dimension_semantics=sem, vmem_limit_bytes=vmem_limit)


def _dot(a, b, precision=None):
    return jnp.dot(a, b, preferred_element_type=F32, precision=precision)


def _dot_nt(a, b, precision=None):
    return lax.dot_general(a, b, (((1,), (1,)), ((), ())), preferred_element_type=F32, precision=precision)


def _dot_tn(a, b, precision=None):
    return lax.dot_general(a, b, (((0,), (0,)), ((), ())), preferred_element_type=F32, precision=precision)


def _ada_kernel(c_ref, w_ref, b_ref, tab_ref, o_ref):
    c = c_ref[...]
    sc = (c * jax.nn.sigmoid(c)).astype(BF16)
    acc = _dot(sc, w_ref[...].astype(BF16)) + b_ref[...]
    o_ref[...] = acc[None, :, :] + tab_ref[...][:, None, :]


def ada_mod(c, ada_w, ada_b, ada_table):
    B, D = c.shape
    N = ada_w.shape[1]
    depth = ada_table.shape[0]
    BP = 8
    cp = jnp.zeros((BP, D), F32).at[:B].set(c)
    tn = 1024
    out = pl.pallas_call(
        _ada_kernel,
        grid=(N // tn,),
        in_specs=[
            pl.BlockSpec((BP, D), lambda j: (0, 0)),
            pl.BlockSpec((D, tn), lambda j: (0, j)),
            pl.BlockSpec((1, tn), lambda j: (0, j)),
            pl.BlockSpec((depth, tn), lambda j: (0, j)),
        ],
        out_specs=pl.BlockSpec((depth, BP, tn), lambda j: (0, 0, j)),
        out_shape=jax.ShapeDtypeStruct((depth, BP, N), F32),
        compiler_params=_cparams(("parallel",)),
        name="ada_mod",
    )(cp, ada_w, ada_b.reshape(1, N), ada_table.reshape(depth, N))
    return out[:, :B].reshape(depth, B, N_MOD, 1, D)


def _cast_pad_kernel(x_ref, o_ref, *, rows, cols):
    br, bc = o_ref.shape
    x = x_ref[...]
    if rows % br or cols % bc:
        r = pl.program_id(0) * br + lax.broadcasted_iota(jnp.int32, (br, bc), 0)
        c = pl.program_id(1) * bc + lax.broadcasted_iota(jnp.int32, (br, bc), 1)
        x = jnp.where((r < rows) & (c < cols), x, 0.0)
    o_ref[...] = x.astype(o_ref.dtype)


def cast_pad_bf16(w, rows_p=None, cols_p=None, br=512, bc=1024):
    rows, cols = w.shape
    rows_p = rows_p or rows
    cols_p = cols_p or cols
    bc = min(bc, cols_p)
    last_r = (rows - 1) // br
    last_c = (cols - 1) // bc
    return pl.pallas_call(
        functools.partial(_cast_pad_kernel, rows=rows, cols=cols),
        grid=(rows_p // br, cols_p // bc),
        in_specs=[pl.BlockSpec((br, bc), lambda i, j: (jnp.minimum(i, last_r), jnp.minimum(j, last_c)))],
        out_specs=pl.BlockSpec((br, bc), lambda i, j: (i, j)),
        out_shape=jax.ShapeDtypeStruct((rows_p, cols_p), BF16),
        compiler_params=_cparams(("parallel", "parallel")),
        name="cast_pad",
    )(w)


def _modulate_kernel(x_ref, sc_ref, sh_ref, o_ref):
    o_ref[...] = (x_ref[...] * (1.0 + sc_ref[...]) + sh_ref[...]).astype(o_ref.dtype)


def modulate(x, sc, sh, T):
    M, D = x.shape
    tm = 512
    tpb = T // tm
    vec = pl.BlockSpec((None, 1, D), lambda i: (i // tpb, 0, 0))
    return pl.pallas_call(
        _modulate_kernel,
        grid=(M // tm,),
        in_specs=[pl.BlockSpec((tm, D), lambda i: (i, 0)), vec, vec],
        out_specs=pl.BlockSpec((tm, D), lambda i: (i, 0)),
        out_shape=jax.ShapeDtypeStruct((M, D), BF16),
        compiler_params=_cparams(("parallel",)),
        name="modulate",
    )(x, sc, sh)


def _mm_kernel(a_ref, w_ref, o_ref):
    o_ref[...] = _dot(a_ref[...], w_ref[...]).astype(o_ref.dtype)


def matmul(a, w, tm, tn, out_dtype=F32):
    M, K = a.shape
    N = w.shape[1]
    return pl.pallas_call(
        _mm_kernel,
        grid=(N // tn, M // tm),
        in_specs=[pl.BlockSpec((tm, K), lambda j, i: (i, 0)),
                  pl.BlockSpec((K, tn), lambda j, i: (0, j))],
        out_specs=pl.BlockSpec((tm, tn), lambda j, i: (i, j)),
        out_shape=jax.ShapeDtypeStruct((M, N), out_dtype),
        compiler_params=_cparams(("parallel", "parallel")),
        name="matmul",
    )(a, w)


def _resid_ln_mod(x, y, gate, lng, lnb, sc, sh, alpha):
    r = alpha * x + gate * y
    mu = jnp.mean(r, axis=-1, keepdims=True)
    d = r - mu
    var = jnp.mean(d * d, axis=-1, keepdims=True)
    xn = d * lax.rsqrt(var + LN_EPS) * lng + lnb
    return xn, xn * (1.0 + sc) + sh


def _mm_ln_kernel(a_ref, w_ref, x_ref, gate_ref, lng_ref, lnb_ref, sc_ref, sh_ref,
                  xo_ref, ho_ref, *, alpha):
    k = pl.program_id(1)

    @pl.when(k == 0)
    def _():
        xo_ref[...] = _dot(a_ref[...], w_ref[...])

    @pl.when(k > 0)
    def _():
        xo_ref[...] += _dot(a_ref[...], w_ref[...])

    @pl.when(k == pl.num_programs(1) - 1)
    def _():
        for r0 in range(0, xo_ref.shape[0], LN_ROWS):
            rows = pl.ds(r0, LN_ROWS)
            xn, h = _resid_ln_mod(x_ref[rows, :], xo_ref[rows, :], gate_ref[...], lng_ref[...], lnb_ref[...],
                                  sc_ref[...], sh_ref[...], alpha)
            xo_ref[rows, :] = xn
            ho_ref[rows, :] = h.astype(ho_ref.dtype)


def matmul_resid_ln(a, w, x, gate, lng, lnb, sc, sh, T, alpha, tm, tk):
    M, K = a.shape
    D = w.shape[1]
    tpb = T // tm
    vec_b = pl.BlockSpec((None, 1, D), lambda i, k: (i // tpb, 0, 0))
    vec = pl.BlockSpec((1, D), lambda i, k: (0, 0))
    row = pl.BlockSpec((tm, D), lambda i, k: (i, 0))
    return pl.pallas_call(
        functools.partial(_mm_ln_kernel, alpha=alpha),
        grid=(M // tm, K // tk),
        in_specs=[pl.BlockSpec((tm, tk), lambda i, k: (i, k)),
                  pl.BlockSpec((tk, D), lambda i, k: (k, 0)),
                  pl.BlockSpec((tm, D), lambda i, k: (i, 0), pipeline_mode=pl.Buffered(1)),
                  vec_b, vec, vec, vec_b, vec_b],
        out_specs=[row, row],
        out_shape=[jax.ShapeDtypeStruct((M, D), F32), jax.ShapeDtypeStruct((M, D), BF16)],
        compiler_params=_cparams(("parallel", "arbitrary"), VMEM_LIMIT_DOWN),
        name="matmul_resid_ln",
    )(a, w, x, gate, lng, lnb, sc, sh)


def _swiglu_up_kernel(a_ref, wg_ref, wu_ref, o_ref):
    a = a_ref[...]
    g = _dot(a, wg_ref[...])
    u = _dot(a, wu_ref[...])
    o_ref[...] = (g * jax.nn.sigmoid(g) * u).astype(o_ref.dtype)


def swiglu_up(a, wg, wu, tm, tn):
    M, K = a.shape
    N = wg.shape[1]
    wspec = pl.BlockSpec((K, tn), lambda i, j: (0, j))
    return pl.pallas_call(
        _swiglu_up_kernel,
        grid=(M // tm, N // tn),
        in_specs=[pl.BlockSpec((tm, K), lambda i, j: (i, 0)), wspec, wspec],
        out_specs=pl.BlockSpec((tm, tn), lambda i, j: (i, j)),
        out_shape=jax.ShapeDtypeStruct((M, N), BF16),
        compiler_params=_cparams(("parallel", "parallel")),
        name="swiglu_up",
    )(a, wg, wu)


def _router_kernel(h_ref, rw_ref, rb_ref, idx_ref, gate_ref, *, n_experts):
    logits = _dot(h_ref[...], rw_ref[...], HI) + rb_ref[...]
    lane = lax.broadcasted_iota(jnp.int32, logits.shape, 1)
    logits = jnp.where(lane < n_experts, logits, -jnp.inf)
    m1 = jnp.max(logits, axis=-1, keepdims=True)
    i1 = jnp.min(jnp.where(logits == m1, lane, LANES), axis=-1, keepdims=True)
    oh1 = lane == i1
    rest = jnp.where(oh1, -jnp.inf, logits)
    m2 = jnp.max(rest, axis=-1, keepdims=True)
    i2 = jnp.min(jnp.where(rest == m2, lane, LANES), axis=-1, keepdims=True)
    e = jnp.exp(m2 - m1)
    g1 = 1.0 / (1.0 + e)
    g2 = e / (1.0 + e)
    idx_ref[...] = jnp.where(lane == 0, i1, jnp.where(lane == 1, i2, 0))
    gate_ref[...] = jnp.where(lane == 0, g1, jnp.where(lane == 1, g2, 0.0))


def moe_router(h, router_w, router_b):
    M, D = h.shape
    E = router_w.shape[1]
    tm = 512
    rw = jnp.zeros((D, LANES), F32).at[:, :E].set(router_w)
    rb = jnp.zeros((1, LANES), F32).at[0, :E].set(router_b)
    out = pl.BlockSpec((tm, LANES), lambda i: (i, 0))
    return pl.pallas_call(
        functools.partial(_router_kernel, n_experts=E),
        grid=(M // tm,),
        in_specs=[pl.BlockSpec((tm, D), lambda i: (i, 0)),
                  pl.BlockSpec((D, LANES), lambda i: (0, 0)),
                  pl.BlockSpec((1, LANES), lambda i: (0, 0))],
        out_specs=[out, out],
        out_shape=[jax.ShapeDtypeStruct((M, LANES), jnp.int32), jax.ShapeDtypeStruct((M, LANES), F32)],
        compiler_params=_cparams(("parallel",)),
        name="moe_router",
    )(h, rw, rb)


def _gather_rows_kernel(tok_ref, h_hbm, o_ref, buf_ref, sem):
    i = pl.program_id(0)
    tm = o_ref.shape[0]

    def start(r0, c):
        for u in range(DMA_UNROLL):
            r = r0 * DMA_UNROLL + u
            pltpu.make_async_copy(h_hbm.at[pl.ds(tok_ref[i * tm + r], 1), :], buf_ref.at[pl.ds(r, 1), :],
                                  sem.at[0]).start()
        return c

    lax.fori_loop(0, tm // DMA_UNROLL, start, 0)
    pltpu.make_async_copy(h_hbm.at[pl.ds(0, tm), :], buf_ref, sem.at[0]).wait()
    o_ref[...] = buf_ref[...].astype(o_ref.dtype)


def moe_gather_rows(h, row_token, tm):
    D = h.shape[1]
    NR = row_token.shape[0]
    return pl.pallas_call(
        _gather_rows_kernel,
        grid_spec=pltpu.PrefetchScalarGridSpec(
            num_scalar_prefetch=1,
            grid=(NR // tm,),
            in_specs=[pl.BlockSpec(memory_space=pl.ANY)],
            out_specs=pl.BlockSpec((tm, D), lambda i, tok: (i, 0)),
            scratch_shapes=[pltpu.VMEM((tm, D), F32), pltpu.SemaphoreType.DMA((1,))]),
        out_shape=jax.ShapeDtypeStruct((NR, D), BF16),
        compiler_params=_cparams(("arbitrary",)),
        name="moe_gather",
    )(row_token, h)


def _moe_up_kernel(te_ref, nu_ref, a_ref, wg_ref, wu_ref, o_ref):
    i = pl.program_id(1)

    @pl.when(i < nu_ref[0])
    def _():
        a = a_ref[...]
        g = _dot(a, wg_ref[...])
        u = _dot(a, wu_ref[...])
        o_ref[...] = (g * jax.nn.sigmoid(g) * u).astype(o_ref.dtype)

    @pl.when(i >= nu_ref[0])
    def _():
        o_ref[...] = jnp.zeros_like(o_ref)


def moe_up_grouped(xs, tile_e, n_used, wg, wu, tm, tn):
    NR, K = xs.shape
    E, _, NE = wg.shape
    wspec = pl.BlockSpec((None, K, tn), lambda j, i, te, nu: (te[i], 0, j))
    return pl.pallas_call(
        _moe_up_kernel,
        grid_spec=pltpu.PrefetchScalarGridSpec(
            num_scalar_prefetch=2,
            grid=(NE // tn, NR // tm),
            in_specs=[pl.BlockSpec((tm, K), lambda j, i, te, nu: (i, 0)), wspec, wspec],
            out_specs=pl.BlockSpec((tm, tn), lambda j, i, te, nu: (i, j))),
        out_shape=jax.ShapeDtypeStruct((NR, NE), BF16),
        compiler_params=_cparams(("parallel", "parallel")),
        name="moe_up",
    )(tile_e, n_used, xs, wg, wu)


def _moe_down_kernel(te_ref, nu_ref, a_ref, w_ref, o_ref):
    i = pl.program_id(0)

    @pl.when(i < nu_ref[0])
    def _():
        o_ref[...] = _dot(a_ref[...], w_ref[...])

    @pl.when(i >= nu_ref[0])
    def _():
        o_ref[...] = jnp.zeros_like(o_ref)


def moe_down_grouped(act, tile_e, n_used, wd, tm):
    NR, NE = act.shape
    D = wd.shape[2]
    return pl.pallas_call(
        _moe_down_kernel,
        grid_spec=pltpu.PrefetchScalarGridSpec(
            num_scalar_prefetch=2,
            grid=(NR // tm,),
            in_specs=[pl.BlockSpec((tm, NE), lambda i, te, nu: (i, 0)),
                      pl.BlockSpec((None, NE, D), lambda i, te, nu: (te[i], 0, 0))],
            out_specs=pl.BlockSpec((tm, D), lambda i, te, nu: (i, 0))),
        out_shape=jax.ShapeDtypeStruct((NR, D), F32),
        compiler_params=_cparams(("parallel",)),
        name="moe_down",
    )(tile_e, n_used, act, wd)


def _moe_combine_ln_kernel(p0_ref, p1_ref, ys_hbm, x_ref, rg_ref, gate_ref, lng_ref, lnb_ref, sc_ref, sh_ref,
                           xo_ref, ho_ref, b0_ref, b1_ref, sem, *, alpha):
    i = pl.program_id(0)
    tm = x_ref.shape[0]

    def start(r0, c):
        for u in range(DMA_UNROLL):
            r = r0 * DMA_UNROLL + u
            t = i * tm + r
            pltpu.make_async_copy(ys_hbm.at[pl.ds(p0_ref[t], 1), :], b0_ref.at[pl.ds(r, 1), :], sem.at[0]).start()
            pltpu.make_async_copy(ys_hbm.at[pl.ds(p1_ref[t], 1), :], b1_ref.at[pl.ds(r, 1), :], sem.at[1]).start()
        return c

    lax.fori_loop(0, tm // DMA_UNROLL, start, 0)
    pltpu.make_async_copy(ys_hbm.at[pl.ds(0, tm), :], b0_ref, sem.at[0]).wait()
    pltpu.make_async_copy(ys_hbm.at[pl.ds(0, tm), :], b1_ref, sem.at[1]).wait()
    for r0 in range(0, tm, LN_ROWS):
        rows = pl.ds(r0, LN_ROWS)
        rg = rg_ref[rows, :]
        y = rg[:, 0:1] * b0_ref[rows, :] + rg[:, 1:2] * b1_ref[rows, :]
        xn, h = _resid_ln_mod(x_ref[rows, :], y, gate_ref[...], lng_ref[...], lnb_ref[...],
                              sc_ref[...], sh_ref[...], alpha)
        xo_ref[rows, :] = xn
        ho_ref[rows, :] = h.astype(ho_ref.dtype)


def moe_combine_ln(ys, pos0, pos1, rgates, x, gate, lng, lnb, sc, sh, T, alpha):
    M, D = x.shape
    tm = 256
    tpb = T // tm
    vec_b = pl.BlockSpec((None, 1, D), lambda i, p0, p1: (i // tpb, 0, 0))
    vec = pl.BlockSpec((1, D), lambda i, p0, p1: (0, 0))
    row = pl.BlockSpec((tm, D), lambda i, p0, p1: (i, 0))
    return pl.pallas_call(
        functools.partial(_moe_combine_ln_kernel, alpha=alpha),
        grid_spec=pltpu.PrefetchScalarGridSpec(
            num_scalar_prefetch=2,
            grid=(M // tm,),
            in_specs=[pl.BlockSpec(memory_space=pl.ANY), row,
                      pl.BlockSpec((tm, LANES), lambda i, p0, p1: (i, 0)), vec_b, vec, vec, vec_b, vec_b],
            out_specs=[row, row],
            scratch_shapes=[pltpu.VMEM((tm, D), F32), pltpu.VMEM((tm, D), F32), pltpu.SemaphoreType.DMA((2,))]),
        out_shape=[jax.ShapeDtypeStruct((M, D), F32), jax.ShapeDtypeStruct((M, D), BF16)],
        compiler_params=_cparams(("arbitrary",)),
        name="moe_combine_ln",
    )(pos0, pos1, ys, x, rgates, gate, lng, lnb, sc, sh)


def moe_group_metadata(idx, n_experts, tm):
    M = idx.shape[0]
    n_pairs = MOE_TOPK * M
    e_flat = idx[:, :MOE_TOPK].reshape(-1)
    onehot = (e_flat[:, None] == jnp.arange(n_experts)[None, :]).astype(F32)
    blk = LANES
    oh3 = onehot.reshape(n_pairs // blk, blk, n_experts)
    tri = (jnp.arange(blk)[:, None] > jnp.arange(blk)[None, :]).astype(F32)
    within = jnp.einsum("ts,bse->bte", tri, oh3, precision=HI)
    tot = oh3.sum(axis=1)
    before = jnp.cumsum(tot, axis=0) - tot
    excl = (within + before[:, None, :]).reshape(n_pairs, n_experts)
    rank = jnp.sum(excl * onehot, axis=1).astype(jnp.int32)
    counts = tot.sum(axis=0).astype(jnp.int32)
    padded = -(-counts // tm) * tm
    ends = jnp.cumsum(padded)
    starts = ends - padded
    pos = jnp.sum(onehot * starts[None, :].astype(F32), axis=1).astype(jnp.int32) + rank
    n_rows = n_pairs + n_experts * tm
    row_token = jnp.zeros((n_rows,), jnp.int32).at[pos].set(jnp.arange(n_pairs, dtype=jnp.int32) // MOE_TOPK)
    tile_start = jnp.arange(n_rows // tm, dtype=jnp.int32) * tm
    tile_e = jnp.minimum(jnp.sum(tile_start[:, None] >= ends[None, :], axis=1), n_experts - 1).astype(jnp.int32)
    n_used = (ends[-1:] // tm).astype(jnp.int32)
    pos = pos.reshape(M, MOE_TOPK)
    return row_token, tile_e, n_used, pos[:, 0], pos[:, 1]


def _pool_kernel(x_ref, xh_ref, sc_ref, sh_ref, w_ref, ps_ref, gate_ref, lng_ref, lnb_ref,
                 sc2_ref, sh2_ref, xo_ref, ho_ref, hh_ref, y_ref, *, tm, tpb, alpha):
    i = pl.program_id(0)
    D = x_ref.shape[1]
    dg = D // len(POOL_WINDOWS)
    sc = sc_ref[...]
    sh = sh_ref[...]
    x = x_ref[...]
    first = (i % tpb) == 0
    halo = xh_ref[...] * (1.0 + sc) + sh
    hh_ref[0:POOL_HALO, :] = jnp.where(first, 0.0, halo)
    hh_ref[POOL_HALO:, :] = x * (1.0 + sc) + sh
    t = (i % tpb) * tm + lax.broadcasted_iota(jnp.int32, (tm, 1), 0)
    tcnt = (t + 1).astype(F32)
    for g, win in enumerate(POOL_WINDOWS):
        cols = slice(g * dg, (g + 1) * dg)
        h = hh_ref[POOL_HALO:, cols]
        s = h
        for d in range(1, win):
            s = s + hh_ref[pl.ds(POOL_HALO - d, tm), cols]
        inv = 1.0 / jnp.minimum(tcnt, float(win))
        p = (s * inv - h).astype(BF16)
        y_ref[:, cols] = _dot(p, w_ref[g])
    y = y_ref[...] * ps_ref[...]
    xn, hn = _resid_ln_mod(x, y, gate_ref[...], lng_ref[...], lnb_ref[...], sc2_ref[...], sh2_ref[...], alpha)
    xo_ref[...] = xn
    ho_ref[...] = hn.astype(ho_ref.dtype)


def pool_mixer_ln(x, sc, sh, pool_w, pool_scale, gate, lng, lnb, sc2, sh2, T, alpha):
    M, D = x.shape
    G, dg, _ = pool_w.shape
    tm = 256
    tpb = T // tm
    hb = tm // POOL_HALO
    vec_b = pl.BlockSpec((None, 1, D), lambda i: (i // tpb, 0, 0))
    vec = pl.BlockSpec((1, D), lambda i: (0, 0))
    row = pl.BlockSpec((tm, D), lambda i: (i, 0))
    return pl.pallas_call(
        functools.partial(_pool_kernel, tm=tm, tpb=tpb, alpha=alpha),
        grid=(M // tm,),
        in_specs=[row,
                  pl.BlockSpec((POOL_HALO, D), lambda i: (jnp.maximum(i * hb - 1, 0), 0)),
                  vec_b, vec_b,
                  pl.BlockSpec((G, dg, dg), lambda i: (0, 0, 0)),
                  vec, vec_b, vec, vec, vec_b, vec_b],
        out_specs=[row, row],
        out_shape=[jax.ShapeDtypeStruct((M, D), F32), jax.ShapeDtypeStruct((M, D), F32)],
        scratch_shapes=[pltpu.VMEM((tm + POOL_HALO, D), F32), pltpu.VMEM((tm, D), F32)],
        compiler_params=_cparams(("parallel",)),
        name="pool_mixer",
    )(x, x, sc, sh, pool_w, pool_scale, gate, lng, lnb, sc2, sh2)


def _moba_kernel(q_ref, k_ref, v_ref, cos_ref, sin_ref, o_ref, *, nb):
    bs = MOBA_BLOCK
    cos = cos_ref[...]
    sin = sin_ref[...]

    def rope(t):
        return t * cos + pltpu.roll(t, MOBA_HEAD_DIM // 2, axis=1) * sin

    qr = rope(q_ref[...])
    kr = rope(k_ref[...])
    T = qr.shape[0]
    prow = lax.broadcasted_iota(jnp.int32, (LANES, T), 0)
    pcol = lax.broadcasted_iota(jnp.int32, (LANES, T), 1)
    pool = jnp.where(pcol // bs == prow, 1.0 / bs, 0.0).astype(F32)
    kmean = _dot(pool, kr, HI)
    qb = qr.astype(BF16)
    kb = kr.astype(BF16)
    vb = v_ref[...].astype(BF16)
    scale = MOBA_HEAD_DIM ** -0.5
    lane = lax.broadcasted_iota(jnp.int32, (bs, LANES), 1)
    rowi = lax.broadcasted_iota(jnp.int32, (bs, bs), 0)
    coli = lax.broadcasted_iota(jnp.int32, (bs, bs), 1)
    for i in range(nb):
        rows = slice(i * bs, (i + 1) * bs)
        qi = qb[rows]
        s_list = [jnp.where(coli <= rowi, _dot_nt(qi, kb[rows]) * scale, NEG_INF)]
        if i > 0:
            gate = _dot_nt(qr[rows], kmean, HI)
            valid = lane < i
            gate = jnp.where(valid, gate, -jnp.inf)
            rank = jnp.zeros((bs, LANES), jnp.int32)
            for j in range(i):
                gj = gate[:, j:j + 1]
                ahead = (gj > gate) | ((gj == gate) & (j < lane))
                rank = rank + ahead.astype(jnp.int32)
            sel = valid & (rank < MOBA_TOPK)
            for j in range(i):
                sj = _dot_nt(qi, kb[j * bs:(j + 1) * bs]) * scale
                s_list.append(jnp.where(sel[:, j:j + 1], sj, NEG_INF))
        m = s_list[0].max(axis=-1, keepdims=True)
        for s in s_list[1:]:
            m = jnp.maximum(m, s.max(axis=-1, keepdims=True))
        l = jnp.zeros((bs, 1), F32)
        acc = jnp.zeros((bs, MOBA_HEAD_DIM), F32)
        for jj, s in enumerate(s_list):
            j = i if jj == 0 else jj - 1
            p = jnp.exp(s - m)
            l = l + p.sum(axis=-1, keepdims=True)
            acc = acc + _dot(p.astype(BF16), vb[j * bs:(j + 1) * bs])
        o_ref[rows, :] = (acc / l).astype(o_ref.dtype)


def moba_attention(z, q_col, n_heads, B, T):
    assert T % MOBA_BLOCK == 0
    nb = T // MOBA_BLOCK
    dh = MOBA_HEAD_DIM
    inv = jnp.power(ROPE_THETA, -jnp.arange(0, dh, 2, dtype=F32) / dh)
    ang = jnp.arange(T, dtype=F32)[:, None] * inv[None, :]
    cos = jnp.concatenate([jnp.cos(ang), jnp.cos(ang)], axis=-1)
    sin = jnp.concatenate([-jnp.sin(ang), jnp.sin(ang)], axis=-1)

    def col(off):
        return pl.BlockSpec((None, T, dh), lambda b, h: (b, 0, q_col + off * n_heads + h))

    tab = pl.BlockSpec((T, dh), lambda b, h: (0, 0))
    return pl.pallas_call(
        functools.partial(_moba_kernel, nb=nb),
        grid=(B, n_heads),
        in_specs=[col(0), col(1), col(2), tab, tab],
        out_specs=pl.BlockSpec((None, T, dh), lambda b, h: (b, 0, h)),
        out_shape=jax.ShapeDtypeStruct((B, T, n_heads * dh), BF16),
        compiler_params=_cparams(("parallel", "parallel")),
        name="moba",
    )(z, z, z, cos, sin)


def _split_bf16(x, n):
    parts, rem = [], x
    for i in range(n):
        p = rem.astype(BF16)
        parts.append(p)
        if i + 1 < n:
            rem = rem - p.astype(F32)
    return parts


_DIMS = {"nn": ((1,), (0,)), "nt": ((1,), (1,)), "tn": ((0,), (0,))}


def _mm(a, b, form="nn", passes=1):
    if passes == 1:
        a3, b3 = a.astype(BF16), b.astype(BF16)
    else:
        ah, al = _split_bf16(a, 2)
        bh, bl = _split_bf16(b, 2)
        a3 = jnp.concatenate([ah, ah, al], axis=0 if form == "tn" else 1)
        b3 = jnp.concatenate([bh, bl, bh], axis=1 if form == "nt" else 0)
    return lax.dot_general(a3, b3, (_DIMS[form], ((), ())), preferred_element_type=F32)


def _mm_exact_rhs(a, b_bf16):
    a3 = jnp.concatenate(_split_bf16(a, 3), axis=1)
    b3 = jnp.concatenate([b_bf16, b_bf16, b_bf16], axis=0)
    return _dot(a3, b3)


def _rwkv_kernel(zr_ref, zk_ref, zv_ref, zgl_ref, zwl_ref, zal_ref,
                 mur_ref, muk_ref, muv_ref, mugl_ref, muwl_ref, mual_ref,
                 w0_ref, a0_ref, kk_ref, ka_ref, rk_ref, lng_ref, lnb_ref,
                 w2_ref, a2_ref, g2_ref,
                 o_ref,
                 s_ref, pr_ref, pk_ref, pv_ref, pgl_ref, pwl_ref, pal_ref, *, L):
    R = zr_ref.shape[0]
    hd = RWKV_HEAD_DIM
    c = pl.program_id(2)

    @pl.when(c == 0)
    def _():
        s_ref[...] = jnp.zeros_like(s_ref)
        for p in (pr_ref, pk_ref, pv_ref, pgl_ref, pwl_ref, pal_ref):
            p[...] = jnp.zeros_like(p)

    def shift(z_ref, p_ref, mu_ref):
        z = z_ref[...]
        row = lax.broadcasted_iota(jnp.int32, z.shape, 0)
        zp = jnp.where(row == 0, p_ref[0:1, :], pltpu.roll(z, 1, axis=0))
        p_ref[0:1, :] = z[R - 1:R, :]
        return z + (zp - z) * mu_ref[...]

    r = shift(zr_ref, pr_ref, mur_ref)
    k = shift(zk_ref, pk_ref, muk_ref)
    v = shift(zv_ref, pv_ref, muv_ref)
    gl = shift(zgl_ref, pgl_ref, mugl_ref)
    wl = shift(zwl_ref, pwl_ref, muwl_ref)
    al = shift(zal_ref, pal_ref, mual_ref)

    WB = zr_ref.shape[1]
    lane_r = lax.broadcasted_iota(jnp.int32, (WB, WB), 0)
    lane_c = lax.broadcasted_iota(jnp.int32, (WB, WB), 1)
    same_head_b = (lane_r // hd) == (lane_c // hd)
    ones_bd = jnp.where(same_head_b, 1.0, 0.0).astype(BF16)
    mean_bd = jnp.where(same_head_b, 1.0 / hd, 0.0).astype(BF16)
    same_head = same_head_b[0:LANES, 0:LANES]
    head0 = lax.broadcasted_iota(jnp.int32, (1, LANES), 1) < hd
    not_head0 = jnp.logical_not(head0)

    wpre = w0_ref[...] + _mm(jnp.tanh(wl), w2_ref[...], "nn", P_LORA_W)
    nw = -wpre
    softplus = jnp.maximum(nw, 0.0) + jnp.log(1.0 + jnp.exp(-jnp.abs(nw)))
    wlog = -softplus - 0.5
    a = jax.nn.sigmoid(a0_ref[...] + _mm(al, a2_ref[...], "nn", P_LORA))
    g = _mm(jax.nn.sigmoid(gl), g2_ref[...], "nn", P_LORA)
    kkv = k * kk_ref[...]
    kkn = kkv / jnp.maximum(jnp.sqrt(_mm_exact_rhs(kkv * kkv, ones_bd)), 1e-12)
    k2 = k * (1.0 + (a - 1.0) * ka_ref[...])
    ld = -jnp.exp(wlog)
    bv = kkn * a

    rr = lax.broadcasted_iota(jnp.int32, (R, R), 0)
    rc = lax.broadcasted_iota(jnp.int32, (R, R), 1)
    ltri = jnp.where((rr // L == rc // L) & (rr >= rc), 1.0, 0.0).astype(BF16)
    cum3 = _dot(ltri, jnp.concatenate(_split_bf16(ld, 3), axis=1))
    cum = cum3[:, 0:WB] + cum3[:, WB:2 * WB] + cum3[:, 2 * WB:3 * WB]

    S2 = 2 * L
    srow = lax.broadcasted_iota(jnp.int32, (S2, S2), 0)
    scol = lax.broadcasted_iota(jnp.int32, (S2, S2), 1)
    same_blk = (srow // L) == (scol // L)
    strict_bd = same_blk & (srow > scol)
    incl_bd = same_blk & (srow >= scol)
    tri_masks = []
    s = 1
    while s < L:
        tri_masks.append((srow // (2 * s) == scol // (2 * s)) & ((srow // s) % 2 == 1) & ((scol // s) % 2 == 0))
        s *= 2
    mid = L // 2 - 1
    zeros_l = jnp.zeros((L, LANES), F32)

    npb = WB // LANES
    nch = R // L
    streams = [(pp, j) for pp in range(npb) for j in range(nch)]
    eye = jnp.where(srow == scol, 1.0, 0.0)

    def piece(x, pp, j):
        return x[j * L:(j + 1) * L, pp * LANES:(pp + 1) * LANES]

    at_l, bt_l, kt_l, rt_l, v_l, rho_l, gam_l = [], [], [], [], [], [], []
    for pp, j in streams:
        cumj = piece(cum, pp, j)
        cmid = cumj[mid:mid + 1, :]
        cc = cumj - cmid
        e_neg = jnp.exp(-cc)
        at_l.append(-piece(kkn, pp, j) * jnp.exp(cc - piece(ld, pp, j)))
        bt_l.append(piece(bv, pp, j) * e_neg)
        kt_l.append(piece(k2, pp, j) * e_neg)
        rt_l.append(piece(r, pp, j) * jnp.exp(cc))
        v_l.append(piece(v, pp, j))
        rho_l.append(jnp.exp(cmid))
        gam_l.append(jnp.exp(cc[L - 1:L, :]))

    def stack_heads(x):
        return jnp.concatenate([jnp.where(head0, x, 0.0), jnp.where(not_head0, x, 0.0)], axis=0)

    ats_l = [stack_heads(x) for x in at_l]
    rts_l = [stack_heads(x) for x in rt_l]
    vs_l = [jnp.concatenate([x, x], axis=0) for x in v_l]
    sc_l = [_mm(jnp.concatenate([a_, r_], axis=0), jnp.concatenate([b_, b_, k_, k_], axis=0), "nt", P_SCORE)
            for a_, r_, b_, k_ in zip(ats_l, rts_l, bt_l, kt_l)]
    nab_l = [jnp.where(strict_bd, s_[0:S2, 0:S2], 0.0) for s_ in sc_l]
    nak_l = [jnp.where(strict_bd, s_[0:S2, S2:2 * S2], 0.0) for s_ in sc_l]
    mrb_l = [jnp.where(incl_bd, s_[S2:2 * S2, 0:S2], 0.0) for s_ in sc_l]
    mrk_l = [jnp.where(incl_bd, s_[S2:2 * S2, S2:2 * S2], 0.0) for s_ in sc_l]
    t_l = [eye + jnp.where(tri_masks[0], n_, 0.0) for n_ in nab_l]
    for m in tri_masks[1:]:
        ot_l = [_mm(jnp.where(m, n_, 0.0), t_, "nn", P_TRI) for n_, t_ in zip(nab_l, t_l)]
        t_l = [t_ + _mm(t_, ot_, "nn", P_TRI) for t_, ot_ in zip(t_l, ot_l)]
    nv_l = [_mm(n_, v_, "nn", P_MAIN) for n_, v_ in zip(nak_l, vs_l)]
    au_l = [_mm(t_, jnp.concatenate([a_, nv_], axis=1), "nn", P_MAIN) for t_, a_, nv_ in zip(t_l, ats_l, nv_l)]
    ry_l = [_mm(jnp.concatenate([mb_, mk_], axis=1),
                jnp.concatenate([au_, jnp.concatenate([jnp.zeros_like(v_), v_], axis=1)], axis=0), "nn", P_MAIN)
            for mb_, mk_, au_, v_ in zip(mrb_l, mrk_l, au_l, vs_l)]
    rbar_l = [rt_ + ry_[0:L, 0:LANES] + ry_[L:S2, 0:LANES] for rt_, ry_ in zip(rt_l, ry_l)]
    ybase_l = [jnp.where(head0, ry_[0:L, LANES:2 * LANES], ry_[L:S2, LANES:2 * LANES]) for ry_ in ry_l]
    gq_l = [_mm(jnp.concatenate(
                [jnp.concatenate([au_[0:L, 0:LANES] + au_[L:S2, 0:LANES],
                                  jnp.where(head0, au_[0:L, LANES:2 * LANES], au_[L:S2, LANES:2 * LANES])], axis=1),
                 jnp.concatenate([zeros_l, v_], axis=1)], axis=0),
                jnp.concatenate([b_, k_], axis=0), "tn", P_MAIN)
            for au_, v_, b_, k_ in zip(au_l, v_l, bt_l, kt_l)]
    gmat_l = [jnp.where(same_head, gq_[0:LANES], 0.0) for gq_ in gq_l]
    q0_l = [jnp.where(same_head, gq_[LANES:2 * LANES], 0.0) for gq_ in gq_l]
    st = [s_ref[pp] for pp in range(npb)]
    ys = [[None] * nch for _ in range(npb)]
    for j in range(nch):
        for pp in range(npb):
            i = pp * nch + j
            sr = st[pp] * rho_l[i]
            ys[pp][j] = _mm(rbar_l[i], sr, "nt", P_STATE) + ybase_l[i]
            st[pp] = (sr + _mm(sr, gmat_l[i], "nn", P_STATE) + q0_l[i]) * gam_l[i]
    for pp in range(npb):
        s_ref[pp] = st[pp]
    y = jnp.concatenate([jnp.concatenate(ys[pp], axis=0) for pp in range(npb)], axis=1)

    mu = _mm_exact_rhs(y, mean_bd)
    d = y - mu
    var = _mm_exact_rhs(d * d, mean_bd)
    yn = d * lax.rsqrt(var + RWKV_LN_EPS) * lng_ref[...] + lnb_ref[...]
    bonus = _mm_exact_rhs(r * k2 * rk_ref[...], ones_bd) * v
    o_ref[...] = ((yn + bonus) * g).astype(o_ref.dtype)


def rwkv_time_mix(z, mu_p, w0, w2, a0, a2, g2p, k_k, k_a, r_k, lnx_g, lnx_b, B, T, RW):
    L = RWKV_CHUNK
    R = L * RWKV_BLOCK_CHUNKS
    W = LANES
    WB = W * RWKV_BLOCK_PAIRS
    ngrp = RW // WB
    c_gl = 3 * RW // GATE_LORA_PAD
    c_wl = (3 * RW + GATE_LORA_PAD) // W
    c_al = c_wl + 1

    def zcol(base):
        return pl.BlockSpec((None, R, WB), lambda b, p, c: (b, c, base + p))

    def zfix(idx, width):
        return pl.BlockSpec((None, R, width), lambda b, p, c: (b, c, idx))

    def mucol(base):
        return pl.BlockSpec((1, WB), lambda b, p, c: (0, base + p))

    def mufix(idx, width):
        return pl.BlockSpec((1, width), lambda b, p, c: (0, idx))

    pvec = pl.BlockSpec((1, WB), lambda b, p, c: (0, p))

    def lora(rank):
        return pl.BlockSpec((rank, WB), lambda b, p, c: (0, p))

    row = lambda a: a.reshape(1, -1)
    return pl.pallas_call(
        functools.partial(_rwkv_kernel, L=L),
        grid=(B, ngrp, T // R),
        in_specs=[zcol(0), zcol(ngrp), zcol(2 * ngrp),
                  zfix(c_gl, GATE_LORA_PAD), zfix(c_wl, W), zfix(c_al, W),
                  mucol(0), mucol(ngrp), mucol(2 * ngrp),
                  mufix(c_gl, GATE_LORA_PAD), mufix(c_wl, W), mufix(c_al, W),
                  pvec, pvec, pvec, pvec, pvec, pvec, pvec,
                  lora(DECAY_LORA), lora(ICLR_LORA), lora(GATE_LORA_PAD)],
        out_specs=pl.BlockSpec((None, R, WB), lambda b, p, c: (b, c, p)),
        out_shape=jax.ShapeDtypeStruct((B, T, RW), BF16),
        scratch_shapes=[pltpu.VMEM((RWKV_BLOCK_PAIRS, W, W), F32)] + [pltpu.VMEM((8, WB), F32)] * 3
        + [pltpu.VMEM((8, GATE_LORA_PAD), F32), pltpu.VMEM((8, W), F32), pltpu.VMEM((8, W), F32)],
        compiler_params=_cparams(("parallel", "parallel", "arbitrary")),
        name="rwkv7",
    )(z, z, z, z, z, z, mu_p, mu_p, mu_p, mu_p, mu_p, mu_p,
      row(w0), row(a0), row(k_k), row(k_a), row(r_k), row(lnx_g), row(lnx_b), w2, a2, g2p)


def _pad_cols(w, n):
    return jnp.pad(w, ((0, 0), (0, n - w.shape[1])))


def kernel(x, c, ada_w, ada_b, ada_table, ln_g, ln_b, mix_w_in, mix_mu, rwkv_w0, rwkv_w2, rwkv_a0, rwkv_a2,
           rwkv_g2, rwkv_kk, rwkv_ka, rwkv_rk, rwkv_lnx_g, rwkv_lnx_b, mix_w_out, ffn_w_gate, ffn_w_up,
           ffn_w_down, pool_w, pool_scale, moe_router_w, moe_router_b, moe_w_gate, moe_w_up, moe_w_down):
    B, T, D = x.shape
    M = B * T
    depth = ada_table.shape[0]
    RW = rwkv_w0.shape[1]
    MW = D - RW
    n_moba = MW // MOBA_HEAD_DIM
    gate_lora = rwkv_g2.shape[1]
    alpha = (2 * depth) ** 0.25

    mods = ada_mod(c, ada_w, ada_b, ada_table)
    xf = x.reshape(M, D)
    h = modulate(xf, mods[0, :, 1], mods[0, :, 0], T)
    lnrow = lambda a: a.reshape(1, D)

    za = 3 * RW + GATE_LORA_PAD + DECAY_LORA + ICLR_LORA
    n_in = za + 3 * MW
    n_pad = -(-n_in // 512) * 512
    o_wl = 3 * RW
    o_al = o_wl + DECAY_LORA
    o_gl = o_al + ICLR_LORA
    o_q = o_gl + gate_lora

    def permute_cols(w):
        return jnp.concatenate(
            [w[..., :3 * RW], _pad_cols(w[..., o_gl:o_q], GATE_LORA_PAD), w[..., o_wl:o_al], w[..., o_al:o_gl],
             _pad_cols(w[..., o_q:], n_pad - za)], axis=-1)

    for l in range(depth):
        i = l // 2
        m = mods[l]
        g1, sh2, sc2, g2 = m[:, 2], m[:, 3], m[:, 4], m[:, 5]
        if l + 1 < depth:
            sh_n, sc_n = mods[l + 1, :, 0], mods[l + 1, :, 1]
        else:
            sh_n, sc_n = jnp.zeros_like(sh2), jnp.zeros_like(sc2)
        if l % 2 == 0:
            w_in = permute_cols(mix_w_in[i]).astype(BF16)
            mu_p = permute_cols(jnp.pad(mix_mu[i], (0, mix_w_in.shape[2] - mix_mu.shape[1]))[None, :])
            z = matmul(h, w_in, 512, 512).reshape(B, T, n_pad)
            g2p = jnp.pad(rwkv_g2[i], ((0, GATE_LORA_PAD - gate_lora), (0, 0)))
            ya = rwkv_time_mix(z, mu_p, rwkv_w0[i], rwkv_w2[i], rwkv_a0[i], rwkv_a2[i], g2p, rwkv_kk[i],
                               rwkv_ka[i], rwkv_rk[i], rwkv_lnx_g[i], rwkv_lnx_b[i], B, T, RW)
            yb = moba_attention(z, za // LANES, n_moba, B, T)
            ycat = jnp.concatenate([ya, yb], axis=-1).reshape(M, D)
            xf, h = matmul_resid_ln(ycat, cast_pad_bf16(mix_w_out[i]), xf, g1, lnrow(ln_g[l, 0]), lnrow(ln_b[l, 0]),
                                    sc2, sh2, T, alpha, 512, 1024)
            F = ffn_w_gate.shape[2]
            fp = -(-F // 1024) * 1024
            act = swiglu_up(h, cast_pad_bf16(ffn_w_gate[i], cols_p=fp), cast_pad_bf16(ffn_w_up[i], cols_p=fp),
                            1024, 512)
            wd = cast_pad_bf16(ffn_w_down[i], rows_p=fp)
            xf, h = matmul_resid_ln(act, wd, xf, g2, lnrow(ln_g[l, 1]), lnrow(ln_b[l, 1]), sc_n, sh_n, T, alpha,
                                    512, 1024)
        else:
            xf, h = pool_mixer_ln(xf, m[:, 1], m[:, 0], pool_w[i].astype(BF16), lnrow(pool_scale[i]), g1,
                                  lnrow(ln_g[l, 0]), lnrow(ln_b[l, 0]), sc2, sh2, T, alpha)
            E, _, NE = moe_w_gate.shape[1:]
            idx, gates = moe_router(h, moe_router_w[i], moe_router_b[i])
            row_token, tile_e, n_used, pos0, pos1 = moe_group_metadata(idx, E, MOE_TILE)
            xs = moe_gather_rows(h, row_token, MOE_TILE)
            cast3 = lambda w, bc: cast_pad_bf16(w.reshape(-1, w.shape[-1]), bc=bc).reshape(w.shape)
            act = moe_up_grouped(xs, tile_e, n_used, cast3(moe_w_gate[i], NE // 2), cast3(moe_w_up[i], NE // 2),
                                 MOE_TILE, NE // 2)
            ys = moe_down_grouped(act, tile_e, n_used, cast3(moe_w_down[i], 1024), MOE_TILE)
            xf, h = moe_combine_ln(ys, pos0, pos1, gates, xf, g2, lnrow(ln_g[l, 1]), lnrow(ln_b[l, 1]), sc_n, sh_n,
                                   T, alpha)
    return xf.reshape(B, T, D)
```

```python
import functools
import math

import jax
import jax.numpy as jnp
from jax import lax
from jax.experimental import pallas as pl
from jax.experimental.pallas import tpu as pltpu

F32 = jnp.float32
BF16 = jnp.bfloat16
HI = lax.Precision.HIGHEST

RWKV_HEAD_DIM = 64
DECAY_LORA = 128
ICLR_LORA = 128
GATE_LORA_PAD = 512
RWKV_LN_EPS = 64e-5
MOBA_HEAD_DIM = 128
MOBA_BLOCK = 256
MOBA_TOPK = 3
ROPE_THETA = 10000.0
POOL_WINDOWS = (2, 4, 8, 16)
POOL_HALO = 16
MOE_TOPK = 2
N_MOD = 6
LN_EPS = 1e-5
NEG_INF = -1e30

LANES = 128
LN_ROWS = 64
MOE_TILE = 256
DMA_UNROLL = 8
RWKV_CHUNK = 64
RWKV_BLOCK_CHUNKS = 4
RWKV_BLOCK_PAIRS = 2
P_LORA_W, P_LORA, P_SCORE, P_TRI, P_MAIN, P_STATE = 1, 1, 1, 1, 1, 1
VMEM_LIMIT = 56 * 1024 * 1024
VMEM_LIMIT_DOWN = 60 * 1024 * 1024


def _cparams(sem, vmem_limit=VMEM_LIMIT):
    return pltpu.CompilerParams(dimension_semantics=sem, vmem_limit_bytes=vmem_limit)


def _dot(a, b, precision=None):
    return jnp.dot(a, b, preferred_element_type=F32, precision=precision)


def _dot_nt(a, b, precision=None):
    return lax.dot_general(a, b, (((1,), (1,)), ((), ())), preferred_element_type=F32, precision=precision)


def _dot_tn(a, b, precision=None):
    return lax.dot_general(a, b, (((0,), (0,)), ((), ())), preferred_element_type=F32, precision=precision)


def _ada_kernel(c_ref, w_ref, b_ref, tab_ref, o_ref):
    c = c_ref[...]
    sc = (c * jax.nn.sigmoid(c)).astype(BF16)
    acc = _dot(sc, w_ref[...].astype(BF16)) + b_ref[...]
    o_ref[...] = acc[None, :, :] + tab_ref[...][:, None, :]


def ada_mod(c, ada_w, ada_b, ada_table):
    B, D = c.shape
    N = ada_w.shape[1]
    depth = ada_table.shape[0]
    BP = 8
    cp = jnp.zeros((BP, D), F32).at[:B].set(c)
    tn = 1024
    out = pl.pallas_call(
        _ada_kernel,
        grid=(N // tn,),
        in_specs=[
            pl.BlockSpec((BP, D), lambda j: (0, 0)),
            pl.BlockSpec((D, tn), lambda j: (0, j)),
            pl.BlockSpec((1, tn), lambda j: (0, j)),
            pl.BlockSpec((depth, tn), lambda j: (0, j)),
        ],
        out_specs=pl.BlockSpec((depth, BP, tn), lambda j: (0, 0, j)),
        out_shape=jax.ShapeDtypeStruct((depth, BP, N), F32),
        compiler_params=_cparams(("parallel",)),
        name="ada_mod",
    )(cp, ada_w, ada_b.reshape(1, N), ada_table.reshape(depth, N))
    return out[:, :B].reshape(depth, B, N_MOD, 1, D)


def _cast_pad_kernel(x_ref, o_ref, *, rows, cols):
    br, bc = o_ref.shape
    i, j = pl.program_id(0), pl.program_id(1)
    inside = ((i + 1) * br <= rows) & ((j + 1) * bc <= cols)

    @pl.when(inside)
    def _():
        o_ref[...] = x_ref[...].astype(o_ref.dtype)

    @pl.when(jnp.logical_not(inside))
    def _():
        r = i * br + lax.broadcasted_iota(jnp.int32, (br, bc), 0)
        c = j * bc + lax.broadcasted_iota(jnp.int32, (br, bc), 1)
        o_ref[...] = jnp.where((r < rows) & (c < cols), x_ref[...], 0.0).astype(o_ref.dtype)


def cast_pad_bf16(w_stack, layer, rows_p=None, cols_p=None, br=512, bc=1024):
    _, rows, cols = w_stack.shape
    rows_p = rows_p or rows
    cols_p = cols_p or cols
    bc = min(bc, cols_p)
    last_r = (rows - 1) // br
    last_c = (cols - 1) // bc
    return pl.pallas_call(
        functools.partial(_cast_pad_kernel, rows=rows, cols=cols),
        grid=(rows_p // br, cols_p // bc),
        in_specs=[pl.BlockSpec((None, br, bc),
                               lambda i, j: (layer, jnp.minimum(i, last_r), jnp.minimum(j, last_c)))],
        out_specs=pl.BlockSpec((br, bc), lambda i, j: (i, j)),
        out_shape=jax.ShapeDtypeStruct((rows_p, cols_p), BF16),
        compiler_params=_cparams(("parallel", "parallel")),
        name="cast_pad",
    )(w_stack)


def _modulate_kernel(x_ref, sc_ref, sh_ref, o_ref):
    o_ref[...] = (x_ref[...] * (1.0 + sc_ref[...]) + sh_ref[...]).astype(o_ref.dtype)


def modulate(x, sc, sh, T):
    M, D = x.shape
    tm = 512
    tpb = T // tm
    vec = pl.BlockSpec((None, 1, D), lambda i: (i // tpb, 0, 0))
    return pl.pallas_call(
        _modulate_kernel,
        grid=(M // tm,),
        in_specs=[pl.BlockSpec((tm, D), lambda i: (i, 0)), vec, vec],
        out_specs=pl.BlockSpec((tm, D), lambda i: (i, 0)),
        out_shape=jax.ShapeDtypeStruct((M, D), BF16),
        compiler_params=_cparams(("parallel",)),
        name="modulate",
    )(x, sc, sh)


def _mm_kernel(a_ref, w_ref, o_ref):
    o_ref[...] = _dot(a_ref[...], w_ref[...]).astype(o_ref.dtype)


def matmul(a, w, tm, tn, out_dtype=F32):
    M, K = a.shape
    N = w.shape[1]
    return pl.pallas_call(
        _mm_kernel,
        grid=(N // tn, M // tm),
        in_specs=[pl.BlockSpec((tm, K), lambda j, i: (i, 0)),
                  pl.BlockSpec((K, tn), lambda j, i: (0, j))],
        out_specs=pl.BlockSpec((tm, tn), lambda j, i: (i, j)),
        out_shape=jax.ShapeDtypeStruct((M, N), out_dtype),
        compiler_params=_cparams(("parallel", "parallel")),
        name="matmul",
    )(a, w)


def _resid_ln_mod(x, y, gate, lng, lnb, sc, sh, alpha):
    r = alpha * x + gate * y
    mu = jnp.mean(r, axis=-1, keepdims=True)
    d = r - mu
    var = jnp.mean(d * d, axis=-1, keepdims=True)
    xn = d * lax.rsqrt(var + LN_EPS) * lng + lnb
    return xn, xn * (1.0 + sc) + sh


def _mm_ln_kernel(a_ref, w_ref, x_ref, gate_ref, lng_ref, lnb_ref, sc_ref, sh_ref,
                  xo_ref, ho_ref, *, alpha):
    k = pl.program_id(1)

    @pl.when(k == 0)
    def _():
        xo_ref[...] = _dot(a_ref[...], w_ref[...])

    @pl.when(k > 0)
    def _():
        xo_ref[...] += _dot(a_ref[...], w_ref[...])

    @pl.when(k == pl.num_programs(1) - 1)
    def _():
        for r0 in range(0, xo_ref.shape[0], LN_ROWS):
            rows = pl.ds(r0, LN_ROWS)
            xn, h = _resid_ln_mod(x_ref[rows, :], xo_ref[rows, :], gate_ref[...], lng_ref[...], lnb_ref[...],
                                  sc_ref[...], sh_ref[...], alpha)
            xo_ref[rows, :] = xn
            ho_ref[rows, :] = h.astype(ho_ref.dtype)


def matmul_resid_ln(a, w, x, gate, lng, lnb, sc, sh, T, alpha, tm, tk):
    M, K = a.shape
    D = w.shape[1]
    tpb = T // tm
    vec_b = pl.BlockSpec((None, 1, D), lambda i, k: (i // tpb, 0, 0))
    vec = pl.BlockSpec((1, D), lambda i, k: (0, 0))
    row = pl.BlockSpec((tm, D), lambda i, k: (i, 0))
    return pl.pallas_call(
        functools.partial(_mm_ln_kernel, alpha=alpha),
        grid=(M // tm, K // tk),
        in_specs=[pl.BlockSpec((tm, tk), lambda i, k: (i, k)),
                  pl.BlockSpec((tk, D), lambda i, k: (k, 0)),
                  pl.BlockSpec((tm, D), lambda i, k: (i, 0), pipeline_mode=pl.Buffered(1)),
                  vec_b, vec, vec, vec_b, vec_b],
        out_specs=[row, row],
        out_shape=[jax.ShapeDtypeStruct((M, D), F32), jax.ShapeDtypeStruct((M, D), BF16)],
        compiler_params=_cparams(("parallel", "arbitrary"), VMEM_LIMIT_DOWN),
        name="matmul_resid_ln",
    )(a, w, x, gate, lng, lnb, sc, sh)


def _swiglu_up_kernel(a_ref, wg_ref, wu_ref, o_ref):
    a = a_ref[...]
    g = _dot(a, wg_ref[...])
    u = _dot(a, wu_ref[...])
    o_ref[...] = (g * jax.nn.sigmoid(g) * u).astype(o_ref.dtype)


def swiglu_up(a, wg, wu, tm, tn):
    M, K = a.shape
    N = wg.shape[1]
    wspec = pl.BlockSpec((K, tn), lambda i, j: (0, j))
    return pl.pallas_call(
        _swiglu_up_kernel,
        grid=(M // tm, N // tn),
        in_specs=[pl.BlockSpec((tm, K), lambda i, j: (i, 0)), wspec, wspec],
        out_specs=pl.BlockSpec((tm, tn), lambda i, j: (i, j)),
        out_shape=jax.ShapeDtypeStruct((M, N), BF16),
        compiler_params=_cparams(("parallel", "parallel")),
        name="swiglu_up",
    )(a, wg, wu)


def _router_kernel(h_ref, rw_ref, rb_ref, idx_ref, gate_ref, *, n_experts):
    logits = _dot(h_ref[...], rw_ref[...], HI) + rb_ref[...]
    lane = lax.broadcasted_iota(jnp.int32, logits.shape, 1)
    logits = jnp.where(lane < n_experts, logits, -jnp.inf)
    m1 = jnp.max(logits, axis=-1, keepdims=True)
    i1 = jnp.min(jnp.where(logits == m1, lane, LANES), axis=-1, keepdims=True)
    oh1 = lane == i1
    rest = jnp.where(oh1, -jnp.inf, logits)
    m2 = jnp.max(rest, axis=-1, keepdims=True)
    i2 = jnp.min(jnp.where(rest == m2, lane, LANES), axis=-1, keepdims=True)
    e = jnp.exp(m2 - m1)
    g1 = 1.0 / (1.0 + e)
    g2 = e / (1.0 + e)
    idx_ref[...] = jnp.where(lane == 0, i1, jnp.where(lane == 1, i2, 0))
    gate_ref[...] = jnp.where(lane == 0, g1, jnp.where(lane == 1, g2, 0.0))


def moe_router(h, router_w, router_b):
    M, D = h.shape
    E = router_w.shape[1]
    tm = 512
    rw = jnp.zeros((D, LANES), F32).at[:, :E].set(router_w)
    rb = jnp.zeros((1, LANES), F32).at[0, :E].set(router_b)
    out = pl.BlockSpec((tm, LANES), lambda i: (i, 0))
    return pl.pallas_call(
        functools.partial(_router_kernel, n_experts=E),
        grid=(M // tm,),
        in_specs=[pl.BlockSpec((tm, D), lambda i: (i, 0)),
                  pl.BlockSpec((D, LANES), lambda i: (0, 0)),
                  pl.BlockSpec((1, LANES), lambda i: (0, 0))],
        out_specs=[out, out],
        out_shape=[jax.ShapeDtypeStruct((M, LANES), jnp.int32), jax.ShapeDtypeStruct((M, LANES), F32)],
        compiler_params=_cparams(("parallel",)),
        name="moe_router",
    )(h, rw, rb)


def _gather_rows_kernel(tok_ref, h_hbm, o_hbm, sem, *, tm):
    i = pl.program_id(0)

    def start(r0, c):
        for u in range(DMA_UNROLL):
            r = i * tm + r0 * DMA_UNROLL + u
            pltpu.make_async_copy(h_hbm.at[pl.ds(tok_ref[r], 1), :], o_hbm.at[pl.ds(r, 1), :], sem.at[0]).start()
        return c

    def wait_tile(t):
        pltpu.make_async_copy(h_hbm.at[pl.ds(0, tm), :], o_hbm.at[pl.ds(t * tm, tm), :], sem.at[0]).wait()

    lax.fori_loop(0, tm // DMA_UNROLL, start, 0)

    @pl.when(i > 0)
    def _():
        wait_tile(i - 1)

    @pl.when(i == pl.num_programs(0) - 1)
    def _():
        wait_tile(i)


def moe_gather_rows(h, row_token, tm):
    D = h.shape[1]
    NR = row_token.shape[0]
    return pl.pallas_call(
        functools.partial(_gather_rows_kernel, tm=tm),
        grid_spec=pltpu.PrefetchScalarGridSpec(
            num_scalar_prefetch=1,
            grid=(NR // tm,),
            in_specs=[pl.BlockSpec(memory_space=pl.ANY)],
            out_specs=pl.BlockSpec(memory_space=pl.ANY),
            scratch_shapes=[pltpu.SemaphoreType.DMA((1,))]),
        out_shape=jax.ShapeDtypeStruct((NR, D), h.dtype),
        compiler_params=_cparams(("arbitrary",)),
        name="moe_gather",
    )(row_token, h)


def _moe_up_kernel(te_ref, nu_ref, a_ref, wg_ref, wu_ref, o_ref):
    i = pl.program_id(1)

    @pl.when(i < nu_ref[0])
    def _():
        a = a_ref[...].astype(BF16)
        g = _dot(a, wg_ref[...])
        u = _dot(a, wu_ref[...])
        o_ref[...] = (g * jax.nn.sigmoid(g) * u).astype(o_ref.dtype)

    @pl.when(i >= nu_ref[0])
    def _():
        o_ref[...] = jnp.zeros_like(o_ref)


def moe_up_grouped(xs, tile_e, n_used, wg, wu, tm, tn):
    NR, K = xs.shape
    E, _, NE = wg.shape
    wspec = pl.BlockSpec((None, K, tn), lambda j, i, te, nu: (te[i], 0, j))
    return pl.pallas_call(
        _moe_up_kernel,
        grid_spec=pltpu.PrefetchScalarGridSpec(
            num_scalar_prefetch=2,
            grid=(NE // tn, NR // tm),
            in_specs=[pl.BlockSpec((tm, K), lambda j, i, te, nu: (i, 0)), wspec, wspec],
            out_specs=pl.BlockSpec((tm, tn), lambda j, i, te, nu: (i, j))),
        out_shape=jax.ShapeDtypeStruct((NR, NE), BF16),
        compiler_params=_cparams(("parallel", "parallel")),
        name="moe_up",
    )(tile_e, n_used, xs, wg, wu)


def _moe_down_kernel(te_ref, nu_ref, a_ref, w_ref, o_ref):
    i = pl.program_id(0)

    @pl.when(i < nu_ref[0])
    def _():
        o_ref[...] = _dot(a_ref[...], w_ref[...])

    @pl.when(i >= nu_ref[0])
    def _():
        o_ref[...] = jnp.zeros_like(o_ref)


def moe_down_grouped(act, tile_e, n_used, wd, tm):
    NR, NE = act.shape
    D = wd.shape[2]
    return pl.pallas_call(
        _moe_down_kernel,
        grid_spec=pltpu.PrefetchScalarGridSpec(
            num_scalar_prefetch=2,
            grid=(NR // tm,),
            in_specs=[pl.BlockSpec((tm, NE), lambda i, te, nu: (i, 0)),
                      pl.BlockSpec((None, NE, D), lambda i, te, nu: (te[i], 0, 0))],
            out_specs=pl.BlockSpec((tm, D), lambda i, te, nu: (i, 0))),
        out_shape=jax.ShapeDtypeStruct((NR, D), F32),
        compiler_params=_cparams(("parallel",)),
        name="moe_down",
    )(tile_e, n_used, act, wd)


def _moe_combine_ln_kernel(p0_ref, p1_ref, ys_hbm, x_ref, rg_ref, gate_ref, lng_ref, lnb_ref, sc_ref, sh_ref,
                           xo_ref, ho_ref, b0_ref, b1_ref, sem, *, alpha):
    i = pl.program_id(0)
    tm = x_ref.shape[0]
    slot = i % 2

    def start_tile(tile, s):
        def body(r0, c):
            for u in range(DMA_UNROLL):
                r = r0 * DMA_UNROLL + u
                t = tile * tm + r
                pltpu.make_async_copy(ys_hbm.at[pl.ds(p0_ref[t], 1), :], b0_ref.at[s, pl.ds(r, 1), :],
                                      sem.at[0, s]).start()
                pltpu.make_async_copy(ys_hbm.at[pl.ds(p1_ref[t], 1), :], b1_ref.at[s, pl.ds(r, 1), :],
                                      sem.at[1, s]).start()
            return c

        lax.fori_loop(0, tm // DMA_UNROLL, body, 0)

    @pl.when(i == 0)
    def _():
        start_tile(i, slot)

    @pl.when(i + 1 < pl.num_programs(0))
    def _():
        start_tile(i + 1, 1 - slot)

    pltpu.make_async_copy(ys_hbm.at[pl.ds(0, tm), :], b0_ref.at[slot], sem.at[0, slot]).wait()
    pltpu.make_async_copy(ys_hbm.at[pl.ds(0, tm), :], b1_ref.at[slot], sem.at[1, slot]).wait()
    for r0 in range(0, tm, LN_ROWS):
        rows = pl.ds(r0, LN_ROWS)
        rg = rg_ref[rows, :]
        y = rg[:, 0:1] * b0_ref[slot, rows, :] + rg[:, 1:2] * b1_ref[slot, rows, :]
        xn, h = _resid_ln_mod(x_ref[rows, :], y, gate_ref[...], lng_ref[...], lnb_ref[...],
                              sc_ref[...], sh_ref[...], alpha)
        xo_ref[rows, :] = xn
        ho_ref[rows, :] = h.astype(ho_ref.dtype)


def moe_combine_ln(ys, pos0, pos1, rgates, x, gate, lng, lnb, sc, sh, T, alpha):
    M, D = x.shape
    tm = 256
    tpb = T // tm
    vec_b = pl.BlockSpec((None, 1, D), lambda i, p0, p1: (i // tpb, 0, 0))
    vec = pl.BlockSpec((1, D), lambda i, p0, p1: (0, 0))
    row = pl.BlockSpec((tm, D), lambda i, p0, p1: (i, 0))
    return pl.pallas_call(
        functools.partial(_moe_combine_ln_kernel, alpha=alpha),
        grid_spec=pltpu.PrefetchScalarGridSpec(
            num_scalar_prefetch=2,
            grid=(M // tm,),
            in_specs=[pl.BlockSpec(memory_space=pl.ANY), row,
                      pl.BlockSpec((tm, LANES), lambda i, p0, p1: (i, 0)), vec_b, vec, vec, vec_b, vec_b],
            out_specs=[row, row],
            scratch_shapes=[pltpu.VMEM((2, tm, D), F32), pltpu.VMEM((2, tm, D), F32),
                            pltpu.SemaphoreType.DMA((2, 2))]),
        out_shape=[jax.ShapeDtypeStruct((M, D), F32), jax.ShapeDtypeStruct((M, D), BF16)],
        compiler_params=_cparams(("arbitrary",)),
        name="moe_combine_ln",
    )(pos0, pos1, ys, x, rgates, gate, lng, lnb, sc, sh)


def moe_group_metadata(idx, n_experts, tm):
    M = idx.shape[0]
    n_pairs = MOE_TOPK * M
    e_flat = idx[:, :MOE_TOPK].reshape(-1)
    onehot = (e_flat[:, None] == jnp.arange(n_experts)[None, :]).astype(F32)
    blk = LANES
    oh3 = onehot.reshape(n_pairs // blk, blk, n_experts)
    tri = (jnp.arange(blk)[:, None] > jnp.arange(blk)[None, :]).astype(F32)
    within = jnp.einsum("ts,bse->bte", tri, oh3, precision=HI)
    tot = oh3.sum(axis=1)
    before = jnp.cumsum(tot, axis=0) - tot
    excl = (within + before[:, None, :]).reshape(n_pairs, n_experts)
    rank = jnp.sum(excl * onehot, axis=1).astype(jnp.int32)
    counts = tot.sum(axis=0).astype(jnp.int32)
    padded = -(-counts // tm) * tm
    ends = jnp.cumsum(padded)
    starts = ends - padded
    pos = jnp.sum(onehot * starts[None, :].astype(F32), axis=1).astype(jnp.int32) + rank
    n_rows = n_pairs + n_experts * tm
    row_token = jnp.zeros((n_rows,), jnp.int32).at[pos].set(jnp.arange(n_pairs, dtype=jnp.int32) // MOE_TOPK)
    tile_start = jnp.arange(n_rows // tm, dtype=jnp.int32) * tm
    tile_e = jnp.minimum(jnp.sum(tile_start[:, None] >= ends[None, :], axis=1), n_experts - 1).astype(jnp.int32)
    n_used = (ends[-1:] // tm).astype(jnp.int32)
    pos = pos.reshape(M, MOE_TOPK)
    return row_token, tile_e, n_used, pos[:, 0], pos[:, 1]


def _pool_kernel(x_ref, xh_ref, sc_ref, sh_ref, w_ref, ps_ref, gate_ref, lng_ref, lnb_ref,
                 sc2_ref, sh2_ref, xo_ref, ho_ref, hh_ref, y_ref, *, tm, tpb, alpha):
    i = pl.program_id(0)
    D = x_ref.shape[1]
    dg = D // len(POOL_WINDOWS)
    sc = sc_ref[...]
    sh = sh_ref[...]
    x = x_ref[...]
    first = (i % tpb) == 0
    halo = xh_ref[...] * (1.0 + sc) + sh
    hh_ref[0:POOL_HALO, :] = jnp.where(first, 0.0, halo)
    hh_ref[POOL_HALO:, :] = x * (1.0 + sc) + sh
    t = (i % tpb) * tm + lax.broadcasted_iota(jnp.int32, (tm, 1), 0)
    tcnt = (t + 1).astype(F32)
    for g, win in enumerate(POOL_WINDOWS):
        cols = slice(g * dg, (g + 1) * dg)
        h = hh_ref[POOL_HALO:, cols]
        s = h
        for d in range(1, win):
            s = s + hh_ref[pl.ds(POOL_HALO - d, tm), cols]
        inv = 1.0 / jnp.minimum(tcnt, float(win))
        p = (s * inv - h).astype(BF16)
        y_ref[:, cols] = _dot(p, w_ref[g])
    y = y_ref[...] * ps_ref[...]
    xn, hn = _resid_ln_mod(x, y, gate_ref[...], lng_ref[...], lnb_ref[...], sc2_ref[...], sh2_ref[...], alpha)
    xo_ref[...] = xn
    ho_ref[...] = hn.astype(ho_ref.dtype)


def pool_mixer_ln(x, sc, sh, pool_w, pool_scale, gate, lng, lnb, sc2, sh2, T, alpha):
    M, D = x.shape
    G, dg, _ = pool_w.shape
    tm = 256
    tpb = T // tm
    hb = tm // POOL_HALO
    vec_b = pl.BlockSpec((None, 1, D), lambda i: (i // tpb, 0, 0))
    vec = pl.BlockSpec((1, D), lambda i: (0, 0))
    row = pl.BlockSpec((tm, D), lambda i: (i, 0))
    return pl.pallas_call(
        functools.partial(_pool_kernel, tm=tm, tpb=tpb, alpha=alpha),
        grid=(M // tm,),
        in_specs=[row,
                  pl.BlockSpec((POOL_HALO, D), lambda i: (jnp.maximum(i * hb - 1, 0), 0)),
                  vec_b, vec_b,
                  pl.BlockSpec((G, dg, dg), lambda i: (0, 0, 0)),
                  vec, vec_b, vec, vec, vec_b, vec_b],
        out_specs=[row, row],
        out_shape=[jax.ShapeDtypeStruct((M, D), F32), jax.ShapeDtypeStruct((M, D), F32)],
        scratch_shapes=[pltpu.VMEM((tm + POOL_HALO, D), F32), pltpu.VMEM((tm, D), F32)],
        compiler_params=_cparams(("parallel",)),
        name="pool_mixer",
    )(x, x, sc, sh, pool_w, pool_scale, gate, lng, lnb, sc2, sh2)


def _moba_kernel(q_ref, k_ref, v_ref, cos_ref, sin_ref, o_ref, *, nb):
    bs = MOBA_BLOCK
    cos = cos_ref[...]
    sin = sin_ref[...]

    def rope(t):
        return t * cos + pltpu.roll(t, MOBA_HEAD_DIM // 2, axis=1) * sin

    qr = rope(q_ref[...])
    kr = rope(k_ref[...])
    T = qr.shape[0]
    prow = lax.broadcasted_iota(jnp.int32, (LANES, T), 0)
    pcol = lax.broadcasted_iota(jnp.int32, (LANES, T), 1)
    pool = jnp.where(pcol // bs == prow, 1.0 / bs, 0.0).astype(F32)
    kmean = _dot(pool, kr, HI)
    qb = qr.astype(BF16)
    kb = kr.astype(BF16)
    vb = v_ref[...].astype(BF16)
    scale = MOBA_HEAD_DIM ** -0.5
    lane = lax.broadcasted_iota(jnp.int32, (bs, LANES), 1)
    rowi = lax.broadcasted_iota(jnp.int32, (bs, bs), 0)
    coli = lax.broadcasted_iota(jnp.int32, (bs, bs), 1)
    for i in range(nb):
        rows = slice(i * bs, (i + 1) * bs)
        qi = qb[rows]
        s_list = [jnp.where(coli <= rowi, _dot_nt(qi, kb[rows]) * scale, NEG_INF)]
        if i > 0:
            gate = _dot_nt(qr[rows], kmean, HI)
            valid = lane < i
            gate = jnp.where(valid, gate, -jnp.inf)
            rank = jnp.zeros((bs, LANES), jnp.int32)
            for j in range(i):
                gj = gate[:, j:j + 1]
                ahead = (gj > gate) | ((gj == gate) & (j < lane))
                rank = rank + ahead.astype(jnp.int32)
            sel = valid & (rank < MOBA_TOPK)
            for j in range(i):
                sj = _dot_nt(qi, kb[j * bs:(j + 1) * bs]) * scale
                s_list.append(jnp.where(sel[:, j:j + 1], sj, NEG_INF))
        smax = s_list[0]
        for s in s_list[1:]:
            smax = jnp.maximum(smax, s)
        m = smax.max(axis=-1, keepdims=True)
        psum = jnp.zeros((bs, bs), F32)
        acc = jnp.zeros((bs, MOBA_HEAD_DIM), F32)
        for jj, s in enumerate(s_list):
            j = i if jj == 0 else jj - 1
            p = jnp.exp(s - m)
            psum = psum + p
            acc = acc + _dot(p.astype(BF16), vb[j * bs:(j + 1) * bs])
        l = psum.sum(axis=-1, keepdims=True)
        o_ref[rows, :] = (acc / l).astype(o_ref.dtype)


def moba_attention(z, q_col, n_heads, B, T):
    assert T % MOBA_BLOCK == 0
    nb = T // MOBA_BLOCK
    dh = MOBA_HEAD_DIM
    inv = jnp.power(ROPE_THETA, -jnp.arange(0, dh, 2, dtype=F32) / dh)
    ang = jnp.arange(T, dtype=F32)[:, None] * inv[None, :]
    cos = jnp.concatenate([jnp.cos(ang), jnp.cos(ang)], axis=-1)
    sin = jnp.concatenate([-jnp.sin(ang), jnp.sin(ang)], axis=-1)

    def col(off):
        return pl.BlockSpec((None, T, dh), lambda b, h: (b, 0, q_col + off * n_heads + h))

    tab = pl.BlockSpec((T, dh), lambda b, h: (0, 0))
    return pl.pallas_call(
        functools.partial(_moba_kernel, nb=nb),
        grid=(B, n_heads),
        in_specs=[col(0), col(1), col(2), tab, tab],
        out_specs=pl.BlockSpec((None, T, dh), lambda b, h: (b, 0, h)),
        out_shape=jax.ShapeDtypeStruct((B, T, n_heads * dh), BF16),
        compiler_params=_cparams(("parallel", "parallel")),
        name="moba",
    )(z, z, z, cos, sin)


def _split_bf16(x, n):
    parts, rem = [], x
    for i in range(n):
        p = rem.astype(BF16)
        parts.append(p)
        if i + 1 < n:
            rem = rem - p.astype(F32)
    return parts


_DIMS = {"nn": ((1,), (0,)), "nt": ((1,), (1,)), "tn": ((0,), (0,))}


def _mm(a, b, form="nn", passes=1):
    if passes == 1:
        a3, b3 = a.astype(BF16), b.astype(BF16)
    else:
        ah, al = _split_bf16(a, 2)
        bh, bl = _split_bf16(b, 2)
        a3 = jnp.concatenate([ah, ah, al], axis=0 if form == "tn" else 1)
        b3 = jnp.concatenate([bh, bl, bh], axis=1 if form == "nt" else 0)
    return lax.dot_general(a3, b3, (_DIMS[form], ((), ())), preferred_element_type=F32)


def _mm_exact_rhs(a, b_bf16):
    a3 = jnp.concatenate(_split_bf16(a, 3), axis=1)
    b3 = jnp.concatenate([b_bf16, b_bf16, b_bf16], axis=0)
    return _dot(a3, b3)


def _rwkv_kernel(zr_ref, zk_ref, zv_ref, zgl_ref, zwl_ref, zal_ref,
                 mur_ref, muk_ref, muv_ref, mugl_ref, muwl_ref, mual_ref,
                 w0_ref, a0_ref, kk_ref, ka_ref, rk_ref, lng_ref, lnb_ref,
                 w2_ref, a2_ref, g2_ref,
                 o_ref,
                 s_ref, pr_ref, pk_ref, pv_ref, pgl_ref, pwl_ref, pal_ref, *, L):
    R = zr_ref.shape[0]
    hd = RWKV_HEAD_DIM
    c = pl.program_id(2)

    @pl.when(c == 0)
    def _():
        s_ref[...] = jnp.zeros_like(s_ref)
        for p in (pr_ref, pk_ref, pv_ref, pgl_ref, pwl_ref, pal_ref):
            p[...] = jnp.zeros_like(p)

    def shift(z_ref, p_ref, mu_ref):
        z = z_ref[...]
        row = lax.broadcasted_iota(jnp.int32, z.shape, 0)
        zp = jnp.where(row == 0, p_ref[0:1, :], pltpu.roll(z, 1, axis=0))
        p_ref[0:1, :] = z[R - 1:R, :]
        return z + (zp - z) * mu_ref[...]

    r = shift(zr_ref, pr_ref, mur_ref)
    k = shift(zk_ref, pk_ref, muk_ref)
    v = shift(zv_ref, pv_ref, muv_ref)
    gl = shift(zgl_ref, pgl_ref, mugl_ref)
    wl = shift(zwl_ref, pwl_ref, muwl_ref)
    al = shift(zal_ref, pal_ref, mual_ref)

    WB = zr_ref.shape[1]
    lane_r = lax.broadcasted_iota(jnp.int32, (WB, WB), 0)
    lane_c = lax.broadcasted_iota(jnp.int32, (WB, WB), 1)
    same_head_b = (lane_r // hd) == (lane_c // hd)
    ones_bd = jnp.where(same_head_b, 1.0, 0.0).astype(BF16)
    mean_bd = jnp.where(same_head_b, 1.0 / hd, 0.0).astype(BF16)
    same_head = same_head_b[0:LANES, 0:LANES]
    head0 = lax.broadcasted_iota(jnp.int32, (1, LANES), 1) < hd
    not_head0 = jnp.logical_not(head0)

    wpre = w0_ref[...] + _mm(jnp.tanh(wl), w2_ref[...], "nn", P_LORA_W)
    nw = -wpre
    softplus = jnp.maximum(nw, 0.0) + jnp.log(1.0 + jnp.exp(-jnp.abs(nw)))
    wlog = -softplus - 0.5
    a = jax.nn.sigmoid(a0_ref[...] + _mm(al, a2_ref[...], "nn", P_LORA))
    g = _mm(jax.nn.sigmoid(gl), g2_ref[...], "nn", P_LORA)
    kkv = k * kk_ref[...]
    kkn = kkv / jnp.maximum(jnp.sqrt(_mm_exact_rhs(kkv * kkv, ones_bd)), 1e-12)
    k2 = k * (1.0 + (a - 1.0) * ka_ref[...])
    ld = -jnp.exp(wlog)
    bv = kkn * a

    rr = lax.broadcasted_iota(jnp.int32, (R, R), 0)
    rc = lax.broadcasted_iota(jnp.int32, (R, R), 1)
    ltri = jnp.where((rr // L == rc // L) & (rr >= rc), 1.0, 0.0).astype(BF16)
    cum3 = _dot(ltri, jnp.concatenate(_split_bf16(ld, 3), axis=1))
    cum = cum3[:, 0:WB] + cum3[:, WB:2 * WB] + cum3[:, 2 * WB:3 * WB]

    S2 = 2 * L
    srow = lax.broadcasted_iota(jnp.int32, (S2, S2), 0)
    scol = lax.broadcasted_iota(jnp.int32, (S2, S2), 1)
    same_blk = (srow // L) == (scol // L)
    strict_bd = same_blk & (srow > scol)
    incl_bd = same_blk & (srow >= scol)
    tri_masks = []
    s = 1
    while s < L:
        tri_masks.append((srow // (2 * s) == scol // (2 * s)) & ((srow // s) % 2 == 1) & ((scol // s) % 2 == 0))
        s *= 2
    mid = L // 2 - 1
    zeros_l = jnp.zeros((L, LANES), F32)

    npb = WB // LANES
    nch = R // L
    streams = [(pp, j) for pp in range(npb) for j in range(nch)]
    eye = jnp.where(srow == scol, 1.0, 0.0)

    def piece(x, pp, j):
        return x[j * L:(j + 1) * L, pp * LANES:(pp + 1) * LANES]

    at_l, bt_l, kt_l, rt_l, v_l, rho_l, gam_l = [], [], [], [], [], [], []
    for pp, j in streams:
        cumj = piece(cum, pp, j)
        cmid = cumj[mid:mid + 1, :]
        cc = cumj - cmid
        e_neg = jnp.exp(-cc)
        at_l.append(-piece(kkn, pp, j) * jnp.exp(cc - piece(ld, pp, j)))
        bt_l.append(piece(bv, pp, j) * e_neg)
        kt_l.append(piece(k2, pp, j) * e_neg)
        rt_l.append(piece(r, pp, j) * jnp.exp(cc))
        v_l.append(piece(v, pp, j))
        rho_l.append(jnp.exp(cmid))
        gam_l.append(jnp.exp(cc[L - 1:L, :]))

    def stack_heads(x):
        return jnp.concatenate([jnp.where(head0, x, 0.0), jnp.where(not_head0, x, 0.0)], axis=0)

    ats_l = [stack_heads(x) for x in at_l]
    rts_l = [stack_heads(x) for x in rt_l]
    vs_l = [jnp.concatenate([x, x], axis=0) for x in v_l]
    sc_l = [_mm(jnp.concatenate([a_, r_], axis=0), jnp.concatenate([b_, b_, k_, k_], axis=0), "nt", P_SCORE)
            for a_, r_, b_, k_ in zip(ats_l, rts_l, bt_l, kt_l)]
    nab_l = [jnp.where(strict_bd, s_[0:S2, 0:S2], 0.0) for s_ in sc_l]
    nak_l = [jnp.where(strict_bd, s_[0:S2, S2:2 * S2], 0.0) for s_ in sc_l]
    mrb_l = [jnp.where(incl_bd, s_[S2:2 * S2, 0:S2], 0.0) for s_ in sc_l]
    mrk_l = [jnp.where(incl_bd, s_[S2:2 * S2, S2:2 * S2], 0.0) for s_ in sc_l]
    t_l = [eye + jnp.where(tri_masks[0], n_, 0.0) for n_ in nab_l]
    for m in tri_masks[1:]:
        ot_l = [_mm(jnp.where(m, n_, 0.0), t_, "nn", P_TRI) for n_, t_ in zip(nab_l, t_l)]
        t_l = [t_ + _mm(t_, ot_, "nn", P_TRI) for t_, ot_ in zip(t_l, ot_l)]
    nv_l = [_mm(n_, v_, "nn", P_MAIN) for n_, v_ in zip(nak_l, vs_l)]
    au_l = [_mm(t_, jnp.concatenate([a_, nv_], axis=1), "nn", P_MAIN) for t_, a_, nv_ in zip(t_l, ats_l, nv_l)]
    ry_l = [_mm(jnp.concatenate([mb_, mk_], axis=1),
                jnp.concatenate([au_, jnp.concatenate([jnp.zeros_like(v_), v_], axis=1)], axis=0), "nn", P_MAIN)
            for mb_, mk_, au_, v_ in zip(mrb_l, mrk_l, au_l, vs_l)]
    rbar_l = [rt_ + ry_[0:L, 0:LANES] + ry_[L:S2, 0:LANES] for rt_, ry_ in zip(rt_l, ry_l)]
    ybase_l = [jnp.where(head0, ry_[0:L, LANES:2 * LANES], ry_[L:S2, LANES:2 * LANES]) for ry_ in ry_l]
    gq_l = [_mm(jnp.concatenate(
                [jnp.concatenate([au_[0:L, 0:LANES] + au_[L:S2, 0:LANES],
                                  jnp.where(head0, au_[0:L, LANES:2 * LANES], au_[L:S2, LANES:2 * LANES])], axis=1),
                 jnp.concatenate([zeros_l, v_], axis=1)], axis=0),
                jnp.concatenate([b_, k_], axis=0), "tn", P_MAIN)
            for au_, v_, b_, k_ in zip(au_l, v_l, bt_l, kt_l)]
    gmat_l = [jnp.where(same_head, gq_[0:LANES], 0.0) for gq_ in gq_l]
    q0_l = [jnp.where(same_head, gq_[LANES:2 * LANES], 0.0) for gq_ in gq_l]
    st = [s_ref[pp] for pp in range(npb)]
    ys = [[None] * nch for _ in range(npb)]
    for j in range(nch):
        for pp in range(npb):
            i = pp * nch + j
            sr = st[pp] * rho_l[i]
            ys[pp][j] = _mm(rbar_l[i], sr, "nt", P_STATE) + ybase_l[i]
            st[pp] = (sr + _mm(sr, gmat_l[i], "nn", P_STATE) + q0_l[i]) * gam_l[i]
    for pp in range(npb):
        s_ref[pp] = st[pp]
    y = jnp.concatenate([jnp.concatenate(ys[pp], axis=0) for pp in range(npb)], axis=1)

    mu = _mm_exact_rhs(y, mean_bd)
    d = y - mu
    var = _mm_exact_rhs(d * d, mean_bd)
    yn = d * lax.rsqrt(var + RWKV_LN_EPS) * lng_ref[...] + lnb_ref[...]
    bonus = _mm_exact_rhs(r * k2 * rk_ref[...], ones_bd) * v
    o_ref[...] = ((yn + bonus) * g).astype(o_ref.dtype)


def rwkv_time_mix(z, mu_p, w0, w2, a0, a2, g2p, k_k, k_a, r_k, lnx_g, lnx_b, B, T, RW):
    L = RWKV_CHUNK
    R = L * RWKV_BLOCK_CHUNKS
    W = LANES
    WB = W * RWKV_BLOCK_PAIRS
    ngrp = RW // WB
    c_gl = 3 * RW // GATE_LORA_PAD
    c_wl = (3 * RW + GATE_LORA_PAD) // W
    c_al = c_wl + 1

    def zcol(base):
        return pl.BlockSpec((None, R, WB), lambda b, p, c: (b, c, base + p))

    def zfix(idx, width):
        return pl.BlockSpec((None, R, width), lambda b, p, c: (b, c, idx))

    def mucol(base):
        return pl.BlockSpec((1, WB), lambda b, p, c: (0, base + p))

    def mufix(idx, width):
        return pl.BlockSpec((1, width), lambda b, p, c: (0, idx))

    pvec = pl.BlockSpec((1, WB), lambda b, p, c: (0, p))

    def lora(rank):
        return pl.BlockSpec((rank, WB), lambda b, p, c: (0, p))

    row = lambda a: a.reshape(1, -1)
    return pl.pallas_call(
        functools.partial(_rwkv_kernel, L=L),
        grid=(B, ngrp, T // R),
        in_specs=[zcol(0), zcol(ngrp), zcol(2 * ngrp),
                  zfix(c_gl, GATE_LORA_PAD), zfix(c_wl, W), zfix(c_al, W),
                  mucol(0), mucol(ngrp), mucol(2 * ngrp),
                  mufix(c_gl, GATE_LORA_PAD), mufix(c_wl, W), mufix(c_al, W),
                  pvec, pvec, pvec, pvec, pvec, pvec, pvec,
                  lora(DECAY_LORA), lora(ICLR_LORA), lora(GATE_LORA_PAD)],
        out_specs=pl.BlockSpec((None, R, WB), lambda b, p, c: (b, c, p)),
        out_shape=jax.ShapeDtypeStruct((B, T, RW), BF16),
        scratch_shapes=[pltpu.VMEM((RWKV_BLOCK_PAIRS, W, W), F32)] + [pltpu.VMEM((8, WB), F32)] * 3
        + [pltpu.VMEM((8, GATE_LORA_PAD), F32), pltpu.VMEM((8, W), F32), pltpu.VMEM((8, W), F32)],
        compiler_params=_cparams(("parallel", "parallel", "arbitrary")),
        name="rwkv7",
    )(z, z, z, z, z, z, mu_p, mu_p, mu_p, mu_p, mu_p, mu_p,
      row(w0), row(a0), row(k_k), row(k_a), row(r_k), row(lnx_g), row(lnx_b), w2, a2, g2p)


def _pad_cols(w, n):
    return jnp.pad(w, ((0, 0), (0, n - w.shape[1])))


def kernel(x, c, ada_w, ada_b, ada_table, ln_g, ln_b, mix_w_in, mix_mu, rwkv_w0, rwkv_w2, rwkv_a0, rwkv_a2,
           rwkv_g2, rwkv_kk, rwkv_ka, rwkv_rk, rwkv_lnx_g, rwkv_lnx_b, mix_w_out, ffn_w_gate, ffn_w_up,
           ffn_w_down, pool_w, pool_scale, moe_router_w, moe_router_b, moe_w_gate, moe_w_up, moe_w_down):
    B, T, D = x.shape
    M = B * T
    depth = ada_table.shape[0]
    RW = rwkv_w0.shape[1]
    MW = D - RW
    n_moba = MW // MOBA_HEAD_DIM
    gate_lora = rwkv_g2.shape[1]
    alpha = (2 * depth) ** 0.25

    mods = ada_mod(c, ada_w, ada_b, ada_table)
    xf = x.reshape(M, D)
    h = modulate(xf, mods[0, :, 1], mods[0, :, 0], T)
    lnrow = lambda a: a.reshape(1, D)

    za = 3 * RW + GATE_LORA_PAD + DECAY_LORA + ICLR_LORA
    n_in = za + 3 * MW
    n_pad = -(-n_in // 512) * 512
    o_wl = 3 * RW
    o_al = o_wl + DECAY_LORA
    o_gl = o_al + ICLR_LORA
    o_q = o_gl + gate_lora

    def permute_cols(w):
        return jnp.concatenate(
            [w[..., :3 * RW], _pad_cols(w[..., o_gl:o_q], GATE_LORA_PAD), w[..., o_wl:o_al], w[..., o_al:o_gl],
             _pad_cols(w[..., o_q:], n_pad - za)], axis=-1)

    for l in range(depth):
        i = l // 2
        m = mods[l]
        g1, sh2, sc2, g2 = m[:, 2], m[:, 3], m[:, 4], m[:, 5]
        if l + 1 < depth:
            sh_n, sc_n = mods[l + 1, :, 0], mods[l + 1, :, 1]
        else:
            sh_n, sc_n = jnp.zeros_like(sh2), jnp.zeros_like(sc2)
        if l % 2 == 0:
            w_in = permute_cols(mix_w_in[i]).astype(BF16)
            mu_p = permute_cols(jnp.pad(mix_mu[i], (0, mix_w_in.shape[2] - mix_mu.shape[1]))[None, :])
            z = matmul(h, w_in, 512, 512).reshape(B, T, n_pad)
            g2p = jnp.pad(rwkv_g2[i], ((0, GATE_LORA_PAD - gate_lora), (0, 0)))
            ya = rwkv_time_mix(z, mu_p, rwkv_w0[i], rwkv_w2[i], rwkv_a0[i], rwkv_a2[i], g2p, rwkv_kk[i],
                               rwkv_ka[i], rwkv_rk[i], rwkv_lnx_g[i], rwkv_lnx_b[i], B, T, RW)
            yb = moba_attention(z, za // LANES, n_moba, B, T)
            ycat = jnp.concatenate([ya, yb], axis=-1).reshape(M, D)
            xf, h = matmul_resid_ln(ycat, cast_pad_bf16(mix_w_out, i), xf, g1, lnrow(ln_g[l, 0]), lnrow(ln_b[l, 0]),
                                    sc2, sh2, T, alpha, 512, 1024)
            F = ffn_w_gate.shape[2]
            fp = -(-F // 1024) * 1024
            act = swiglu_up(h, cast_pad_bf16(ffn_w_gate, i, cols_p=fp), cast_pad_bf16(ffn_w_up, i, cols_p=fp),
                            1024, 512)
            wd = cast_pad_bf16(ffn_w_down, i, rows_p=fp)
            xf, h = matmul_resid_ln(act, wd, xf, g2, lnrow(ln_g[l, 1]), lnrow(ln_b[l, 1]), sc_n, sh_n, T, alpha,
                                    512, 1024)
        else:
            xf, h = pool_mixer_ln(xf, m[:, 1], m[:, 0], pool_w[i].astype(BF16), lnrow(pool_scale[i]), g1,
                                  lnrow(ln_g[l, 0]), lnrow(ln_b[l, 0]), sc2, sh2, T, alpha)
            E, _, NE = moe_w_gate.shape[1:]
            idx, gates = moe_router(h, moe_router_w[i], moe_router_b[i])
            row_token, tile_e, n_used, pos0, pos1 = moe_group_metadata(idx, E, MOE_TILE)
            xs = moe_gather_rows(h, row_token, MOE_TILE)
            cast3 = lambda w, bc: cast_pad_bf16(w.reshape(w.shape[0], -1, w.shape[-1]), i, bc=bc).reshape(w.shape[1:])
            act = moe_up_grouped(xs, tile_e, n_used, cast3(moe_w_gate, NE // 2), cast3(moe_w_up, NE // 2),
                                 MOE_TILE, NE // 2)
            ys = moe_down_grouped(act, tile_e, n_used, cast3(moe_w_down, 1024), MOE_TILE)
            xf, h = moe_combine_ln(ys, pos0, pos1, gates, xf, g2, lnrow(ln_g[l, 1]), lnrow(ln_b[l, 1]), sc_n, sh_n,
                                   T, alpha)
    return xf.reshape(B, T, D)
```

```python
import functools
import math

import jax
import jax.numpy as jnp
from jax import lax
from jax.experimental import pallas as pl
from jax.experimental.pallas import tpu as pltpu

F32 = jnp.float32
BF16 = jnp.bfloat16
HI = lax.Precision.HIGHEST

RWKV_HEAD_DIM = 64
DECAY_LORA = 128
ICLR_LORA = 128
GATE_LORA_PAD = 512
RWKV_LN_EPS = 64e-5
MOBA_HEAD_DIM = 128
MOBA_BLOCK = 256
MOBA_TOPK = 3
ROPE_THETA = 10000.0
POOL_WINDOWS = (2, 4, 8, 16)
POOL_HALO = 16
MOE_TOPK = 2
N_MOD = 6
LN_EPS = 1e-5
NEG_INF = -1e30

LANES = 128
LN_ROWS = 64
MOE_TILE = 256
DMA_UNROLL = 8
RWKV_CHUNK = 64
RWKV_BLOCK_CHUNKS = 4
RWKV_BLOCK_PAIRS = 2
P_LORA_W, P_LORA, P_SCORE, P_TRI, P_MAIN, P_STATE = 1, 1, 1, 1, 1, 1
VMEM_LIMIT = 56 * 1024 * 1024
VMEM_LIMIT_DOWN = 60 * 1024 * 1024


def _cparams(sem, vmem_limit=VMEM_LIMIT):
    return pltpu.CompilerParams(dimension_semantics=sem, vmem_limit_bytes=vmem_limit)


def _dot(a, b, precision=None):
    return jnp.dot(a, b, preferred_element_type=F32, precision=precision)


def _dot_nt(a, b, precision=None):
    return lax.dot_general(a, b, (((1,), (1,)), ((), ())), preferred_element_type=F32, precision=precision)


def _dot_tn(a, b, precision=None):
    return lax.dot_general(a, b, (((0,), (0,)), ((), ())), preferred_element_type=F32, precision=precision)


def _ada_kernel(c_ref, w_ref, b_ref, tab_ref, o_ref):
    c = c_ref[...]
    sc = (c * jax.nn.sigmoid(c)).astype(BF16)
    acc = _dot(sc, w_ref[...].astype(BF16)) + b_ref[...]
    o_ref[...] = acc[None, :, :] + tab_ref[...][:, None, :]


def ada_mod(c, ada_w, ada_b, ada_table):
    B, D = c.shape
    N = ada_w.shape[1]
    depth = ada_table.shape[0]
    BP = 8
    cp = jnp.zeros((BP, D), F32).at[:B].set(c)
    tn = 1024
    out = pl.pallas_call(
        _ada_kernel,
        grid=(N // tn,),
        in_specs=[
            pl.BlockSpec((BP, D), lambda j: (0, 0)),
            pl.BlockSpec((D, tn), lambda j: (0, j)),
            pl.BlockSpec((1, tn), lambda j: (0, j)),
            pl.BlockSpec((depth, tn), lambda j: (0, j)),
        ],
        out_specs=pl.BlockSpec((depth, BP, tn), lambda j: (0, 0, j)),
        out_shape=jax.ShapeDtypeStruct((depth, BP, N), F32),
        compiler_params=_cparams(("parallel",)),
        name="ada_mod",
    )(cp, ada_w, ada_b.reshape(1, N), ada_table.reshape(depth, N))
    return out[:, :B].reshape(depth, B, N_MOD, 1, D)


def _cast_pad_kernel(x_ref, o_ref, *, rows, cols):
    br, bc = o_ref.shape
    i, j = pl.program_id(0), pl.program_id(1)
    inside = ((i + 1) * br <= rows) & ((j + 1) * bc <= cols)

    @pl.when(inside)
    def _():
        o_ref[...] = x_ref[...].astype(o_ref.dtype)

    @pl.when(jnp.logical_not(inside))
    def _():
        r = i * br + lax.broadcasted_iota(jnp.int32, (br, bc), 0)
        c = j * bc + lax.broadcasted_iota(jnp.int32, (br, bc), 1)
        o_ref[...] = jnp.where((r < rows) & (c < cols), x_ref[...], 0.0).astype(o_ref.dtype)


def cast_pad_bf16(w_stack, layer, rows_p=None, cols_p=None, br=1024, bc=1024):
    _, rows, cols = w_stack.shape
    rows_p = rows_p or rows
    cols_p = cols_p or cols
    bc = min(bc, cols_p)
    last_r = (rows - 1) // br
    last_c = (cols - 1) // bc
    return pl.pallas_call(
        functools.partial(_cast_pad_kernel, rows=rows, cols=cols),
        grid=(rows_p // br, cols_p // bc),
        in_specs=[pl.BlockSpec((None, br, bc),
                               lambda i, j: (layer, jnp.minimum(i, last_r), jnp.minimum(j, last_c)))],
        out_specs=pl.BlockSpec((br, bc), lambda i, j: (i, j)),
        out_shape=jax.ShapeDtypeStruct((rows_p, cols_p), BF16),
        compiler_params=_cparams(("parallel", "parallel")),
        name="cast_pad",
    )(w_stack)


def _modulate_kernel(x_ref, sc_ref, sh_ref, o_ref):
    o_ref[...] = (x_ref[...] * (1.0 + sc_ref[...]) + sh_ref[...]).astype(o_ref.dtype)


def modulate(x, sc, sh, T):
    M, D = x.shape
    tm = 512
    tpb = T // tm
    vec = pl.BlockSpec((None, 1, D), lambda i: (i // tpb, 0, 0))
    return pl.pallas_call(
        _modulate_kernel,
        grid=(M // tm,),
        in_specs=[pl.BlockSpec((tm, D), lambda i: (i, 0)), vec, vec],
        out_specs=pl.BlockSpec((tm, D), lambda i: (i, 0)),
        out_shape=jax.ShapeDtypeStruct((M, D), BF16),
        compiler_params=_cparams(("parallel",)),
        name="modulate",
    )(x, sc, sh)


def _mm_kernel(a_ref, w_ref, o_ref):
    o_ref[...] = _dot(a_ref[...], w_ref[...]).astype(o_ref.dtype)


def matmul(a, w, tm, tn, out_dtype=F32):
    M, K = a.shape
    N = w.shape[1]
    return pl.pallas_call(
        _mm_kernel,
        grid=(N // tn, M // tm),
        in_specs=[pl.BlockSpec((tm, K), lambda j, i: (i, 0)),
                  pl.BlockSpec((K, tn), lambda j, i: (0, j))],
        out_specs=pl.BlockSpec((tm, tn), lambda j, i: (i, j)),
        out_shape=jax.ShapeDtypeStruct((M, N), out_dtype),
        compiler_params=_cparams(("parallel", "parallel")),
        name="matmul",
    )(a, w)


def _resid_ln_mod(x, y, gate, lng, lnb, sc, sh, alpha):
    r = alpha * x + gate * y
    mu = jnp.mean(r, axis=-1, keepdims=True)
    d = r - mu
    var = jnp.mean(d * d, axis=-1, keepdims=True)
    xn = d * lax.rsqrt(var + LN_EPS) * lng + lnb
    return xn, xn * (1.0 + sc) + sh


def _mm_ln_kernel(a_ref, w_ref, x_ref, gate_ref, lng_ref, lnb_ref, sc_ref, sh_ref,
                  xo_ref, ho_ref, *, alpha):
    k = pl.program_id(1)

    @pl.when(k == 0)
    def _():
        xo_ref[...] = _dot(a_ref[...], w_ref[...])

    @pl.when(k > 0)
    def _():
        xo_ref[...] += _dot(a_ref[...], w_ref[...])

    @pl.when(k == pl.num_programs(1) - 1)
    def _():
        for r0 in range(0, xo_ref.shape[0], LN_ROWS):
            rows = pl.ds(r0, LN_ROWS)
            xn, h = _resid_ln_mod(x_ref[rows, :], xo_ref[rows, :], gate_ref[...], lng_ref[...], lnb_ref[...],
                                  sc_ref[...], sh_ref[...], alpha)
            xo_ref[rows, :] = xn
            ho_ref[rows, :] = h.astype(ho_ref.dtype)


def matmul_resid_ln(a, w, x, gate, lng, lnb, sc, sh, T, alpha, tm, tk):
    M, K = a.shape
    D = w.shape[1]
    tpb = T // tm
    vec_b = pl.BlockSpec((None, 1, D), lambda i, k: (i // tpb, 0, 0))
    vec = pl.BlockSpec((1, D), lambda i, k: (0, 0))
    row = pl.BlockSpec((tm, D), lambda i, k: (i, 0))
    return pl.pallas_call(
        functools.partial(_mm_ln_kernel, alpha=alpha),
        grid=(M // tm, K // tk),
        in_specs=[pl.BlockSpec((tm, tk), lambda i, k: (i, k)),
                  pl.BlockSpec((tk, D), lambda i, k: (k, 0)),
                  pl.BlockSpec((tm, D), lambda i, k: (i, 0), pipeline_mode=pl.Buffered(1)),
                  vec_b, vec, vec, vec_b, vec_b],
        out_specs=[row, row],
        out_shape=[jax.ShapeDtypeStruct((M, D), F32), jax.ShapeDtypeStruct((M, D), BF16)],
        compiler_params=_cparams(("parallel", "arbitrary"), VMEM_LIMIT_DOWN),
        name="matmul_resid_ln",
    )(a, w, x, gate, lng, lnb, sc, sh)


def _swiglu_up_kernel(a_ref, wg_ref, wu_ref, o_ref):
    a = a_ref[...]
    g = _dot(a, wg_ref[...])
    u = _dot(a, wu_ref[...])
    o_ref[...] = (g * jax.nn.sigmoid(g) * u).astype(o_ref.dtype)


def swiglu_up(a, wg, wu, tm, tn):
    M, K = a.shape
    N = wg.shape[1]
    wspec = pl.BlockSpec((K, tn), lambda i, j: (0, j))
    return pl.pallas_call(
        _swiglu_up_kernel,
        grid=(M // tm, N // tn),
        in_specs=[pl.BlockSpec((tm, K), lambda i, j: (i, 0)), wspec, wspec],
        out_specs=pl.BlockSpec((tm, tn), lambda i, j: (i, j)),
        out_shape=jax.ShapeDtypeStruct((M, N), BF16),
        compiler_params=_cparams(("parallel", "parallel")),
        name="swiglu_up",
    )(a, wg, wu)


def _router_kernel(h_ref, rw_ref, rb_ref, idx_ref, gate_ref, *, n_experts):
    logits = _dot(h_ref[...], rw_ref[...], HI) + rb_ref[...]
    lane = lax.broadcasted_iota(jnp.int32, logits.shape, 1)
    logits = jnp.where(lane < n_experts, logits, -jnp.inf)
    m1 = jnp.max(logits, axis=-1, keepdims=True)
    i1 = jnp.min(jnp.where(logits == m1, lane, LANES), axis=-1, keepdims=True)
    oh1 = lane == i1
    rest = jnp.where(oh1, -jnp.inf, logits)
    m2 = jnp.max(rest, axis=-1, keepdims=True)
    i2 = jnp.min(jnp.where(rest == m2, lane, LANES), axis=-1, keepdims=True)
    e = jnp.exp(m2 - m1)
    g1 = 1.0 / (1.0 + e)
    g2 = e / (1.0 + e)
    idx_ref[...] = jnp.where(lane == 0, i1, jnp.where(lane == 1, i2, 0))
    gate_ref[...] = jnp.where(lane == 0, g1, jnp.where(lane == 1, g2, 0.0))


def moe_router(h, router_w, router_b):
    M, D = h.shape
    E = router_w.shape[1]
    tm = 512
    rw = jnp.zeros((D, LANES), F32).at[:, :E].set(router_w)
    rb = jnp.zeros((1, LANES), F32).at[0, :E].set(router_b)
    out = pl.BlockSpec((tm, LANES), lambda i: (i, 0))
    return pl.pallas_call(
        functools.partial(_router_kernel, n_experts=E),
        grid=(M // tm,),
        in_specs=[pl.BlockSpec((tm, D), lambda i: (i, 0)),
                  pl.BlockSpec((D, LANES), lambda i: (0, 0)),
                  pl.BlockSpec((1, LANES), lambda i: (0, 0))],
        out_specs=[out, out],
        out_shape=[jax.ShapeDtypeStruct((M, LANES), jnp.int32), jax.ShapeDtypeStruct((M, LANES), F32)],
        compiler_params=_cparams(("parallel",)),
        name="moe_router",
    )(h, rw, rb)


def _gather_rows_kernel(tok_ref, h_hbm, o_ref, buf_ref, sem):
    i = pl.program_id(0)
    tm = o_ref.shape[0]
    slot = i % 2

    def start_tile(tile, s):
        def body(r0, c):
            for u in range(DMA_UNROLL):
                r = r0 * DMA_UNROLL + u
                pltpu.make_async_copy(h_hbm.at[pl.ds(tok_ref[tile * tm + r], 1), :], buf_ref.at[s, pl.ds(r, 1), :],
                                      sem.at[s]).start()
            return c

        lax.fori_loop(0, tm // DMA_UNROLL, body, 0)

    @pl.when(i == 0)
    def _():
        start_tile(i, slot)

    @pl.when(i + 1 < pl.num_programs(0))
    def _():
        start_tile(i + 1, 1 - slot)

    pltpu.make_async_copy(h_hbm.at[pl.ds(0, tm), :], buf_ref.at[slot], sem.at[slot]).wait()
    o_ref[...] = buf_ref[slot].astype(o_ref.dtype)


def moe_gather_rows(h, row_token, tm):
    D = h.shape[1]
    NR = row_token.shape[0]
    return pl.pallas_call(
        _gather_rows_kernel,
        grid_spec=pltpu.PrefetchScalarGridSpec(
            num_scalar_prefetch=1,
            grid=(NR // tm,),
            in_specs=[pl.BlockSpec(memory_space=pl.ANY)],
            out_specs=pl.BlockSpec((tm, D), lambda i, tok: (i, 0)),
            scratch_shapes=[pltpu.VMEM((2, tm, D), F32), pltpu.SemaphoreType.DMA((2,))]),
        out_shape=jax.ShapeDtypeStruct((NR, D), BF16),
        compiler_params=_cparams(("arbitrary",)),
        name="moe_gather",
    )(row_token, h)


def _moe_up_kernel(te_ref, nu_ref, a_ref, wg_ref, wu_ref, o_ref):
    i = pl.program_id(1)

    @pl.when(i < nu_ref[0])
    def _():
        a = a_ref[...]
        g = _dot(a, wg_ref[...])
        u = _dot(a, wu_ref[...])
        o_ref[...] = (g * jax.nn.sigmoid(g) * u).astype(o_ref.dtype)

    @pl.when(i >= nu_ref[0])
    def _():
        o_ref[...] = jnp.zeros_like(o_ref)


def moe_up_grouped(xs, tile_e, n_used, wg, wu, tm, tn):
    NR, K = xs.shape
    E, _, NE = wg.shape
    wspec = pl.BlockSpec((None, K, tn), lambda j, i, te, nu: (te[i], 0, j))
    return pl.pallas_call(
        _moe_up_kernel,
        grid_spec=pltpu.PrefetchScalarGridSpec(
            num_scalar_prefetch=2,
            grid=(NE // tn, NR // tm),
            in_specs=[pl.BlockSpec((tm, K), lambda j, i, te, nu: (i, 0)), wspec, wspec],
            out_specs=pl.BlockSpec((tm, tn), lambda j, i, te, nu: (i, j))),
        out_shape=jax.ShapeDtypeStruct((NR, NE), BF16),
        compiler_params=_cparams(("parallel", "parallel")),
        name="moe_up",
    )(tile_e, n_used, xs, wg, wu)


def _moe_down_kernel(te_ref, nu_ref, a_ref, w_ref, o_ref):
    i = pl.program_id(0)

    @pl.when(i < nu_ref[0])
    def _():
        o_ref[...] = _dot(a_ref[...], w_ref[...])

    @pl.when(i >= nu_ref[0])
    def _():
        o_ref[...] = jnp.zeros_like(o_ref)


def moe_down_grouped(act, tile_e, n_used, wd, tm):
    NR, NE = act.shape
    D = wd.shape[2]
    return pl.pallas_call(
        _moe_down_kernel,
        grid_spec=pltpu.PrefetchScalarGridSpec(
            num_scalar_prefetch=2,
            grid=(NR // tm,),
            in_specs=[pl.BlockSpec((tm, NE), lambda i, te, nu: (i, 0)),
                      pl.BlockSpec((None, NE, D), lambda i, te, nu: (te[i], 0, 0))],
            out_specs=pl.BlockSpec((tm, D), lambda i, te, nu: (i, 0))),
        out_shape=jax.ShapeDtypeStruct((NR, D), F32),
        compiler_params=_cparams(("parallel",)),
        name="moe_down",
    )(tile_e, n_used, act, wd)


def _moe_combine_ln_kernel(p0_ref, p1_ref, ys_hbm, x_ref, rg_ref, gate_ref, lng_ref, lnb_ref, sc_ref, sh_ref,
                           xo_ref, ho_ref, b0_ref, b1_ref, sem, *, alpha):
    i = pl.program_id(0)
    tm = x_ref.shape[0]
    slot = i % 2

    def start_tile(tile, s):
        def body(r0, c):
            for u in range(DMA_UNROLL):
                r = r0 * DMA_UNROLL + u
                t = tile * tm + r
                pltpu.make_async_copy(ys_hbm.at[pl.ds(p0_ref[t], 1), :], b0_ref.at[s, pl.ds(r, 1), :],
                                      sem.at[0, s]).start()
                pltpu.make_async_copy(ys_hbm.at[pl.ds(p1_ref[t], 1), :], b1_ref.at[s, pl.ds(r, 1), :],
                                      sem.at[1, s]).start()
            return c

        lax.fori_loop(0, tm // DMA_UNROLL, body, 0)

    @pl.when(i == 0)
    def _():
        start_tile(i, slot)

    @pl.when(i + 1 < pl.num_programs(0))
    def _():
        start_tile(i + 1, 1 - slot)

    pltpu.make_async_copy(ys_hbm.at[pl.ds(0, tm), :], b0_ref.at[slot], sem.at[0, slot]).wait()
    pltpu.make_async_copy(ys_hbm.at[pl.ds(0, tm), :], b1_ref.at[slot], sem.at[1, slot]).wait()
    for r0 in range(0, tm, LN_ROWS):
        rows = pl.ds(r0, LN_ROWS)
        rg = rg_ref[rows, :]
        y = rg[:, 0:1] * b0_ref[slot, rows, :] + rg[:, 1:2] * b1_ref[slot, rows, :]
        xn, h = _resid_ln_mod(x_ref[rows, :], y, gate_ref[...], lng_ref[...], lnb_ref[...],
                              sc_ref[...], sh_ref[...], alpha)
        xo_ref[rows, :] = xn
        ho_ref[rows, :] = h.astype(ho_ref.dtype)


def moe_combine_ln(ys, pos0, pos1, rgates, x, gate, lng, lnb, sc, sh, T, alpha):
    M, D = x.shape
    tm = 256
    tpb = T // tm
    vec_b = pl.BlockSpec((None, 1, D), lambda i, p0, p1: (i // tpb, 0, 0))
    vec = pl.BlockSpec((1, D), lambda i, p0, p1: (0, 0))
    row = pl.BlockSpec((tm, D), lambda i, p0, p1: (i, 0))
    return pl.pallas_call(
        functools.partial(_moe_combine_ln_kernel, alpha=alpha),
        grid_spec=pltpu.PrefetchScalarGridSpec(
            num_scalar_prefetch=2,
            grid=(M // tm,),
            in_specs=[pl.BlockSpec(memory_space=pl.ANY), row,
                      pl.BlockSpec((tm, LANES), lambda i, p0, p1: (i, 0)), vec_b, vec, vec, vec_b, vec_b],
            out_specs=[row, row],
            scratch_shapes=[pltpu.VMEM((2, tm, D), F32), pltpu.VMEM((2, tm, D), F32),
                            pltpu.SemaphoreType.DMA((2, 2))]),
        out_shape=[jax.ShapeDtypeStruct((M, D), F32), jax.ShapeDtypeStruct((M, D), BF16)],
        compiler_params=_cparams(("arbitrary",)),
        name="moe_combine_ln",
    )(pos0, pos1, ys, x, rgates, gate, lng, lnb, sc, sh)


def moe_group_metadata(idx, n_experts, tm):
    M = idx.shape[0]
    n_pairs = MOE_TOPK * M
    e_flat = idx[:, :MOE_TOPK].reshape(-1)
    onehot = (e_flat[:, None] == jnp.arange(n_experts)[None, :]).astype(F32)
    blk = LANES
    oh3 = onehot.reshape(n_pairs // blk, blk, n_experts)
    tri = (jnp.arange(blk)[:, None] > jnp.arange(blk)[None, :]).astype(F32)
    within = jnp.einsum("ts,bse->bte", tri, oh3, precision=HI)
    tot = oh3.sum(axis=1)
    before = jnp.cumsum(tot, axis=0) - tot
    excl = (within + before[:, None, :]).reshape(n_pairs, n_experts)
    rank = jnp.sum(excl * onehot, axis=1).astype(jnp.int32)
    counts = tot.sum(axis=0).astype(jnp.int32)
    padded = -(-counts // tm) * tm
    ends = jnp.cumsum(padded)
    starts = ends - padded
    pos = jnp.sum(onehot * starts[None, :].astype(F32), axis=1).astype(jnp.int32) + rank
    n_rows = n_pairs + n_experts * tm
    row_token = jnp.zeros((n_rows,), jnp.int32).at[pos].set(jnp.arange(n_pairs, dtype=jnp.int32) // MOE_TOPK)
    tile_start = jnp.arange(n_rows // tm, dtype=jnp.int32) * tm
    tile_e = jnp.minimum(jnp.sum(tile_start[:, None] >= ends[None, :], axis=1), n_experts - 1).astype(jnp.int32)
    n_used = (ends[-1:] // tm).astype(jnp.int32)
    pos = pos.reshape(M, MOE_TOPK)
    return row_token, tile_e, n_used, pos[:, 0], pos[:, 1]


def _pool_kernel(x_ref, xh_ref, sc_ref, sh_ref, w_ref, ps_ref, gate_ref, lng_ref, lnb_ref,
                 sc2_ref, sh2_ref, xo_ref, ho_ref, hh_ref, y_ref, *, tm, tpb, alpha):
    i = pl.program_id(0)
    D = x_ref.shape[1]
    dg = D // len(POOL_WINDOWS)
    sc = sc_ref[...]
    sh = sh_ref[...]
    x = x_ref[...]
    first = (i % tpb) == 0
    halo = xh_ref[...] * (1.0 + sc) + sh
    hh_ref[0:POOL_HALO, :] = jnp.where(first, 0.0, halo)
    hh_ref[POOL_HALO:, :] = x * (1.0 + sc) + sh
    t = (i % tpb) * tm + lax.broadcasted_iota(jnp.int32, (tm, 1), 0)
    tcnt = (t + 1).astype(F32)
    for g, win in enumerate(POOL_WINDOWS):
        cols = slice(g * dg, (g + 1) * dg)
        h = hh_ref[POOL_HALO:, cols]
        s = h
        for d in range(1, win):
            s = s + hh_ref[pl.ds(POOL_HALO - d, tm), cols]
        inv = 1.0 / jnp.minimum(tcnt, float(win))
        p = (s * inv - h).astype(BF16)
        y_ref[:, cols] = _dot(p, w_ref[g])
    y = y_ref[...] * ps_ref[...]
    xn, hn = _resid_ln_mod(x, y, gate_ref[...], lng_ref[...], lnb_ref[...], sc2_ref[...], sh2_ref[...], alpha)
    xo_ref[...] = xn
    ho_ref[...] = hn.astype(ho_ref.dtype)


def pool_mixer_ln(x, sc, sh, pool_w, pool_scale, gate, lng, lnb, sc2, sh2, T, alpha):
    M, D = x.shape
    G, dg, _ = pool_w.shape
    tm = 256
    tpb = T // tm
    hb = tm // POOL_HALO
    vec_b = pl.BlockSpec((None, 1, D), lambda i: (i // tpb, 0, 0))
    vec = pl.BlockSpec((1, D), lambda i: (0, 0))
    row = pl.BlockSpec((tm, D), lambda i: (i, 0))
    return pl.pallas_call(
        functools.partial(_pool_kernel, tm=tm, tpb=tpb, alpha=alpha),
        grid=(M // tm,),
        in_specs=[row,
                  pl.BlockSpec((POOL_HALO, D), lambda i: (jnp.maximum(i * hb - 1, 0), 0)),
                  vec_b, vec_b,
                  pl.BlockSpec((G, dg, dg), lambda i: (0, 0, 0)),
                  vec, vec_b, vec, vec, vec_b, vec_b],
        out_specs=[row, row],
        out_shape=[jax.ShapeDtypeStruct((M, D), F32), jax.ShapeDtypeStruct((M, D), F32)],
        scratch_shapes=[pltpu.VMEM((tm + POOL_HALO, D), F32), pltpu.VMEM((tm, D), F32)],
        compiler_params=_cparams(("parallel",)),
        name="pool_mixer",
    )(x, x, sc, sh, pool_w, pool_scale, gate, lng, lnb, sc2, sh2)


def _moba_kernel(q_ref, k_ref, v_ref, cos_ref, sin_ref, o_ref, *, nb):
    bs = MOBA_BLOCK
    cos = cos_ref[...]
    sin = sin_ref[...]

    def rope(t):
        return t * cos + pltpu.roll(t, MOBA_HEAD_DIM // 2, axis=1) * sin

    qr = rope(q_ref[...])
    kr = rope(k_ref[...])
    T = qr.shape[0]
    prow = lax.broadcasted_iota(jnp.int32, (LANES, T), 0)
    pcol = lax.broadcasted_iota(jnp.int32, (LANES, T), 1)
    pool = jnp.where(pcol // bs == prow, 1.0 / bs, 0.0).astype(F32)
    kmean = _dot(pool, kr, HI)
    qb = qr.astype(BF16)
    kb = kr.astype(BF16)
    vb = v_ref[...].astype(BF16)
    scale = MOBA_HEAD_DIM ** -0.5
    lane = lax.broadcasted_iota(jnp.int32, (bs, LANES), 1)
    rowi = lax.broadcasted_iota(jnp.int32, (bs, bs), 0)
    coli = lax.broadcasted_iota(jnp.int32, (bs, bs), 1)
    for i in range(nb):
        rows = slice(i * bs, (i + 1) * bs)
        qi = qb[rows]
        s_list = [jnp.where(coli <= rowi, _dot_nt(qi, kb[rows]) * scale, NEG_INF)]
        if i > 0:
            gate = _dot_nt(qr[rows], kmean, HI)
            valid = lane < i
            gate = jnp.where(valid, gate, -jnp.inf)
            rank = jnp.zeros((bs, LANES), jnp.int32)
            for j in range(i):
                gj = gate[:, j:j + 1]
                ahead = (gj > gate) | ((gj == gate) & (j < lane))
                rank = rank + ahead.astype(jnp.int32)
            sel = valid & (rank < MOBA_TOPK)
            for j in range(i):
                sj = _dot_nt(qi, kb[j * bs:(j + 1) * bs]) * scale
                s_list.append(jnp.where(sel[:, j:j + 1], sj, NEG_INF))
        smax = s_list[0]
        for s in s_list[1:]:
            smax = jnp.maximum(smax, s)
        m = smax.max(axis=-1, keepdims=True)
        psum = jnp.zeros((bs, bs), F32)
        acc = jnp.zeros((bs, MOBA_HEAD_DIM), F32)
        for jj, s in enumerate(s_list):
            j = i if jj == 0 else jj - 1
            p = jnp.exp(s - m)
            psum = psum + p
            acc = acc + _dot(p.astype(BF16), vb[j * bs:(j + 1) * bs])
        l = psum.sum(axis=-1, keepdims=True)
        o_ref[rows, :] = (acc / l).astype(o_ref.dtype)


def moba_attention(z, q_col, n_heads, B, T):
    assert T % MOBA_BLOCK == 0
    nb = T // MOBA_BLOCK
    dh = MOBA_HEAD_DIM
    inv = jnp.power(ROPE_THETA, -jnp.arange(0, dh, 2, dtype=F32) / dh)
    ang = jnp.arange(T, dtype=F32)[:, None] * inv[None, :]
    cos = jnp.concatenate([jnp.cos(ang), jnp.cos(ang)], axis=-1)
    sin = jnp.concatenate([-jnp.sin(ang), jnp.sin(ang)], axis=-1)

    def col(off):
        return pl.BlockSpec((None, T, dh), lambda b, h: (b, 0, q_col + off * n_heads + h))

    tab = pl.BlockSpec((T, dh), lambda b, h: (0, 0))
    return pl.pallas_call(
        functools.partial(_moba_kernel, nb=nb),
        grid=(B, n_heads),
        in_specs=[col(0), col(1), col(2), tab, tab],
        out_specs=pl.BlockSpec((None, T, dh), lambda b, h: (b, 0, h)),
        out_shape=jax.ShapeDtypeStruct((B, T, n_heads * dh), BF16),
        compiler_params=_cparams(("parallel", "parallel")),
        name="moba",
    )(z, z, z, cos, sin)


def _split_bf16(x, n):
    parts, rem = [], x
    for i in range(n):
        p = rem.astype(BF16)
        parts.append(p)
        if i + 1 < n:
            rem = rem - p.astype(F32)
    return parts


_DIMS = {"nn": ((1,), (0,)), "nt": ((1,), (1,)), "tn": ((0,), (0,))}


def _mm(a, b, form="nn", passes=1):
    if passes == 1:
        a3, b3 = a.astype(BF16), b.astype(BF16)
    else:
        ah, al = _split_bf16(a, 2)
        bh, bl = _split_bf16(b, 2)
        a3 = jnp.concatenate([ah, ah, al], axis=0 if form == "tn" else 1)
        b3 = jnp.concatenate([bh, bl, bh], axis=1 if form == "nt" else 0)
    return lax.dot_general(a3, b3, (_DIMS[form], ((), ())), preferred_element_type=F32)


def _mm_exact_rhs(a, b_bf16):
    a3 = jnp.concatenate(_split_bf16(a, 3), axis=1)
    b3 = jnp.concatenate([b_bf16, b_bf16, b_bf16], axis=0)
    return _dot(a3, b3)


def _rwkv_kernel(zr_ref, zk_ref, zv_ref, zgl_ref, zwl_ref, zal_ref,
                 mur_ref, muk_ref, muv_ref, mugl_ref, muwl_ref, mual_ref,
                 w0_ref, a0_ref, kk_ref, ka_ref, rk_ref, lng_ref, lnb_ref,
                 w2_ref, a2_ref, g2_ref,
                 o_ref,
                 s_ref, pr_ref, pk_ref, pv_ref, pgl_ref, pwl_ref, pal_ref, *, L):
    R = zr_ref.shape[0]
    hd = RWKV_HEAD_DIM
    c = pl.program_id(2)

    @pl.when(c == 0)
    def _():
        s_ref[...] = jnp.zeros_like(s_ref)
        for p in (pr_ref, pk_ref, pv_ref, pgl_ref, pwl_ref, pal_ref):
            p[...] = jnp.zeros_like(p)

    def shift(z_ref, p_ref, mu_ref):
        z = z_ref[...]
        row = lax.broadcasted_iota(jnp.int32, z.shape, 0)
        zp = jnp.where(row == 0, p_ref[0:1, :], pltpu.roll(z, 1, axis=0))
        p_ref[0:1, :] = z[R - 1:R, :]
        return z + (zp - z) * mu_ref[...]

    r = shift(zr_ref, pr_ref, mur_ref)
    k = shift(zk_ref, pk_ref, muk_ref)
    v = shift(zv_ref, pv_ref, muv_ref)
    gl = shift(zgl_ref, pgl_ref, mugl_ref)
    wl = shift(zwl_ref, pwl_ref, muwl_ref)
    al = shift(zal_ref, pal_ref, mual_ref)

    WB = zr_ref.shape[1]
    lane_r = lax.broadcasted_iota(jnp.int32, (WB, WB), 0)
    lane_c = lax.broadcasted_iota(jnp.int32, (WB, WB), 1)
    same_head_b = (lane_r // hd) == (lane_c // hd)
    ones_bd = jnp.where(same_head_b, 1.0, 0.0).astype(BF16)
    mean_bd = jnp.where(same_head_b, 1.0 / hd, 0.0).astype(BF16)
    same_head = same_head_b[0:LANES, 0:LANES]
    head0 = lax.broadcasted_iota(jnp.int32, (1, LANES), 1) < hd
    not_head0 = jnp.logical_not(head0)

    wpre = w0_ref[...] + _mm(jnp.tanh(wl), w2_ref[...], "nn", P_LORA_W)
    nw = -wpre
    softplus = jnp.maximum(nw, 0.0) + jnp.log(1.0 + jnp.exp(-jnp.abs(nw)))
    wlog = -softplus - 0.5
    a = jax.nn.sigmoid(a0_ref[...] + _mm(al, a2_ref[...], "nn", P_LORA))
    g = _mm(jax.nn.sigmoid(gl), g2_ref[...], "nn", P_LORA)
    kkv = k * kk_ref[...]
    kkn = kkv / jnp.maximum(jnp.sqrt(_mm_exact_rhs(kkv * kkv, ones_bd)), 1e-12)
    k2 = k * (1.0 + (a - 1.0) * ka_ref[...])
    ld = -jnp.exp(wlog)
    bv = kkn * a

    rr = lax.broadcasted_iota(jnp.int32, (R, R), 0)
    rc = lax.broadcasted_iota(jnp.int32, (R, R), 1)
    ltri = jnp.where((rr // L == rc // L) & (rr >= rc), 1.0, 0.0).astype(BF16)
    cum3 = _dot(ltri, jnp.concatenate(_split_bf16(ld, 3), axis=1))
    cum = cum3[:, 0:WB] + cum3[:, WB:2 * WB] + cum3[:, 2 * WB:3 * WB]

    S2 = 2 * L
    srow = lax.broadcasted_iota(jnp.int32, (S2, S2), 0)
    scol = lax.broadcasted_iota(jnp.int32, (S2, S2), 1)
    same_blk = (srow // L) == (scol // L)
    strict_bd = same_blk & (srow > scol)
    incl_bd = same_blk & (srow >= scol)
    tri_masks = []
    s = 1
    while s < L:
        tri_masks.append((srow // (2 * s) == scol // (2 * s)) & ((srow // s) % 2 == 1) & ((scol // s) % 2 == 0))
        s *= 2
    mid = L // 2 - 1
    zeros_l = jnp.zeros((L, LANES), F32)

    npb = WB // LANES
    nch = R // L
    streams = [(pp, j) for pp in range(npb) for j in range(nch)]
    eye = jnp.where(srow == scol, 1.0, 0.0)

    def piece(x, pp, j):
        return x[j * L:(j + 1) * L, pp * LANES:(pp + 1) * LANES]

    at_l, bt_l, kt_l, rt_l, v_l, rho_l, gam_l = [], [], [], [], [], [], []
    for pp, j in streams:
        cumj = piece(cum, pp, j)
        cmid = cumj[mid:mid + 1, :]
        cc = cumj - cmid
        e_neg = jnp.exp(-cc)
        at_l.append(-piece(kkn, pp, j) * jnp.exp(cc - piece(ld, pp, j)))
        bt_l.append(piece(bv, pp, j) * e_neg)
        kt_l.append(piece(k2, pp, j) * e_neg)
        rt_l.append(piece(r, pp, j) * jnp.exp(cc))
        v_l.append(piece(v, pp, j))
        rho_l.append(jnp.exp(cmid))
        gam_l.append(jnp.exp(cc[L - 1:L, :]))

    def stack_heads(x):
        return jnp.concatenate([jnp.where(head0, x, 0.0), jnp.where(not_head0, x, 0.0)], axis=0)

    ats_l = [stack_heads(x) for x in at_l]
    rts_l = [stack_heads(x) for x in rt_l]
    vs_l = [jnp.concatenate([x, x], axis=0) for x in v_l]
    sc_l = [_mm(jnp.concatenate([a_, r_], axis=0), jnp.concatenate([b_, b_, k_, k_], axis=0), "nt", P_SCORE)
            for a_, r_, b_, k_ in zip(ats_l, rts_l, bt_l, kt_l)]
    nab_l = [jnp.where(strict_bd, s_[0:S2, 0:S2], 0.0) for s_ in sc_l]
    nak_l = [jnp.where(strict_bd, s_[0:S2, S2:2 * S2], 0.0) for s_ in sc_l]
    mrb_l = [jnp.where(incl_bd, s_[S2:2 * S2, 0:S2], 0.0) for s_ in sc_l]
    mrk_l = [jnp.where(incl_bd, s_[S2:2 * S2, S2:2 * S2], 0.0) for s_ in sc_l]
    t_l = [eye + jnp.where(tri_masks[0], n_, 0.0) for n_ in nab_l]
    for m in tri_masks[1:]:
        ot_l = [_mm(jnp.where(m, n_, 0.0), t_, "nn", P_TRI) for n_, t_ in zip(nab_l, t_l)]
        t_l = [t_ + _mm(t_, ot_, "nn", P_TRI) for t_, ot_ in zip(t_l, ot_l)]
    nv_l = [_mm(n_, v_, "nn", P_MAIN) for n_, v_ in zip(nak_l, vs_l)]
    au_l = [_mm(t_, jnp.concatenate([a_, nv_], axis=1), "nn", P_MAIN) for t_, a_, nv_ in zip(t_l, ats_l, nv_l)]
    ry_l = [_mm(jnp.concatenate([mb_, mk_], axis=1),
                jnp.concatenate([au_, jnp.concatenate([jnp.zeros_like(v_), v_], axis=1)], axis=0), "nn", P_MAIN)
            for mb_, mk_, au_, v_ in zip(mrb_l, mrk_l, au_l, vs_l)]
    rbar_l = [rt_ + ry_[0:L, 0:LANES] + ry_[L:S2, 0:LANES] for rt_, ry_ in zip(rt_l, ry_l)]
    ybase_l = [jnp.where(head0, ry_[0:L, LANES:2 * LANES], ry_[L:S2, LANES:2 * LANES]) for ry_ in ry_l]
    gq_l = [_mm(jnp.concatenate(
                [jnp.concatenate([au_[0:L, 0:LANES] + au_[L:S2, 0:LANES],
                                  jnp.where(head0, au_[0:L, LANES:2 * LANES], au_[L:S2, LANES:2 * LANES])], axis=1),
                 jnp.concatenate([zeros_l, v_], axis=1)], axis=0),
                jnp.concatenate([b_, k_], axis=0), "tn", P_MAIN)
            for au_, v_, b_, k_ in zip(au_l, v_l, bt_l, kt_l)]
    gmat_l = [jnp.where(same_head, gq_[0:LANES], 0.0) for gq_ in gq_l]
    q0_l = [jnp.where(same_head, gq_[LANES:2 * LANES], 0.0) for gq_ in gq_l]
    st = [s_ref[pp] for pp in range(npb)]
    ys = [[None] * nch for _ in range(npb)]
    for j in range(nch):
        for pp in range(npb):
            i = pp * nch + j
            sr = st[pp] * rho_l[i]
            ys[pp][j] = _mm(rbar_l[i], sr, "nt", P_STATE) + ybase_l[i]
            st[pp] = (sr + _mm(sr, gmat_l[i], "nn", P_STATE) + q0_l[i]) * gam_l[i]
    for pp in range(npb):
        s_ref[pp] = st[pp]
    y = jnp.concatenate([jnp.concatenate(ys[pp], axis=0) for pp in range(npb)], axis=1)

    mu = _mm_exact_rhs(y, mean_bd)
    d = y - mu
    var = _mm_exact_rhs(d * d, mean_bd)
    yn = d * lax.rsqrt(var + RWKV_LN_EPS) * lng_ref[...] + lnb_ref[...]
    bonus = _mm_exact_rhs(r * k2 * rk_ref[...], ones_bd) * v
    o_ref[...] = ((yn + bonus) * g).astype(o_ref.dtype)


def rwkv_time_mix(z, mu_p, w0, w2, a0, a2, g2p, k_k, k_a, r_k, lnx_g, lnx_b, B, T, RW):
    L = RWKV_CHUNK
    R = L * RWKV_BLOCK_CHUNKS
    W = LANES
    WB = W * RWKV_BLOCK_PAIRS
    ngrp = RW // WB
    c_gl = 3 * RW // GATE_LORA_PAD
    c_wl = (3 * RW + GATE_LORA_PAD) // W
    c_al = c_wl + 1

    def zcol(base):
        return pl.BlockSpec((None, R, WB), lambda b, p, c: (b, c, base + p))

    def zfix(idx, width):
        return pl.BlockSpec((None, R, width), lambda b, p, c: (b, c, idx))

    def mucol(base):
        return pl.BlockSpec((1, WB), lambda b, p, c: (0, base + p))

    def mufix(idx, width):
        return pl.BlockSpec((1, width), lambda b, p, c: (0, idx))

    pvec = pl.BlockSpec((1, WB), lambda b, p, c: (0, p))

    def lora(rank):
        return pl.BlockSpec((rank, WB), lambda b, p, c: (0, p))

    row = lambda a: a.reshape(1, -1)
    return pl.pallas_call(
        functools.partial(_rwkv_kernel, L=L),
        grid=(B, ngrp, T // R),
        in_specs=[zcol(0), zcol(ngrp), zcol(2 * ngrp),
                  zfix(c_gl, GATE_LORA_PAD), zfix(c_wl, W), zfix(c_al, W),
                  mucol(0), mucol(ngrp), mucol(2 * ngrp),
                  mufix(c_gl, GATE_LORA_PAD), mufix(c_wl, W), mufix(c_al, W),
                  pvec, pvec, pvec, pvec, pvec, pvec, pvec,
                  lora(DECAY_LORA), lora(ICLR_LORA), lora(GATE_LORA_PAD)],
        out_specs=pl.BlockSpec((None, R, WB), lambda b, p, c: (b, c, p)),
        out_shape=jax.ShapeDtypeStruct((B, T, RW), BF16),
        scratch_shapes=[pltpu.VMEM((RWKV_BLOCK_PAIRS, W, W), F32)] + [pltpu.VMEM((8, WB), F32)] * 3
        + [pltpu.VMEM((8, GATE_LORA_PAD), F32), pltpu.VMEM((8, W), F32), pltpu.VMEM((8, W), F32)],
        compiler_params=_cparams(("parallel", "parallel", "arbitrary")),
        name="rwkv7",
    )(z, z, z, z, z, z, mu_p, mu_p, mu_p, mu_p, mu_p, mu_p,
      row(w0), row(a0), row(k_k), row(k_a), row(r_k), row(lnx_g), row(lnx_b), w2, a2, g2p)


def _pad_cols(w, n):
    return jnp.pad(w, ((0, 0), (0, n - w.shape[1])))


def kernel(x, c, ada_w, ada_b, ada_table, ln_g, ln_b, mix_w_in, mix_mu, rwkv_w0, rwkv_w2, rwkv_a0, rwkv_a2,
           rwkv_g2, rwkv_kk, rwkv_ka, rwkv_rk, rwkv_lnx_g, rwkv_lnx_b, mix_w_out, ffn_w_gate, ffn_w_up,
           ffn_w_down, pool_w, pool_scale, moe_router_w, moe_router_b, moe_w_gate, moe_w_up, moe_w_down):
    B, T, D = x.shape
    M = B * T
    depth = ada_table.shape[0]
    RW = rwkv_w0.shape[1]
    MW = D - RW
    n_moba = MW // MOBA_HEAD_DIM
    gate_lora = rwkv_g2.shape[1]
    alpha = (2 * depth) ** 0.25

    mods = ada_mod(c, ada_w, ada_b, ada_table)
    xf = x.reshape(M, D)
    h = modulate(xf, mods[0, :, 1], mods[0, :, 0], T)
    lnrow = lambda a: a.reshape(1, D)

    za = 3 * RW + GATE_LORA_PAD + DECAY_LORA + ICLR_LORA
    n_in = za + 3 * MW
    n_pad = -(-n_in // 512) * 512
    o_wl = 3 * RW
    o_al = o_wl + DECAY_LORA
    o_gl = o_al + ICLR_LORA
    o_q = o_gl + gate_lora

    def permute_cols(w):
        return jnp.concatenate(
            [w[..., :3 * RW], _pad_cols(w[..., o_gl:o_q], GATE_LORA_PAD), w[..., o_wl:o_al], w[..., o_al:o_gl],
             _pad_cols(w[..., o_q:], n_pad - za)], axis=-1)

    for l in range(depth):
        i = l // 2
        m = mods[l]
        g1, sh2, sc2, g2 = m[:, 2], m[:, 3], m[:, 4], m[:, 5]
        if l + 1 < depth:
            sh_n, sc_n = mods[l + 1, :, 0], mods[l + 1, :, 1]
        else:
            sh_n, sc_n = jnp.zeros_like(sh2), jnp.zeros_like(sc2)
        if l % 2 == 0:
            w_in = permute_cols(mix_w_in[i]).astype(BF16)
            mu_p = permute_cols(jnp.pad(mix_mu[i], (0, mix_w_in.shape[2] - mix_mu.shape[1]))[None, :])
            z = matmul(h, w_in, 512, 512).reshape(B, T, n_pad)
            g2p = jnp.pad(rwkv_g2[i], ((0, GATE_LORA_PAD - gate_lora), (0, 0)))
            ya = rwkv_time_mix(z, mu_p, rwkv_w0[i], rwkv_w2[i], rwkv_a0[i], rwkv_a2[i], g2p, rwkv_kk[i],
                               rwkv_ka[i], rwkv_rk[i], rwkv_lnx_g[i], rwkv_lnx_b[i], B, T, RW)
            yb = moba_attention(z, za // LANES, n_moba, B, T)
            ycat = jnp.concatenate([ya, yb], axis=-1).reshape(M, D)
            xf, h = matmul_resid_ln(ycat, cast_pad_bf16(mix_w_out, i), xf, g1, lnrow(ln_g[l, 0]), lnrow(ln_b[l, 0]),
                                    sc2, sh2, T, alpha, 512, 1024)
            F = ffn_w_gate.shape[2]
            fp = -(-F // 1024) * 1024
            act = swiglu_up(h, cast_pad_bf16(ffn_w_gate, i, cols_p=fp), cast_pad_bf16(ffn_w_up, i, cols_p=fp),
                            1024, 512)
            wd = cast_pad_bf16(ffn_w_down, i, rows_p=fp)
            xf, h = matmul_resid_ln(act, wd, xf, g2, lnrow(ln_g[l, 1]), lnrow(ln_b[l, 1]), sc_n, sh_n, T, alpha,
                                    512, 1024)
        else:
            xf, h = pool_mixer_ln(xf, m[:, 1], m[:, 0], pool_w[i].astype(BF16), lnrow(pool_scale[i]), g1,
                                  lnrow(ln_g[l, 0]), lnrow(ln_b[l, 0]), sc2, sh2, T, alpha)
            E, _, NE = moe_w_gate.shape[1:]
            idx, gates = moe_router(h, moe_router_w[i], moe_router_b[i])
            row_token, tile_e, n_used, pos0, pos1 = moe_group_metadata(idx, E, MOE_TILE)
            xs = moe_gather_rows(h, row_token, MOE_TILE)
            cast3 = lambda w, bc: cast_pad_bf16(w.reshape(w.shape[0], -1, w.shape[-1]), i, bc=bc).reshape(w.shape[1:])
            act = moe_up_grouped(xs, tile_e, n_used, cast3(moe_w_gate, NE // 2), cast3(moe_w_up, NE // 2),
                                 MOE_TILE, NE // 2)
            ys = moe_down_grouped(act, tile_e, n_used, cast3(moe_w_down, 1024), MOE_TILE)
            xf, h = moe_combine_ln(ys, pos0, pos1, gates, xf, g2, lnrow(ln_g[l, 1]), lnrow(ln_b[l, 1]), sc_n, sh_n,
                                   T, alpha)
    return xf.reshape(B, T, D)
```

```python
import functools
import math

import jax
import jax.numpy as jnp
from jax import lax
from jax.experimental import pallas as pl
from jax.experimental.pallas import tpu as pltpu

F32 = jnp.float32
BF16 = jnp.bfloat16
HI = lax.Precision.HIGHEST

RWKV_HEAD_DIM = 64
DECAY_LORA = 128
ICLR_LORA = 128
GATE_LORA_PAD = 512
RWKV_LN_EPS = 64e-5
MOBA_HEAD_DIM = 128
MOBA_BLOCK = 256
MOBA_TOPK = 3
ROPE_THETA = 10000.0
POOL_WINDOWS = (2, 4, 8, 16)
POOL_HALO = 16
MOE_TOPK = 2
N_MOD = 6
LN_EPS = 1e-5
NEG_INF = -1e30

LANES = 128
LN_ROWS = 64
MOE_TILE = 256
DMA_UNROLL = 8
RWKV_CHUNK = 64
RWKV_BLOCK_CHUNKS = 4
RWKV_BLOCK_PAIRS = 4
P_LORA_W, P_LORA, P_SCORE, P_TRI, P_MAIN, P_STATE = 1, 1, 1, 1, 1, 1
VMEM_LIMIT = 56 * 1024 * 1024
VMEM_LIMIT_DOWN = 60 * 1024 * 1024


def _cparams(sem, vmem_limit=VMEM_LIMIT):
    return pltpu.CompilerParams(dimension_semantics=sem, vmem_limit_bytes=vmem_limit)


def _dot(a, b, precision=None):
    return jnp.dot(a, b, preferred_element_type=F32, precision=precision)


def _dot_nt(a, b, precision=None):
    return lax.dot_general(a, b, (((1,), (1,)), ((), ())), preferred_element_type=F32, precision=precision)


def _dot_tn(a, b, precision=None):
    return lax.dot_general(a, b, (((0,), (0,)), ((), ())), preferred_element_type=F32, precision=precision)


def _ada_kernel(c_ref, w_ref, b_ref, tab_ref, o_ref):
    c = c_ref[...]
    sc = (c * jax.nn.sigmoid(c)).astype(BF16)
    acc = _dot(sc, w_ref[...].astype(BF16)) + b_ref[...]
    o_ref[...] = acc[None, :, :] + tab_ref[...][:, None, :]


def ada_mod(c, ada_w, ada_b, ada_table):
    B, D = c.shape
    N = ada_w.shape[1]
    depth = ada_table.shape[0]
    BP = 8
    cp = jnp.zeros((BP, D), F32).at[:B].set(c)
    tn = 1024
    out = pl.pallas_call(
        _ada_kernel,
        grid=(N // tn,),
        in_specs=[
            pl.BlockSpec((BP, D), lambda j: (0, 0)),
            pl.BlockSpec((D, tn), lambda j: (0, j)),
            pl.BlockSpec((1, tn), lambda j: (0, j)),
            pl.BlockSpec((depth, tn), lambda j: (0, j)),
        ],
        out_specs=pl.BlockSpec((depth, BP, tn), lambda j: (0, 0, j)),
        out_shape=jax.ShapeDtypeStruct((depth, BP, N), F32),
        compiler_params=_cparams(("parallel",)),
        name="ada_mod",
    )(cp, ada_w, ada_b.reshape(1, N), ada_table.reshape(depth, N))
    return out[:, :B].reshape(depth, B, N_MOD, 1, D)


def _cast_pad_kernel(x_ref, o_ref, *, rows, cols):
    br, bc = o_ref.shape
    i, j = pl.program_id(0), pl.program_id(1)
    inside = ((i + 1) * br <= rows) & ((j + 1) * bc <= cols)

    @pl.when(inside)
    def _():
        o_ref[...] = x_ref[...].astype(o_ref.dtype)

    @pl.when(jnp.logical_not(inside))
    def _():
        r = i * br + lax.broadcasted_iota(jnp.int32, (br, bc), 0)
        c = j * bc + lax.broadcasted_iota(jnp.int32, (br, bc), 1)
        o_ref[...] = jnp.where((r < rows) & (c < cols), x_ref[...], 0.0).astype(o_ref.dtype)


def cast_pad_bf16(w_stack, layer, rows_p=None, cols_p=None, br=1024, bc=1024):
    _, rows, cols = w_stack.shape
    rows_p = rows_p or rows
    cols_p = cols_p or cols
    bc = min(bc, cols_p)
    last_r = (rows - 1) // br
    last_c = (cols - 1) // bc
    return pl.pallas_call(
        functools.partial(_cast_pad_kernel, rows=rows, cols=cols),
        grid=(rows_p // br, cols_p // bc),
        in_specs=[pl.BlockSpec((None, br, bc),
                               lambda i, j: (layer, jnp.minimum(i, last_r), jnp.minimum(j, last_c)))],
        out_specs=pl.BlockSpec((br, bc), lambda i, j: (i, j)),
        out_shape=jax.ShapeDtypeStruct((rows_p, cols_p), BF16),
        compiler_params=_cparams(("parallel", "parallel")),
        name="cast_pad",
    )(w_stack)


def _modulate_kernel(x_ref, sc_ref, sh_ref, o_ref):
    o_ref[...] = (x_ref[...] * (1.0 + sc_ref[...]) + sh_ref[...]).astype(o_ref.dtype)


def modulate(x, sc, sh, T):
    M, D = x.shape
    tm = 512
    tpb = T // tm
    vec = pl.BlockSpec((None, 1, D), lambda i: (i // tpb, 0, 0))
    return pl.pallas_call(
        _modulate_kernel,
        grid=(M // tm,),
        in_specs=[pl.BlockSpec((tm, D), lambda i: (i, 0)), vec, vec],
        out_specs=pl.BlockSpec((tm, D), lambda i: (i, 0)),
        out_shape=jax.ShapeDtypeStruct((M, D), BF16),
        compiler_params=_cparams(("parallel",)),
        name="modulate",
    )(x, sc, sh)


def _mm_kernel(a_ref, w_ref, o_ref):
    o_ref[...] = _dot(a_ref[...], w_ref[...]).astype(o_ref.dtype)


def matmul(a, w, tm, tn, out_dtype=F32):
    M, K = a.shape
    N = w.shape[1]
    return pl.pallas_call(
        _mm_kernel,
        grid=(N // tn, M // tm),
        in_specs=[pl.BlockSpec((tm, K), lambda j, i: (i, 0)),
                  pl.BlockSpec((K, tn), lambda j, i: (0, j))],
        out_specs=pl.BlockSpec((tm, tn), lambda j, i: (i, j)),
        out_shape=jax.ShapeDtypeStruct((M, N), out_dtype),
        compiler_params=_cparams(("parallel", "parallel")),
        name="matmul",
    )(a, w)


def _resid_ln_mod(x, y, gate, lng, lnb, sc, sh, alpha):
    r = alpha * x + gate * y
    mu = jnp.mean(r, axis=-1, keepdims=True)
    d = r - mu
    var = jnp.mean(d * d, axis=-1, keepdims=True)
    xn = d * lax.rsqrt(var + LN_EPS) * lng + lnb
    return xn, xn * (1.0 + sc) + sh


def _mm_ln_kernel(a_ref, w_ref, x_ref, gate_ref, lng_ref, lnb_ref, sc_ref, sh_ref,
                  xo_ref, ho_ref, *, alpha):
    k = pl.program_id(1)

    @pl.when(k == 0)
    def _():
        xo_ref[...] = _dot(a_ref[...], w_ref[...])

    @pl.when(k > 0)
    def _():
        xo_ref[...] += _dot(a_ref[...], w_ref[...])

    @pl.when(k == pl.num_programs(1) - 1)
    def _():
        for r0 in range(0, xo_ref.shape[0], LN_ROWS):
            rows = pl.ds(r0, LN_ROWS)
            xn, h = _resid_ln_mod(x_ref[rows, :], xo_ref[rows, :], gate_ref[...], lng_ref[...], lnb_ref[...],
                                  sc_ref[...], sh_ref[...], alpha)
            xo_ref[rows, :] = xn
            ho_ref[rows, :] = h.astype(ho_ref.dtype)


def matmul_resid_ln(a, w, x, gate, lng, lnb, sc, sh, T, alpha, tm, tk):
    M, K = a.shape
    D = w.shape[1]
    tpb = T // tm
    vec_b = pl.BlockSpec((None, 1, D), lambda i, k: (i // tpb, 0, 0))
    vec = pl.BlockSpec((1, D), lambda i, k: (0, 0))
    row = pl.BlockSpec((tm, D), lambda i, k: (i, 0))
    return pl.pallas_call(
        functools.partial(_mm_ln_kernel, alpha=alpha),
        grid=(M // tm, K // tk),
        in_specs=[pl.BlockSpec((tm, tk), lambda i, k: (i, k)),
                  pl.BlockSpec((tk, D), lambda i, k: (k, 0)),
                  pl.BlockSpec((tm, D), lambda i, k: (i, 0), pipeline_mode=pl.Buffered(1)),
                  vec_b, vec, vec, vec_b, vec_b],
        out_specs=[row, row],
        out_shape=[jax.ShapeDtypeStruct((M, D), F32), jax.ShapeDtypeStruct((M, D), BF16)],
        compiler_params=_cparams(("parallel", "arbitrary"), VMEM_LIMIT_DOWN),
        name="matmul_resid_ln",
    )(a, w, x, gate, lng, lnb, sc, sh)


def _swiglu_up_kernel(a_ref, wg_ref, wu_ref, o_ref):
    a = a_ref[...]
    g = _dot(a, wg_ref[...])
    u = _dot(a, wu_ref[...])
    o_ref[...] = (g * jax.nn.sigmoid(g) * u).astype(o_ref.dtype)


def swiglu_up(a, wg, wu, tm, tn):
    M, K = a.shape
    N = wg.shape[1]
    wspec = pl.BlockSpec((K, tn), lambda i, j: (0, j))
    return pl.pallas_call(
        _swiglu_up_kernel,
        grid=(M // tm, N // tn),
        in_specs=[pl.BlockSpec((tm, K), lambda i, j: (i, 0)), wspec, wspec],
        out_specs=pl.BlockSpec((tm, tn), lambda i, j: (i, j)),
        out_shape=jax.ShapeDtypeStruct((M, N), BF16),
        compiler_params=_cparams(("parallel", "parallel")),
        name="swiglu_up",
    )(a, wg, wu)


def _router_kernel(h_ref, rw_ref, rb_ref, idx_ref, gate_ref, *, n_experts):
    logits = _dot(h_ref[...], rw_ref[...], HI) + rb_ref[...]
    lane = lax.broadcasted_iota(jnp.int32, logits.shape, 1)
    logits = jnp.where(lane < n_experts, logits, -jnp.inf)
    m1 = jnp.max(logits, axis=-1, keepdims=True)
    i1 = jnp.min(jnp.where(logits == m1, lane, LANES), axis=-1, keepdims=True)
    oh1 = lane == i1
    rest = jnp.where(oh1, -jnp.inf, logits)
    m2 = jnp.max(rest, axis=-1, keepdims=True)
    i2 = jnp.min(jnp.where(rest == m2, lane, LANES), axis=-1, keepdims=True)
    e = jnp.exp(m2 - m1)
    g1 = 1.0 / (1.0 + e)
    g2 = e / (1.0 + e)
    idx_ref[...] = jnp.where(lane == 0, i1, jnp.where(lane == 1, i2, 0))
    gate_ref[...] = jnp.where(lane == 0, g1, jnp.where(lane == 1, g2, 0.0))


def moe_router(h, router_w, router_b):
    M, D = h.shape
    E = router_w.shape[1]
    tm = 512
    rw = jnp.zeros((D, LANES), F32).at[:, :E].set(router_w)
    rb = jnp.zeros((1, LANES), F32).at[0, :E].set(router_b)
    out = pl.BlockSpec((tm, LANES), lambda i: (i, 0))
    return pl.pallas_call(
        functools.partial(_router_kernel, n_experts=E),
        grid=(M // tm,),
        in_specs=[pl.BlockSpec((tm, D), lambda i: (i, 0)),
                  pl.BlockSpec((D, LANES), lambda i: (0, 0)),
                  pl.BlockSpec((1, LANES), lambda i: (0, 0))],
        out_specs=[out, out],
        out_shape=[jax.ShapeDtypeStruct((M, LANES), jnp.int32), jax.ShapeDtypeStruct((M, LANES), F32)],
        compiler_params=_cparams(("parallel",)),
        name="moe_router",
    )(h, rw, rb)


def _gather_rows_kernel(tok_ref, h_hbm, o_ref, buf_ref, sem):
    i = pl.program_id(0)
    tm = o_ref.shape[0]
    slot = i % 2

    def start_tile(tile, s):
        def body(r0, c):
            for u in range(DMA_UNROLL):
                r = r0 * DMA_UNROLL + u
                pltpu.make_async_copy(h_hbm.at[pl.ds(tok_ref[tile * tm + r], 1), :], buf_ref.at[s, pl.ds(r, 1), :],
                                      sem.at[s]).start()
            return c

        lax.fori_loop(0, tm // DMA_UNROLL, body, 0)

    @pl.when(i == 0)
    def _():
        start_tile(i, slot)

    @pl.when(i + 1 < pl.num_programs(0))
    def _():
        start_tile(i + 1, 1 - slot)

    pltpu.make_async_copy(h_hbm.at[pl.ds(0, tm), :], buf_ref.at[slot], sem.at[slot]).wait()
    o_ref[...] = buf_ref[slot].astype(o_ref.dtype)


def moe_gather_rows(h, row_token, tm):
    D = h.shape[1]
    NR = row_token.shape[0]
    return pl.pallas_call(
        _gather_rows_kernel,
        grid_spec=pltpu.PrefetchScalarGridSpec(
            num_scalar_prefetch=1,
            grid=(NR // tm,),
            in_specs=[pl.BlockSpec(memory_space=pl.ANY)],
            out_specs=pl.BlockSpec((tm, D), lambda i, tok: (i, 0)),
            scratch_shapes=[pltpu.VMEM((2, tm, D), F32), pltpu.SemaphoreType.DMA((2,))]),
        out_shape=jax.ShapeDtypeStruct((NR, D), BF16),
        compiler_params=_cparams(("arbitrary",)),
        name="moe_gather",
    )(row_token, h)


def _moe_up_kernel(te_ref, nu_ref, a_ref, wg_ref, wu_ref, o_ref):
    i = pl.program_id(1)

    @pl.when(i < nu_ref[0])
    def _():
        a = a_ref[...]
        g = _dot(a, wg_ref[...])
        u = _dot(a, wu_ref[...])
        o_ref[...] = (g * jax.nn.sigmoid(g) * u).astype(o_ref.dtype)

    @pl.when(i >= nu_ref[0])
    def _():
        o_ref[...] = jnp.zeros_like(o_ref)


def moe_up_grouped(xs, tile_e, n_used, wg, wu, tm, tn):
    NR, K = xs.shape
    E, _, NE = wg.shape
    wspec = pl.BlockSpec((None, K, tn), lambda j, i, te, nu: (te[i], 0, j))
    return pl.pallas_call(
        _moe_up_kernel,
        grid_spec=pltpu.PrefetchScalarGridSpec(
            num_scalar_prefetch=2,
            grid=(NE // tn, NR // tm),
            in_specs=[pl.BlockSpec((tm, K), lambda j, i, te, nu: (i, 0)), wspec, wspec],
            out_specs=pl.BlockSpec((tm, tn), lambda j, i, te, nu: (i, j))),
        out_shape=jax.ShapeDtypeStruct((NR, NE), BF16),
        compiler_params=_cparams(("parallel", "parallel")),
        name="moe_up",
    )(tile_e, n_used, xs, wg, wu)


def _moe_down_kernel(te_ref, nu_ref, a_ref, w_ref, o_ref):
    i = pl.program_id(0)

    @pl.when(i < nu_ref[0])
    def _():
        o_ref[...] = _dot(a_ref[...], w_ref[...])

    @pl.when(i >= nu_ref[0])
    def _():
        o_ref[...] = jnp.zeros_like(o_ref)


def moe_down_grouped(act, tile_e, n_used, wd, tm):
    NR, NE = act.shape
    D = wd.shape[2]
    return pl.pallas_call(
        _moe_down_kernel,
        grid_spec=pltpu.PrefetchScalarGridSpec(
            num_scalar_prefetch=2,
            grid=(NR // tm,),
            in_specs=[pl.BlockSpec((tm, NE), lambda i, te, nu: (i, 0)),
                      pl.BlockSpec((None, NE, D), lambda i, te, nu: (te[i], 0, 0))],
            out_specs=pl.BlockSpec((tm, D), lambda i, te, nu: (i, 0))),
        out_shape=jax.ShapeDtypeStruct((NR, D), F32),
        compiler_params=_cparams(("parallel",)),
        name="moe_down",
    )(tile_e, n_used, act, wd)


def _moe_combine_ln_kernel(p0_ref, p1_ref, ys_hbm, x_ref, rg_ref, gate_ref, lng_ref, lnb_ref, sc_ref, sh_ref,
                           xo_ref, ho_ref, b0_ref, b1_ref, sem, *, alpha):
    i = pl.program_id(0)
    tm = x_ref.shape[0]
    slot = i % 2

    def start_tile(tile, s):
        def body(r0, c):
            for u in range(DMA_UNROLL):
                r = r0 * DMA_UNROLL + u
                t = tile * tm + r
                pltpu.make_async_copy(ys_hbm.at[pl.ds(p0_ref[t], 1), :], b0_ref.at[s, pl.ds(r, 1), :],
                                      sem.at[0, s]).start()
                pltpu.make_async_copy(ys_hbm.at[pl.ds(p1_ref[t], 1), :], b1_ref.at[s, pl.ds(r, 1), :],
                                      sem.at[1, s]).start()
            return c

        lax.fori_loop(0, tm // DMA_UNROLL, body, 0)

    @pl.when(i == 0)
    def _():
        start_tile(i, slot)

    @pl.when(i + 1 < pl.num_programs(0))
    def _():
        start_tile(i + 1, 1 - slot)

    pltpu.make_async_copy(ys_hbm.at[pl.ds(0, tm), :], b0_ref.at[slot], sem.at[0, slot]).wait()
    pltpu.make_async_copy(ys_hbm.at[pl.ds(0, tm), :], b1_ref.at[slot], sem.at[1, slot]).wait()
    for r0 in range(0, tm, LN_ROWS):
        rows = pl.ds(r0, LN_ROWS)
        rg = rg_ref[rows, :]
        y = rg[:, 0:1] * b0_ref[slot, rows, :] + rg[:, 1:2] * b1_ref[slot, rows, :]
        xn, h = _resid_ln_mod(x_ref[rows, :], y, gate_ref[...], lng_ref[...], lnb_ref[...],
                              sc_ref[...], sh_ref[...], alpha)
        xo_ref[rows, :] = xn
        ho_ref[rows, :] = h.astype(ho_ref.dtype)


def moe_combine_ln(ys, pos0, pos1, rgates, x, gate, lng, lnb, sc, sh, T, alpha):
    M, D = x.shape
    tm = 256
    tpb = T // tm
    vec_b = pl.BlockSpec((None, 1, D), lambda i, p0, p1: (i // tpb, 0, 0))
    vec = pl.BlockSpec((1, D), lambda i, p0, p1: (0, 0))
    row = pl.BlockSpec((tm, D), lambda i, p0, p1: (i, 0))
    return pl.pallas_call(
        functools.partial(_moe_combine_ln_kernel, alpha=alpha),
        grid_spec=pltpu.PrefetchScalarGridSpec(
            num_scalar_prefetch=2,
            grid=(M // tm,),
            in_specs=[pl.BlockSpec(memory_space=pl.ANY), row,
                      pl.BlockSpec((tm, LANES), lambda i, p0, p1: (i, 0)), vec_b, vec, vec, vec_b, vec_b],
            out_specs=[row, row],
            scratch_shapes=[pltpu.VMEM((2, tm, D), F32), pltpu.VMEM((2, tm, D), F32),
                            pltpu.SemaphoreType.DMA((2, 2))]),
        out_shape=[jax.ShapeDtypeStruct((M, D), F32), jax.ShapeDtypeStruct((M, D), BF16)],
        compiler_params=_cparams(("arbitrary",)),
        name="moe_combine_ln",
    )(pos0, pos1, ys, x, rgates, gate, lng, lnb, sc, sh)


def moe_group_metadata(idx, n_experts, tm):
    M = idx.shape[0]
    n_pairs = MOE_TOPK * M
    e_flat = idx[:, :MOE_TOPK].reshape(-1)
    onehot = (e_flat[:, None] == jnp.arange(n_experts)[None, :]).astype(F32)
    blk = LANES
    oh3 = onehot.reshape(n_pairs // blk, blk, n_experts)
    tri = (jnp.arange(blk)[:, None] > jnp.arange(blk)[None, :]).astype(F32)
    within = jnp.einsum("ts,bse->bte", tri, oh3, precision=HI)
    tot = oh3.sum(axis=1)
    before = jnp.cumsum(tot, axis=0) - tot
    excl = (within + before[:, None, :]).reshape(n_pairs, n_experts)
    rank = jnp.sum(excl * onehot, axis=1).astype(jnp.int32)
    counts = tot.sum(axis=0).astype(jnp.int32)
    padded = -(-counts // tm) * tm
    ends = jnp.cumsum(padded)
    starts = ends - padded
    pos = jnp.sum(onehot * starts[None, :].astype(F32), axis=1).astype(jnp.int32) + rank
    n_rows = n_pairs + n_experts * tm
    row_token = jnp.zeros((n_rows,), jnp.int32).at[pos].set(jnp.arange(n_pairs, dtype=jnp.int32) // MOE_TOPK)
    tile_start = jnp.arange(n_rows // tm, dtype=jnp.int32) * tm
    tile_e = jnp.minimum(jnp.sum(tile_start[:, None] >= ends[None, :], axis=1), n_experts - 1).astype(jnp.int32)
    n_used = (ends[-1:] // tm).astype(jnp.int32)
    pos = pos.reshape(M, MOE_TOPK)
    return row_token, tile_e, n_used, pos[:, 0], pos[:, 1]


def _pool_kernel(x_ref, xh_ref, sc_ref, sh_ref, w_ref, ps_ref, gate_ref, lng_ref, lnb_ref,
                 sc2_ref, sh2_ref, xo_ref, ho_ref, hh_ref, y_ref, *, tm, tpb, alpha):
    i = pl.program_id(0)
    D = x_ref.shape[1]
    dg = D // len(POOL_WINDOWS)
    sc = sc_ref[...]
    sh = sh_ref[...]
    x = x_ref[...]
    first = (i % tpb) == 0
    halo = xh_ref[...] * (1.0 + sc) + sh
    hh_ref[0:POOL_HALO, :] = jnp.where(first, 0.0, halo)
    hh_ref[POOL_HALO:, :] = x * (1.0 + sc) + sh
    t = (i % tpb) * tm + lax.broadcasted_iota(jnp.int32, (tm, 1), 0)
    tcnt = (t + 1).astype(F32)
    for g, win in enumerate(POOL_WINDOWS):
        cols = slice(g * dg, (g + 1) * dg)
        h = hh_ref[POOL_HALO:, cols]
        s = h
        for d in range(1, win):
            s = s + hh_ref[pl.ds(POOL_HALO - d, tm), cols]
        inv = 1.0 / jnp.minimum(tcnt, float(win))
        p = (s * inv - h).astype(BF16)
        y_ref[:, cols] = _dot(p, w_ref[g])
    y = y_ref[...] * ps_ref[...]
    xn, hn = _resid_ln_mod(x, y, gate_ref[...], lng_ref[...], lnb_ref[...], sc2_ref[...], sh2_ref[...], alpha)
    xo_ref[...] = xn
    ho_ref[...] = hn.astype(ho_ref.dtype)


def pool_mixer_ln(x, sc, sh, pool_w, pool_scale, gate, lng, lnb, sc2, sh2, T, alpha):
    M, D = x.shape
    G, dg, _ = pool_w.shape
    tm = 256
    tpb = T // tm
    hb = tm // POOL_HALO
    vec_b = pl.BlockSpec((None, 1, D), lambda i: (i // tpb, 0, 0))
    vec = pl.BlockSpec((1, D), lambda i: (0, 0))
    row = pl.BlockSpec((tm, D), lambda i: (i, 0))
    return pl.pallas_call(
        functools.partial(_pool_kernel, tm=tm, tpb=tpb, alpha=alpha),
        grid=(M // tm,),
        in_specs=[row,
                  pl.BlockSpec((POOL_HALO, D), lambda i: (jnp.maximum(i * hb - 1, 0), 0)),
                  vec_b, vec_b,
                  pl.BlockSpec((G, dg, dg), lambda i: (0, 0, 0)),
                  vec, vec_b, vec, vec, vec_b, vec_b],
        out_specs=[row, row],
        out_shape=[jax.ShapeDtypeStruct((M, D), F32), jax.ShapeDtypeStruct((M, D), F32)],
        scratch_shapes=[pltpu.VMEM((tm + POOL_HALO, D), F32), pltpu.VMEM((tm, D), F32)],
        compiler_params=_cparams(("parallel",)),
        name="pool_mixer",
    )(x, x, sc, sh, pool_w, pool_scale, gate, lng, lnb, sc2, sh2)


def _moba_kernel(q_ref, k_ref, v_ref, cos_ref, sin_ref, o_ref, *, nb):
    bs = MOBA_BLOCK
    cos = cos_ref[...]
    sin = sin_ref[...]

    def rope(t):
        return t * cos + pltpu.roll(t, MOBA_HEAD_DIM // 2, axis=1) * sin

    qr = rope(q_ref[...])
    kr = rope(k_ref[...])
    T = qr.shape[0]
    prow = lax.broadcasted_iota(jnp.int32, (LANES, T), 0)
    pcol = lax.broadcasted_iota(jnp.int32, (LANES, T), 1)
    pool = jnp.where(pcol // bs == prow, 1.0 / bs, 0.0).astype(F32)
    kmean = _dot(pool, kr, HI)
    qb = qr.astype(BF16)
    kb = kr.astype(BF16)
    vb = v_ref[...].astype(BF16)
    scale = MOBA_HEAD_DIM ** -0.5
    lane = lax.broadcasted_iota(jnp.int32, (bs, LANES), 1)
    rowi = lax.broadcasted_iota(jnp.int32, (bs, bs), 0)
    coli = lax.broadcasted_iota(jnp.int32, (bs, bs), 1)
    for i in range(nb):
        rows = slice(i * bs, (i + 1) * bs)
        qi = qb[rows]
        s_list = [jnp.where(coli <= rowi, _dot_nt(qi, kb[rows]) * scale, NEG_INF)]
        if i > 0:
            gate = _dot_nt(qr[rows], kmean, HI)
            valid = lane < i
            gate = jnp.where(valid, gate, -jnp.inf)
            rank = jnp.zeros((bs, LANES), jnp.int32)
            for j in range(i):
                gj = gate[:, j:j + 1]
                ahead = (gj > gate) | ((gj == gate) & (j < lane))
                rank = rank + ahead.astype(jnp.int32)
            sel = valid & (rank < MOBA_TOPK)
            for j in range(i):
                sj = _dot_nt(qi, kb[j * bs:(j + 1) * bs]) * scale
                s_list.append(jnp.where(sel[:, j:j + 1], sj, NEG_INF))
        smax = s_list[0]
        for s in s_list[1:]:
            smax = jnp.maximum(smax, s)
        m = smax.max(axis=-1, keepdims=True)
        psum = jnp.zeros((bs, bs), F32)
        acc = jnp.zeros((bs, MOBA_HEAD_DIM), F32)
        for jj, s in enumerate(s_list):
            j = i if jj == 0 else jj - 1
            p = jnp.exp(s - m)
            psum = psum + p
            acc = acc + _dot(p.astype(BF16), vb[j * bs:(j + 1) * bs])
        l = psum.sum(axis=-1, keepdims=True)
        o_ref[rows, :] = (acc / l).astype(o_ref.dtype)


def moba_attention(z, q_col, n_heads, B, T):
    assert T % MOBA_BLOCK == 0
    nb = T // MOBA_BLOCK
    dh = MOBA_HEAD_DIM
    inv = jnp.power(ROPE_THETA, -jnp.arange(0, dh, 2, dtype=F32) / dh)
    ang = jnp.arange(T, dtype=F32)[:, None] * inv[None, :]
    cos = jnp.concatenate([jnp.cos(ang), jnp.cos(ang)], axis=-1)
    sin = jnp.concatenate([-jnp.sin(ang), jnp.sin(ang)], axis=-1)

    def col(off):
        return pl.BlockSpec((None, T, dh), lambda b, h: (b, 0, q_col + off * n_heads + h))

    tab = pl.BlockSpec((T, dh), lambda b, h: (0, 0))
    return pl.pallas_call(
        functools.partial(_moba_kernel, nb=nb),
        grid=(B, n_heads),
        in_specs=[col(0), col(1), col(2), tab, tab],
        out_specs=pl.BlockSpec((None, T, dh), lambda b, h: (b, 0, h)),
        out_shape=jax.ShapeDtypeStruct((B, T, n_heads * dh), BF16),
        compiler_params=_cparams(("parallel", "parallel")),
        name="moba",
    )(z, z, z, cos, sin)


def _split_bf16(x, n):
    parts, rem = [], x
    for i in range(n):
        p = rem.astype(BF16)
        parts.append(p)
        if i + 1 < n:
            rem = rem - p.astype(F32)
    return parts


_DIMS = {"nn": ((1,), (0,)), "nt": ((1,), (1,)), "tn": ((0,), (0,))}


def _mm(a, b, form="nn", passes=1):
    if passes == 1:
        a3, b3 = a.astype(BF16), b.astype(BF16)
    else:
        ah, al = _split_bf16(a, 2)
        bh, bl = _split_bf16(b, 2)
        a3 = jnp.concatenate([ah, ah, al], axis=0 if form == "tn" else 1)
        b3 = jnp.concatenate([bh, bl, bh], axis=1 if form == "nt" else 0)
    return lax.dot_general(a3, b3, (_DIMS[form], ((), ())), preferred_element_type=F32)


def _mm_exact_rhs(a, b_bf16):
    a3 = jnp.concatenate(_split_bf16(a, 3), axis=1)
    b3 = jnp.concatenate([b_bf16, b_bf16, b_bf16], axis=0)
    return _dot(a3, b3)


def _rwkv_kernel(zr_ref, zk_ref, zv_ref, zgl_ref, zwl_ref, zal_ref,
                 mur_ref, muk_ref, muv_ref, mugl_ref, muwl_ref, mual_ref,
                 w0_ref, a0_ref, kk_ref, ka_ref, rk_ref, lng_ref, lnb_ref,
                 w2_ref, a2_ref, g2_ref,
                 o_ref,
                 s_ref, pr_ref, pk_ref, pv_ref, pgl_ref, pwl_ref, pal_ref, *, L):
    R = zr_ref.shape[0]
    hd = RWKV_HEAD_DIM
    c = pl.program_id(2)

    @pl.when(c == 0)
    def _():
        s_ref[...] = jnp.zeros_like(s_ref)
        for p in (pr_ref, pk_ref, pv_ref, pgl_ref, pwl_ref, pal_ref):
            p[...] = jnp.zeros_like(p)

    def shift(z_ref, p_ref, mu_ref):
        z = z_ref[...]
        row = lax.broadcasted_iota(jnp.int32, z.shape, 0)
        zp = jnp.where(row == 0, p_ref[0:1, :], pltpu.roll(z, 1, axis=0))
        p_ref[0:1, :] = z[R - 1:R, :]
        return z + (zp - z) * mu_ref[...]

    r = shift(zr_ref, pr_ref, mur_ref)
    k = shift(zk_ref, pk_ref, muk_ref)
    v = shift(zv_ref, pv_ref, muv_ref)
    gl = shift(zgl_ref, pgl_ref, mugl_ref)
    wl = shift(zwl_ref, pwl_ref, muwl_ref)
    al = shift(zal_ref, pal_ref, mual_ref)

    WB = zr_ref.shape[1]
    lane_r = lax.broadcasted_iota(jnp.int32, (WB, WB), 0)
    lane_c = lax.broadcasted_iota(jnp.int32, (WB, WB), 1)
    same_head_b = (lane_r // hd) == (lane_c // hd)
    ones_bd = jnp.where(same_head_b, 1.0, 0.0).astype(BF16)
    mean_bd = jnp.where(same_head_b, 1.0 / hd, 0.0).astype(BF16)
    same_head = same_head_b[0:LANES, 0:LANES]
    head0 = lax.broadcasted_iota(jnp.int32, (1, LANES), 1) < hd
    not_head0 = jnp.logical_not(head0)

    wpre = w0_ref[...] + _mm(jnp.tanh(wl), w2_ref[...], "nn", P_LORA_W)
    nw = -wpre
    softplus = jnp.maximum(nw, 0.0) + jnp.log(1.0 + jnp.exp(-jnp.abs(nw)))
    wlog = -softplus - 0.5
    a = jax.nn.sigmoid(a0_ref[...] + _mm(al, a2_ref[...], "nn", P_LORA))
    g = _mm(jax.nn.sigmoid(gl), g2_ref[...], "nn", P_LORA)
    kkv = k * kk_ref[...]
    kkn = kkv / jnp.maximum(jnp.sqrt(_mm_exact_rhs(kkv * kkv, ones_bd)), 1e-12)
    k2 = k * (1.0 + (a - 1.0) * ka_ref[...])
    ld = -jnp.exp(wlog)
    bv = kkn * a

    rr = lax.broadcasted_iota(jnp.int32, (R, R), 0)
    rc = lax.broadcasted_iota(jnp.int32, (R, R), 1)
    ltri = jnp.where((rr // L == rc // L) & (rr >= rc), 1.0, 0.0).astype(BF16)
    cum3 = _dot(ltri, jnp.concatenate(_split_bf16(ld, 3), axis=1))
    cum = cum3[:, 0:WB] + cum3[:, WB:2 * WB] + cum3[:, 2 * WB:3 * WB]

    S2 = 2 * L
    srow = lax.broadcasted_iota(jnp.int32, (S2, S2), 0)
    scol = lax.broadcasted_iota(jnp.int32, (S2, S2), 1)
    same_blk = (srow // L) == (scol // L)
    strict_bd = same_blk & (srow > scol)
    incl_bd = same_blk & (srow >= scol)
    tri_masks = []
    s = 1
    while s < L:
        tri_masks.append((srow // (2 * s) == scol // (2 * s)) & ((srow // s) % 2 == 1) & ((scol // s) % 2 == 0))
        s *= 2
    mid = L // 2 - 1
    zeros_l = jnp.zeros((L, LANES), F32)

    npb = WB // LANES
    nch = R // L
    streams = [(pp, j) for pp in range(npb) for j in range(nch)]
    eye = jnp.where(srow == scol, 1.0, 0.0)

    def piece(x, pp, j):
        return x[j * L:(j + 1) * L, pp * LANES:(pp + 1) * LANES]

    at_l, bt_l, kt_l, rt_l, v_l, rho_l, gam_l = [], [], [], [], [], [], []
    for pp, j in streams:
        cumj = piece(cum, pp, j)
        cmid = cumj[mid:mid + 1, :]
        cc = cumj - cmid
        e_neg = jnp.exp(-cc)
        at_l.append(-piece(kkn, pp, j) * jnp.exp(cc - piece(ld, pp, j)))
        bt_l.append(piece(bv, pp, j) * e_neg)
        kt_l.append(piece(k2, pp, j) * e_neg)
        rt_l.append(piece(r, pp, j) * jnp.exp(cc))
        v_l.append(piece(v, pp, j))
        rho_l.append(jnp.exp(cmid))
        gam_l.append(jnp.exp(cc[L - 1:L, :]))

    def stack_heads(x):
        return jnp.concatenate([jnp.where(head0, x, 0.0), jnp.where(not_head0, x, 0.0)], axis=0)

    ats_l = [stack_heads(x) for x in at_l]
    rts_l = [stack_heads(x) for x in rt_l]
    vs_l = [jnp.concatenate([x, x], axis=0) for x in v_l]
    sc_l = [_mm(jnp.concatenate([a_, r_], axis=0), jnp.concatenate([b_, b_, k_, k_], axis=0), "nt", P_SCORE)
            for a_, r_, b_, k_ in zip(ats_l, rts_l, bt_l, kt_l)]
    nab_l = [jnp.where(strict_bd, s_[0:S2, 0:S2], 0.0) for s_ in sc_l]
    nak_l = [jnp.where(strict_bd, s_[0:S2, S2:2 * S2], 0.0) for s_ in sc_l]
    mrb_l = [jnp.where(incl_bd, s_[S2:2 * S2, 0:S2], 0.0) for s_ in sc_l]
    mrk_l = [jnp.where(incl_bd, s_[S2:2 * S2, S2:2 * S2], 0.0) for s_ in sc_l]
    t_l = [eye + jnp.where(tri_masks[0], n_, 0.0) for n_ in nab_l]
    for m in tri_masks[1:]:
        ot_l = [_mm(jnp.where(m, n_, 0.0), t_, "nn", P_TRI) for n_, t_ in zip(nab_l, t_l)]
        t_l = [t_ + _mm(t_, ot_, "nn", P_TRI) for t_, ot_ in zip(t_l, ot_l)]
    nv_l = [_mm(n_, v_, "nn", P_MAIN) for n_, v_ in zip(nak_l, vs_l)]
    au_l = [_mm(t_, jnp.concatenate([a_, nv_], axis=1), "nn", P_MAIN) for t_, a_, nv_ in zip(t_l, ats_l, nv_l)]
    ry_l = [_mm(jnp.concatenate([mb_, mk_], axis=1),
                jnp.concatenate([au_, jnp.concatenate([jnp.zeros_like(v_), v_], axis=1)], axis=0), "nn", P_MAIN)
            for mb_, mk_, au_, v_ in zip(mrb_l, mrk_l, au_l, vs_l)]
    rbar_l = [rt_ + ry_[0:L, 0:LANES] + ry_[L:S2, 0:LANES] for rt_, ry_ in zip(rt_l, ry_l)]
    ybase_l = [jnp.where(head0, ry_[0:L, LANES:2 * LANES], ry_[L:S2, LANES:2 * LANES]) for ry_ in ry_l]
    gq_l = [_mm(jnp.concatenate(
                [jnp.concatenate([au_[0:L, 0:LANES] + au_[L:S2, 0:LANES],
                                  jnp.where(head0, au_[0:L, LANES:2 * LANES], au_[L:S2, LANES:2 * LANES])], axis=1),
                 jnp.concatenate([zeros_l, v_], axis=1)], axis=0),
                jnp.concatenate([b_, k_], axis=0), "tn", P_MAIN)
            for au_, v_, b_, k_ in zip(au_l, v_l, bt_l, kt_l)]
    gmat_l = [jnp.where(same_head, gq_[0:LANES], 0.0) for gq_ in gq_l]
    q0_l = [jnp.where(same_head, gq_[LANES:2 * LANES], 0.0) for gq_ in gq_l]
    st = [s_ref[pp] for pp in range(npb)]
    ys = [[None] * nch for _ in range(npb)]
    for j in range(nch):
        for pp in range(npb):
            i = pp * nch + j
            sr = st[pp] * rho_l[i]
            ys[pp][j] = _mm(rbar_l[i], sr, "nt", P_STATE) + ybase_l[i]
            st[pp] = (sr + _mm(sr, gmat_l[i], "nn", P_STATE) + q0_l[i]) * gam_l[i]
    for pp in range(npb):
        s_ref[pp] = st[pp]
    y = jnp.concatenate([jnp.concatenate(ys[pp], axis=0) for pp in range(npb)], axis=1)

    mu = _mm_exact_rhs(y, mean_bd)
    d = y - mu
    var = _mm_exact_rhs(d * d, mean_bd)
    yn = d * lax.rsqrt(var + RWKV_LN_EPS) * lng_ref[...] + lnb_ref[...]
    bonus = _mm_exact_rhs(r * k2 * rk_ref[...], ones_bd) * v
    o_ref[...] = ((yn + bonus) * g).astype(o_ref.dtype)


def rwkv_time_mix(z, mu_p, w0, w2, a0, a2, g2p, k_k, k_a, r_k, lnx_g, lnx_b, B, T, RW):
    L = RWKV_CHUNK
    R = L * RWKV_BLOCK_CHUNKS
    W = LANES
    WB = W * RWKV_BLOCK_PAIRS
    ngrp = RW // WB
    c_gl = 3 * RW // GATE_LORA_PAD
    c_wl = (3 * RW + GATE_LORA_PAD) // W
    c_al = c_wl + 1

    def zcol(base):
        return pl.BlockSpec((None, R, WB), lambda b, p, c: (b, c, base + p))

    def zfix(idx, width):
        return pl.BlockSpec((None, R, width), lambda b, p, c: (b, c, idx))

    def mucol(base):
        return pl.BlockSpec((1, WB), lambda b, p, c: (0, base + p))

    def mufix(idx, width):
        return pl.BlockSpec((1, width), lambda b, p, c: (0, idx))

    pvec = pl.BlockSpec((1, WB), lambda b, p, c: (0, p))

    def lora(rank):
        return pl.BlockSpec((rank, WB), lambda b, p, c: (0, p))

    row = lambda a: a.reshape(1, -1)
    return pl.pallas_call(
        functools.partial(_rwkv_kernel, L=L),
        grid=(B, ngrp, T // R),
        in_specs=[zcol(0), zcol(ngrp), zcol(2 * ngrp),
                  zfix(c_gl, GATE_LORA_PAD), zfix(c_wl, W), zfix(c_al, W),
                  mucol(0), mucol(ngrp), mucol(2 * ngrp),
                  mufix(c_gl, GATE_LORA_PAD), mufix(c_wl, W), mufix(c_al, W),
                  pvec, pvec, pvec, pvec, pvec, pvec, pvec,
                  lora(DECAY_LORA), lora(ICLR_LORA), lora(GATE_LORA_PAD)],
        out_specs=pl.BlockSpec((None, R, WB), lambda b, p, c: (b, c, p)),
        out_shape=jax.ShapeDtypeStruct((B, T, RW), BF16),
        scratch_shapes=[pltpu.VMEM((RWKV_BLOCK_PAIRS, W, W), F32)] + [pltpu.VMEM((8, WB), F32)] * 3
        + [pltpu.VMEM((8, GATE_LORA_PAD), F32), pltpu.VMEM((8, W), F32), pltpu.VMEM((8, W), F32)],
        compiler_params=_cparams(("parallel", "parallel", "arbitrary")),
        name="rwkv7",
    )(z, z, z, z, z, z, mu_p, mu_p, mu_p, mu_p, mu_p, mu_p,
      row(w0), row(a0), row(k_k), row(k_a), row(r_k), row(lnx_g), row(lnx_b), w2, a2, g2p)


def _pad_cols(w, n):
    return jnp.pad(w, ((0, 0), (0, n - w.shape[1])))


def kernel(x, c, ada_w, ada_b, ada_table, ln_g, ln_b, mix_w_in, mix_mu, rwkv_w0, rwkv_w2, rwkv_a0, rwkv_a2,
           rwkv_g2, rwkv_kk, rwkv_ka, rwkv_rk, rwkv_lnx_g, rwkv_lnx_b, mix_w_out, ffn_w_gate, ffn_w_up,
           ffn_w_down, pool_w, pool_scale, moe_router_w, moe_router_b, moe_w_gate, moe_w_up, moe_w_down):
    B, T, D = x.shape
    M = B * T
    depth = ada_table.shape[0]
    RW = rwkv_w0.shape[1]
    MW = D - RW
    n_moba = MW // MOBA_HEAD_DIM
    gate_lora = rwkv_g2.shape[1]
    alpha = (2 * depth) ** 0.25

    mods = ada_mod(c, ada_w, ada_b, ada_table)
    xf = x.reshape(M, D)
    h = modulate(xf, mods[0, :, 1], mods[0, :, 0], T)
    lnrow = lambda a: a.reshape(1, D)

    za = 3 * RW + GATE_LORA_PAD + DECAY_LORA + ICLR_LORA
    n_in = za + 3 * MW
    n_pad = -(-n_in // 1024) * 1024
    o_wl = 3 * RW
    o_al = o_wl + DECAY_LORA
    o_gl = o_al + ICLR_LORA
    o_q = o_gl + gate_lora

    def permute_cols(w):
        return jnp.concatenate(
            [w[..., :3 * RW], _pad_cols(w[..., o_gl:o_q], GATE_LORA_PAD), w[..., o_wl:o_al], w[..., o_al:o_gl],
             _pad_cols(w[..., o_q:], n_pad - za)], axis=-1)

    for l in range(depth):
        i = l // 2
        m = mods[l]
        g1, sh2, sc2, g2 = m[:, 2], m[:, 3], m[:, 4], m[:, 5]
        if l + 1 < depth:
            sh_n, sc_n = mods[l + 1, :, 0], mods[l + 1, :, 1]
        else:
            sh_n, sc_n = jnp.zeros_like(sh2), jnp.zeros_like(sc2)
        if l % 2 == 0:
            w_in = permute_cols(mix_w_in[i]).astype(BF16)
            mu_p = permute_cols(jnp.pad(mix_mu[i], (0, mix_w_in.shape[2] - mix_mu.shape[1]))[None, :])
            z = matmul(h, w_in, 512, 1024).reshape(B, T, n_pad)
            g2p = jnp.pad(rwkv_g2[i], ((0, GATE_LORA_PAD - gate_lora), (0, 0)))
            ya = rwkv_time_mix(z, mu_p, rwkv_w0[i], rwkv_w2[i], rwkv_a0[i], rwkv_a2[i], g2p, rwkv_kk[i],
                               rwkv_ka[i], rwkv_rk[i], rwkv_lnx_g[i], rwkv_lnx_b[i], B, T, RW)
            yb = moba_attention(z, za // LANES, n_moba, B, T)
            ycat = jnp.concatenate([ya, yb], axis=-1).reshape(M, D)
            xf, h = matmul_resid_ln(ycat, cast_pad_bf16(mix_w_out, i), xf, g1, lnrow(ln_g[l, 0]), lnrow(ln_b[l, 0]),
                                    sc2, sh2, T, alpha, 512, 1024)
            F = ffn_w_gate.shape[2]
            fp = -(-F // 1024) * 1024
            act = swiglu_up(h, cast_pad_bf16(ffn_w_gate, i, cols_p=fp), cast_pad_bf16(ffn_w_up, i, cols_p=fp),
                            1024, 512)
            wd = cast_pad_bf16(ffn_w_down, i, rows_p=fp)
            xf, h = matmul_resid_ln(act, wd, xf, g2, lnrow(ln_g[l, 1]), lnrow(ln_b[l, 1]), sc_n, sh_n, T, alpha,
                                    512, 1024)
        else:
            xf, h = pool_mixer_ln(xf, m[:, 1], m[:, 0], pool_w[i].astype(BF16), lnrow(pool_scale[i]), g1,
                                  lnrow(ln_g[l, 0]), lnrow(ln_b[l, 0]), sc2, sh2, T, alpha)
            E, _, NE = moe_w_gate.shape[1:]
            idx, gates = moe_router(h, moe_router_w[i], moe_router_b[i])
            row_token, tile_e, n_used, pos0, pos1 = moe_group_metadata(idx, E, MOE_TILE)
            xs = moe_gather_rows(h, row_token, MOE_TILE)
            cast3 = lambda w, bc: cast_pad_bf16(w.reshape(w.shape[0], -1, w.shape[-1]), i, bc=bc).reshape(w.shape[1:])
            act = moe_up_grouped(xs, tile_e, n_used, cast3(moe_w_gate, NE // 2), cast3(moe_w_up, NE // 2),
                                 MOE_TILE, NE // 2)
            ys = moe_down_grouped(act, tile_e, n_used, cast3(moe_w_down, 1024), MOE_TILE)
            xf, h = moe_combine_ln(ys, pos0, pos1, gates, xf, g2, lnrow(ln_g[l, 1]), lnrow(ln_b[l, 1]), sc_n, sh_n,
                                   T, alpha)
    return xf.reshape(B, T, D)
```

```python
import functools
import math

import jax
import jax.numpy as jnp
from jax import lax
from jax.experimental import pallas as pl
from jax.experimental.pallas import tpu as pltpu

F32 = jnp.float32
BF16 = jnp.bfloat16
HI = lax.Precision.HIGHEST

RWKV_HEAD_DIM = 64
DECAY_LORA = 128
ICLR_LORA = 128
GATE_LORA_PAD = 512
RWKV_LN_EPS = 64e-5
MOBA_HEAD_DIM = 128
MOBA_BLOCK = 256
MOBA_TOPK = 3
ROPE_THETA = 10000.0
POOL_WINDOWS = (2, 4, 8, 16)
POOL_HALO = 16
MOE_TOPK = 2
N_MOD = 6
LN_EPS = 1e-5
NEG_INF = -1e30

LANES = 128
LN_ROWS = 64
MOE_TILE = 256
DMA_UNROLL = 8
RWKV_CHUNK = 64
RWKV_BLOCK_CHUNKS = 4
RWKV_BLOCK_PAIRS = 4
P_LORA_W, P_LORA, P_SCORE, P_TRI, P_MAIN, P_STATE = 1, 1, 1, 1, 1, 1
VMEM_LIMIT = 56 * 1024 * 1024
VMEM_LIMIT_DOWN = 60 * 1024 * 1024


def _cparams(sem, vmem_limit=VMEM_LIMIT):
    return pltpu.CompilerParams(dimension_semantics=sem, vmem_limit_bytes=vmem_limit)


def _dot(a, b, precision=None):
    return jnp.dot(a, b, preferred_element_type=F32, precision=precision)


def _dot_nt(a, b, precision=None):
    return lax.dot_general(a, b, (((1,), (1,)), ((), ())), preferred_element_type=F32, precision=precision)


def _dot_tn(a, b, precision=None):
    return lax.dot_general(a, b, (((0,), (0,)), ((), ())), preferred_element_type=F32, precision=precision)


def _ada_kernel(c_ref, w_ref, b_ref, tab_ref, o_ref):
    c = c_ref[...]
    sc = (c * jax.nn.sigmoid(c)).astype(BF16)
    acc = _dot(sc, w_ref[...].astype(BF16)) + b_ref[...]
    o_ref[...] = acc[None, :, :] + tab_ref[...][:, None, :]


def ada_mod(c, ada_w, ada_b, ada_table):
    B, D = c.shape
    N = ada_w.shape[1]
    depth = ada_table.shape[0]
    BP = 8
    cp = jnp.zeros((BP, D), F32).at[:B].set(c)
    tn = 1024
    out = pl.pallas_call(
        _ada_kernel,
        grid=(N // tn,),
        in_specs=[
            pl.BlockSpec((BP, D), lambda j: (0, 0)),
            pl.BlockSpec((D, tn), lambda j: (0, j)),
            pl.BlockSpec((1, tn), lambda j: (0, j)),
            pl.BlockSpec((depth, tn), lambda j: (0, j)),
        ],
        out_specs=pl.BlockSpec((depth, BP, tn), lambda j: (0, 0, j)),
        out_shape=jax.ShapeDtypeStruct((depth, BP, N), F32),
        compiler_params=_cparams(("parallel",)),
        name="ada_mod",
    )(cp, ada_w, ada_b.reshape(1, N), ada_table.reshape(depth, N))
    return out[:, :B].reshape(depth, B, N_MOD, 1, D)


def _cast_pad_kernel(x_ref, o_ref, *, rows, cols):
    br, bc = o_ref.shape
    i, j = pl.program_id(0), pl.program_id(1)
    inside = ((i + 1) * br <= rows) & ((j + 1) * bc <= cols)

    @pl.when(inside)
    def _():
        o_ref[...] = x_ref[...].astype(o_ref.dtype)

    @pl.when(jnp.logical_not(inside))
    def _():
        r = i * br + lax.broadcasted_iota(jnp.int32, (br, bc), 0)
        c = j * bc + lax.broadcasted_iota(jnp.int32, (br, bc), 1)
        o_ref[...] = jnp.where((r < rows) & (c < cols), x_ref[...], 0.0).astype(o_ref.dtype)


def cast_pad_bf16(w_stack, layer, rows_p=None, cols_p=None, br=1024, bc=1024):
    _, rows, cols = w_stack.shape
    rows_p = rows_p or rows
    cols_p = cols_p or cols
    bc = min(bc, cols_p)
    last_r = (rows - 1) // br
    last_c = (cols - 1) // bc
    return pl.pallas_call(
        functools.partial(_cast_pad_kernel, rows=rows, cols=cols),
        grid=(rows_p // br, cols_p // bc),
        in_specs=[pl.BlockSpec((None, br, bc),
                               lambda i, j: (layer, jnp.minimum(i, last_r), jnp.minimum(j, last_c)))],
        out_specs=pl.BlockSpec((br, bc), lambda i, j: (i, j)),
        out_shape=jax.ShapeDtypeStruct((rows_p, cols_p), BF16),
        compiler_params=_cparams(("parallel", "parallel")),
        name="cast_pad",
    )(w_stack)


def _modulate_kernel(x_ref, sc_ref, sh_ref, o_ref):
    o_ref[...] = (x_ref[...] * (1.0 + sc_ref[...]) + sh_ref[...]).astype(o_ref.dtype)


def modulate(x, sc, sh, T):
    M, D = x.shape
    tm = 512
    tpb = T // tm
    vec = pl.BlockSpec((None, 1, D), lambda i: (i // tpb, 0, 0))
    return pl.pallas_call(
        _modulate_kernel,
        grid=(M // tm,),
        in_specs=[pl.BlockSpec((tm, D), lambda i: (i, 0)), vec, vec],
        out_specs=pl.BlockSpec((tm, D), lambda i: (i, 0)),
        out_shape=jax.ShapeDtypeStruct((M, D), BF16),
        compiler_params=_cparams(("parallel",)),
        name="modulate",
    )(x, sc, sh)


def _mm_kernel(a_ref, w_ref, o_ref):
    o_ref[...] = _dot(a_ref[...], w_ref[...]).astype(o_ref.dtype)


def matmul(a, w, tm, tn, out_dtype=F32):
    M, K = a.shape
    N = w.shape[1]
    return pl.pallas_call(
        _mm_kernel,
        grid=(N // tn, M // tm),
        in_specs=[pl.BlockSpec((tm, K), lambda j, i: (i, 0)),
                  pl.BlockSpec((K, tn), lambda j, i: (0, j))],
        out_specs=pl.BlockSpec((tm, tn), lambda j, i: (i, j)),
        out_shape=jax.ShapeDtypeStruct((M, N), out_dtype),
        compiler_params=_cparams(("parallel", "parallel")),
        name="matmul",
    )(a, w)


def _resid_ln_mod(x, y, gate, lng, lnb, sc, sh, alpha):
    r = alpha * x + gate * y
    mu = jnp.mean(r, axis=-1, keepdims=True)
    d = r - mu
    var = jnp.mean(d * d, axis=-1, keepdims=True)
    xn = d * lax.rsqrt(var + LN_EPS) * lng + lnb
    return xn, xn * (1.0 + sc) + sh


def _mm_ln_kernel(a_ref, w_ref, x_ref, gate_ref, lng_ref, lnb_ref, sc_ref, sh_ref,
                  xo_ref, ho_ref, *, alpha):
    k = pl.program_id(1)

    @pl.when(k == 0)
    def _():
        xo_ref[...] = _dot(a_ref[...], w_ref[...])

    @pl.when(k > 0)
    def _():
        xo_ref[...] += _dot(a_ref[...], w_ref[...])

    @pl.when(k == pl.num_programs(1) - 1)
    def _():
        for r0 in range(0, xo_ref.shape[0], LN_ROWS):
            rows = pl.ds(r0, LN_ROWS)
            xn, h = _resid_ln_mod(x_ref[rows, :], xo_ref[rows, :], gate_ref[...], lng_ref[...], lnb_ref[...],
                                  sc_ref[...], sh_ref[...], alpha)
            xo_ref[rows, :] = xn
            ho_ref[rows, :] = h.astype(ho_ref.dtype)


def matmul_resid_ln(a, w, x, gate, lng, lnb, sc, sh, T, alpha, tm, tk):
    M, K = a.shape
    D = w.shape[1]
    tpb = T // tm
    vec_b = pl.BlockSpec((None, 1, D), lambda i, k: (i // tpb, 0, 0))
    vec = pl.BlockSpec((1, D), lambda i, k: (0, 0))
    row = pl.BlockSpec((tm, D), lambda i, k: (i, 0))
    return pl.pallas_call(
        functools.partial(_mm_ln_kernel, alpha=alpha),
        grid=(M // tm, K // tk),
        in_specs=[pl.BlockSpec((tm, tk), lambda i, k: (i, k)),
                  pl.BlockSpec((tk, D), lambda i, k: (k, 0)),
                  pl.BlockSpec((tm, D), lambda i, k: (i, 0), pipeline_mode=pl.Buffered(1)),
                  vec_b, vec, vec, vec_b, vec_b],
        out_specs=[row, row],
        out_shape=[jax.ShapeDtypeStruct((M, D), F32), jax.ShapeDtypeStruct((M, D), BF16)],
        compiler_params=_cparams(("parallel", "arbitrary"), VMEM_LIMIT_DOWN),
        name="matmul_resid_ln",
    )(a, w, x, gate, lng, lnb, sc, sh)


def _swiglu_up_kernel(a_ref, wg_ref, wu_ref, o_ref):
    a = a_ref[...]
    g = _dot(a, wg_ref[...])
    u = _dot(a, wu_ref[...])
    o_ref[...] = (g * jax.nn.sigmoid(g) * u).astype(o_ref.dtype)


def swiglu_up(a, wg, wu, tm, tn):
    M, K = a.shape
    N = wg.shape[1]
    wspec = pl.BlockSpec((K, tn), lambda i, j: (0, j))
    return pl.pallas_call(
        _swiglu_up_kernel,
        grid=(M // tm, N // tn),
        in_specs=[pl.BlockSpec((tm, K), lambda i, j: (i, 0)), wspec, wspec],
        out_specs=pl.BlockSpec((tm, tn), lambda i, j: (i, j)),
        out_shape=jax.ShapeDtypeStruct((M, N), BF16),
        compiler_params=_cparams(("parallel", "parallel")),
        name="swiglu_up",
    )(a, wg, wu)


def _router_kernel(h_ref, rw_ref, rb_ref, idx_ref, gate_ref, *, n_experts):
    logits = _dot(h_ref[...], rw_ref[...], HI) + rb_ref[...]
    lane = lax.broadcasted_iota(jnp.int32, logits.shape, 1)
    logits = jnp.where(lane < n_experts, logits, -jnp.inf)
    m1 = jnp.max(logits, axis=-1, keepdims=True)
    i1 = jnp.min(jnp.where(logits == m1, lane, LANES), axis=-1, keepdims=True)
    oh1 = lane == i1
    rest = jnp.where(oh1, -jnp.inf, logits)
    m2 = jnp.max(rest, axis=-1, keepdims=True)
    i2 = jnp.min(jnp.where(rest == m2, lane, LANES), axis=-1, keepdims=True)
    e = jnp.exp(m2 - m1)
    g1 = 1.0 / (1.0 + e)
    g2 = e / (1.0 + e)
    idx_ref[...] = jnp.where(lane == 0, i1, jnp.where(lane == 1, i2, 0))
    gate_ref[...] = jnp.where(lane == 0, g1, jnp.where(lane == 1, g2, 0.0))


def moe_router(h, router_w, router_b):
    M, D = h.shape
    E = router_w.shape[1]
    tm = 512
    rw = jnp.zeros((D, LANES), F32).at[:, :E].set(router_w)
    rb = jnp.zeros((1, LANES), F32).at[0, :E].set(router_b)
    out = pl.BlockSpec((tm, LANES), lambda i: (i, 0))
    return pl.pallas_call(
        functools.partial(_router_kernel, n_experts=E),
        grid=(M // tm,),
        in_specs=[pl.BlockSpec((tm, D), lambda i: (i, 0)),
                  pl.BlockSpec((D, LANES), lambda i: (0, 0)),
                  pl.BlockSpec((1, LANES), lambda i: (0, 0))],
        out_specs=[out, out],
        out_shape=[jax.ShapeDtypeStruct((M, LANES), jnp.int32), jax.ShapeDtypeStruct((M, LANES), F32)],
        compiler_params=_cparams(("parallel",)),
        name="moe_router",
    )(h, rw, rb)


def _gather_rows_kernel(tok_ref, h_hbm, o_ref, buf_ref, sem):
    i = pl.program_id(0)
    tm = o_ref.shape[0]
    slot = i % 2

    def start_tile(tile, s):
        def body(r0, c):
            for u in range(DMA_UNROLL):
                r = r0 * DMA_UNROLL + u
                pltpu.make_async_copy(h_hbm.at[pl.ds(tok_ref[tile * tm + r], 1), :], buf_ref.at[s, pl.ds(r, 1), :],
                                      sem.at[s]).start()
            return c

        lax.fori_loop(0, tm // DMA_UNROLL, body, 0)

    @pl.when(i == 0)
    def _():
        start_tile(i, slot)

    @pl.when(i + 1 < pl.num_programs(0))
    def _():
        start_tile(i + 1, 1 - slot)

    pltpu.make_async_copy(h_hbm.at[pl.ds(0, tm), :], buf_ref.at[slot], sem.at[slot]).wait()
    o_ref[...] = buf_ref[slot].astype(o_ref.dtype)


def moe_gather_rows(h, row_token, tm):
    D = h.shape[1]
    NR = row_token.shape[0]
    return pl.pallas_call(
        _gather_rows_kernel,
        grid_spec=pltpu.PrefetchScalarGridSpec(
            num_scalar_prefetch=1,
            grid=(NR // tm,),
            in_specs=[pl.BlockSpec(memory_space=pl.ANY)],
            out_specs=pl.BlockSpec((tm, D), lambda i, tok: (i, 0)),
            scratch_shapes=[pltpu.VMEM((2, tm, D), F32), pltpu.SemaphoreType.DMA((2,))]),
        out_shape=jax.ShapeDtypeStruct((NR, D), BF16),
        compiler_params=_cparams(("arbitrary",)),
        name="moe_gather",
    )(row_token, h)


def _moe_up_kernel(te_ref, nu_ref, a_ref, wg_ref, wu_ref, o_ref):
    i = pl.program_id(1)

    @pl.when(i < nu_ref[0])
    def _():
        a = a_ref[...]
        g = _dot(a, wg_ref[...])
        u = _dot(a, wu_ref[...])
        o_ref[...] = (g * jax.nn.sigmoid(g) * u).astype(o_ref.dtype)

    @pl.when(i >= nu_ref[0])
    def _():
        o_ref[...] = jnp.zeros_like(o_ref)


def moe_up_grouped(xs, tile_e, n_used, wg, wu, tm, tn):
    NR, K = xs.shape
    E, _, NE = wg.shape
    wspec = pl.BlockSpec((None, K, tn), lambda j, i, te, nu: (te[i], 0, j))
    return pl.pallas_call(
        _moe_up_kernel,
        grid_spec=pltpu.PrefetchScalarGridSpec(
            num_scalar_prefetch=2,
            grid=(NE // tn, NR // tm),
            in_specs=[pl.BlockSpec((tm, K), lambda j, i, te, nu: (i, 0)), wspec, wspec],
            out_specs=pl.BlockSpec((tm, tn), lambda j, i, te, nu: (i, j))),
        out_shape=jax.ShapeDtypeStruct((NR, NE), BF16),
        compiler_params=_cparams(("parallel", "parallel")),
        name="moe_up",
    )(tile_e, n_used, xs, wg, wu)


def _moe_down_kernel(te_ref, nu_ref, a_ref, w_ref, o_ref):
    i = pl.program_id(0)

    @pl.when(i < nu_ref[0])
    def _():
        o_ref[...] = _dot(a_ref[...], w_ref[...])

    @pl.when(i >= nu_ref[0])
    def _():
        o_ref[...] = jnp.zeros_like(o_ref)


def moe_down_grouped(act, tile_e, n_used, wd, tm):
    NR, NE = act.shape
    D = wd.shape[2]
    return pl.pallas_call(
        _moe_down_kernel,
        grid_spec=pltpu.PrefetchScalarGridSpec(
            num_scalar_prefetch=2,
            grid=(NR // tm,),
            in_specs=[pl.BlockSpec((tm, NE), lambda i, te, nu: (i, 0)),
                      pl.BlockSpec((None, NE, D), lambda i, te, nu: (te[i], 0, 0))],
            out_specs=pl.BlockSpec((tm, D), lambda i, te, nu: (i, 0))),
        out_shape=jax.ShapeDtypeStruct((NR, D), F32),
        compiler_params=_cparams(("parallel",)),
        name="moe_down",
    )(tile_e, n_used, act, wd)


def _moe_combine_ln_kernel(p0_ref, p1_ref, ys_hbm, x_ref, rg_ref, gate_ref, lng_ref, lnb_ref, sc_ref, sh_ref,
                           xo_ref, ho_ref, b0_ref, b1_ref, sem, *, alpha):
    i = pl.program_id(0)
    tm = x_ref.shape[0]
    slot = i % 2

    def start_tile(tile, s):
        def body(r0, c):
            for u in range(DMA_UNROLL):
                r = r0 * DMA_UNROLL + u
                t = tile * tm + r
                pltpu.make_async_copy(ys_hbm.at[pl.ds(p0_ref[t], 1), :], b0_ref.at[s, pl.ds(r, 1), :],
                                      sem.at[0, s]).start()
                pltpu.make_async_copy(ys_hbm.at[pl.ds(p1_ref[t], 1), :], b1_ref.at[s, pl.ds(r, 1), :],
                                      sem.at[1, s]).start()
            return c

        lax.fori_loop(0, tm // DMA_UNROLL, body, 0)

    @pl.when(i == 0)
    def _():
        start_tile(i, slot)

    @pl.when(i + 1 < pl.num_programs(0))
    def _():
        start_tile(i + 1, 1 - slot)

    pltpu.make_async_copy(ys_hbm.at[pl.ds(0, tm), :], b0_ref.at[slot], sem.at[0, slot]).wait()
    pltpu.make_async_copy(ys_hbm.at[pl.ds(0, tm), :], b1_ref.at[slot], sem.at[1, slot]).wait()
    for r0 in range(0, tm, LN_ROWS):
        rows = pl.ds(r0, LN_ROWS)
        rg = rg_ref[rows, :]
        y = rg[:, 0:1] * b0_ref[slot, rows, :] + rg[:, 1:2] * b1_ref[slot, rows, :]
        xn, h = _resid_ln_mod(x_ref[rows, :], y, gate_ref[...], lng_ref[...], lnb_ref[...],
                              sc_ref[...], sh_ref[...], alpha)
        xo_ref[rows, :] = xn
        ho_ref[rows, :] = h.astype(ho_ref.dtype)


def moe_combine_ln(ys, pos0, pos1, rgates, x, gate, lng, lnb, sc, sh, T, alpha):
    M, D = x.shape
    tm = 256
    tpb = T // tm
    vec_b = pl.BlockSpec((None, 1, D), lambda i, p0, p1: (i // tpb, 0, 0))
    vec = pl.BlockSpec((1, D), lambda i, p0, p1: (0, 0))
    row = pl.BlockSpec((tm, D), lambda i, p0, p1: (i, 0))
    return pl.pallas_call(
        functools.partial(_moe_combine_ln_kernel, alpha=alpha),
        grid_spec=pltpu.PrefetchScalarGridSpec(
            num_scalar_prefetch=2,
            grid=(M // tm,),
            in_specs=[pl.BlockSpec(memory_space=pl.ANY), row,
                      pl.BlockSpec((tm, LANES), lambda i, p0, p1: (i, 0)), vec_b, vec, vec, vec_b, vec_b],
            out_specs=[row, row],
            scratch_shapes=[pltpu.VMEM((2, tm, D), F32), pltpu.VMEM((2, tm, D), F32),
                            pltpu.SemaphoreType.DMA((2, 2))]),
        out_shape=[jax.ShapeDtypeStruct((M, D), F32), jax.ShapeDtypeStruct((M, D), BF16)],
        compiler_params=_cparams(("arbitrary",)),
        name="moe_combine_ln",
    )(pos0, pos1, ys, x, rgates, gate, lng, lnb, sc, sh)


def moe_group_metadata(idx, n_experts, tm):
    M = idx.shape[0]
    n_pairs = MOE_TOPK * M
    e_flat = idx[:, :MOE_TOPK].reshape(-1)
    onehot = (e_flat[:, None] == jnp.arange(n_experts)[None, :]).astype(F32)
    blk = LANES
    oh3 = onehot.reshape(n_pairs // blk, blk, n_experts)
    tri = (jnp.arange(blk)[:, None] > jnp.arange(blk)[None, :]).astype(F32)
    within = jnp.einsum("ts,bse->bte", tri, oh3, precision=HI)
    tot = oh3.sum(axis=1)
    before = jnp.cumsum(tot, axis=0) - tot
    excl = (within + before[:, None, :]).reshape(n_pairs, n_experts)
    rank = jnp.sum(excl * onehot, axis=1).astype(jnp.int32)
    counts = tot.sum(axis=0).astype(jnp.int32)
    padded = -(-counts // tm) * tm
    ends = jnp.cumsum(padded)
    starts = ends - padded
    pos = jnp.sum(onehot * starts[None, :].astype(F32), axis=1).astype(jnp.int32) + rank
    n_rows = n_pairs + n_experts * tm
    row_token = jnp.zeros((n_rows,), jnp.int32).at[pos].set(jnp.arange(n_pairs, dtype=jnp.int32) // MOE_TOPK)
    tile_start = jnp.arange(n_rows // tm, dtype=jnp.int32) * tm
    tile_e = jnp.minimum(jnp.sum(tile_start[:, None] >= ends[None, :], axis=1), n_experts - 1).astype(jnp.int32)
    n_used = (ends[-1:] // tm).astype(jnp.int32)
    pos = pos.reshape(M, MOE_TOPK)
    return row_token, tile_e, n_used, pos[:, 0], pos[:, 1]


def _pool_kernel(x_ref, xh_ref, sc_ref, sh_ref, w_ref, ps_ref, gate_ref, lng_ref, lnb_ref,
                 sc2_ref, sh2_ref, xo_ref, ho_ref, hh_ref, y_ref, *, tm, tpb, alpha):
    i = pl.program_id(0)
    D = x_ref.shape[1]
    dg = D // len(POOL_WINDOWS)
    sc = sc_ref[...]
    sh = sh_ref[...]
    x = x_ref[...]
    first = (i % tpb) == 0
    halo = xh_ref[...] * (1.0 + sc) + sh
    hh_ref[0:POOL_HALO, :] = jnp.where(first, 0.0, halo)
    hh_ref[POOL_HALO:, :] = x * (1.0 + sc) + sh
    t = (i % tpb) * tm + lax.broadcasted_iota(jnp.int32, (tm, 1), 0)
    tcnt = (t + 1).astype(F32)
    for g, win in enumerate(POOL_WINDOWS):
        cols = slice(g * dg, (g + 1) * dg)
        h = hh_ref[POOL_HALO:, cols]
        s = h
        for d in range(1, win):
            s = s + hh_ref[pl.ds(POOL_HALO - d, tm), cols]
        inv = 1.0 / jnp.minimum(tcnt, float(win))
        p = (s * inv - h).astype(BF16)
        y_ref[:, cols] = _dot(p, w_ref[g])
    y = y_ref[...] * ps_ref[...]
    xn, hn = _resid_ln_mod(x, y, gate_ref[...], lng_ref[...], lnb_ref[...], sc2_ref[...], sh2_ref[...], alpha)
    xo_ref[...] = xn
    ho_ref[...] = hn.astype(ho_ref.dtype)


def pool_mixer_ln(x, sc, sh, pool_w, pool_scale, gate, lng, lnb, sc2, sh2, T, alpha):
    M, D = x.shape
    G, dg, _ = pool_w.shape
    tm = 256
    tpb = T // tm
    hb = tm // POOL_HALO
    vec_b = pl.BlockSpec((None, 1, D), lambda i: (i // tpb, 0, 0))
    vec = pl.BlockSpec((1, D), lambda i: (0, 0))
    row = pl.BlockSpec((tm, D), lambda i: (i, 0))
    return pl.pallas_call(
        functools.partial(_pool_kernel, tm=tm, tpb=tpb, alpha=alpha),
        grid=(M // tm,),
        in_specs=[row,
                  pl.BlockSpec((POOL_HALO, D), lambda i: (jnp.maximum(i * hb - 1, 0), 0)),
                  vec_b, vec_b,
                  pl.BlockSpec((G, dg, dg), lambda i: (0, 0, 0)),
                  vec, vec_b, vec, vec, vec_b, vec_b],
        out_specs=[row, row],
        out_shape=[jax.ShapeDtypeStruct((M, D), F32), jax.ShapeDtypeStruct((M, D), F32)],
        scratch_shapes=[pltpu.VMEM((tm + POOL_HALO, D), F32), pltpu.VMEM((tm, D), F32)],
        compiler_params=_cparams(("parallel",)),
        name="pool_mixer",
    )(x, x, sc, sh, pool_w, pool_scale, gate, lng, lnb, sc2, sh2)


def _moba_kernel(q_ref, k_ref, v_ref, cos_ref, sin_ref, o_ref, *, nb):
    bs = MOBA_BLOCK
    cos = cos_ref[...]
    sin = sin_ref[...]

    def rope(t):
        return t * cos + pltpu.roll(t, MOBA_HEAD_DIM // 2, axis=1) * sin

    qr = rope(q_ref[...])
    kr = rope(k_ref[...])
    T = qr.shape[0]
    prow = lax.broadcasted_iota(jnp.int32, (LANES, T), 0)
    pcol = lax.broadcasted_iota(jnp.int32, (LANES, T), 1)
    pool = jnp.where(pcol // bs == prow, 1.0 / bs, 0.0).astype(F32)
    kmean = _dot(pool, kr, HI)
    qb = qr.astype(BF16)
    kb = kr.astype(BF16)
    vb = v_ref[...].astype(BF16)
    scale = MOBA_HEAD_DIM ** -0.5
    lane = lax.broadcasted_iota(jnp.int32, (bs, LANES), 1)
    rowi = lax.broadcasted_iota(jnp.int32, (bs, bs), 0)
    coli = lax.broadcasted_iota(jnp.int32, (bs, bs), 1)
    for i in range(nb):
        rows = slice(i * bs, (i + 1) * bs)
        qi = qb[rows]
        s_list = [jnp.where(coli <= rowi, _dot_nt(qi, kb[rows]) * scale, NEG_INF)]
        if i > 0:
            gate = _dot_nt(qr[rows], kmean, HI)
            valid = lane < i
            gate = jnp.where(valid, gate, -jnp.inf)
            rank = jnp.zeros((bs, LANES), jnp.int32)
            for j in range(i):
                gj = gate[:, j:j + 1]
                ahead = (gj > gate) | ((gj == gate) & (j < lane))
                rank = rank + ahead.astype(jnp.int32)
            sel = valid & (rank < MOBA_TOPK)
            for j in range(i):
                sj = _dot_nt(qi, kb[j * bs:(j + 1) * bs]) * scale
                s_list.append(jnp.where(sel[:, j:j + 1], sj, NEG_INF))
        smax = s_list[0]
        for s in s_list[1:]:
            smax = jnp.maximum(smax, s)
        m = smax.max(axis=-1, keepdims=True)
        psum = jnp.zeros((bs, bs), F32)
        acc = jnp.zeros((bs, MOBA_HEAD_DIM), F32)
        for jj, s in enumerate(s_list):
            j = i if jj == 0 else jj - 1
            p = jnp.exp(s - m)
            psum = psum + p
            acc = acc + _dot(p.astype(BF16), vb[j * bs:(j + 1) * bs])
        l = psum.sum(axis=-1, keepdims=True)
        o_ref[rows, :] = (acc / l).astype(o_ref.dtype)


def moba_attention(z, q_col, n_heads, B, T):
    assert T % MOBA_BLOCK == 0
    nb = T // MOBA_BLOCK
    dh = MOBA_HEAD_DIM
    inv = jnp.power(ROPE_THETA, -jnp.arange(0, dh, 2, dtype=F32) / dh)
    ang = jnp.arange(T, dtype=F32)[:, None] * inv[None, :]
    cos = jnp.concatenate([jnp.cos(ang), jnp.cos(ang)], axis=-1)
    sin = jnp.concatenate([-jnp.sin(ang), jnp.sin(ang)], axis=-1)

    def col(off):
        return pl.BlockSpec((None, T, dh), lambda b, h: (b, 0, q_col + off * n_heads + h))

    tab = pl.BlockSpec((T, dh), lambda b, h: (0, 0))
    return pl.pallas_call(
        functools.partial(_moba_kernel, nb=nb),
        grid=(B, n_heads),
        in_specs=[col(0), col(1), col(2), tab, tab],
        out_specs=pl.BlockSpec((None, T, dh), lambda b, h: (b, 0, h)),
        out_shape=jax.ShapeDtypeStruct((B, T, n_heads * dh), BF16),
        compiler_params=_cparams(("parallel", "parallel")),
        name="moba",
    )(z, z, z, cos, sin)


def _split_bf16(x, n):
    parts, rem = [], x
    for i in range(n):
        p = rem.astype(BF16)
        parts.append(p)
        if i + 1 < n:
            rem = rem - p.astype(F32)
    return parts


_DIMS = {"nn": ((1,), (0,)), "nt": ((1,), (1,)), "tn": ((0,), (0,))}


def _mm(a, b, form="nn", passes=1):
    if passes == 1:
        a3, b3 = a.astype(BF16), b.astype(BF16)
    else:
        ah, al = _split_bf16(a, 2)
        bh, bl = _split_bf16(b, 2)
        a3 = jnp.concatenate([ah, ah, al], axis=0 if form == "tn" else 1)
        b3 = jnp.concatenate([bh, bl, bh], axis=1 if form == "nt" else 0)
    return lax.dot_general(a3, b3, (_DIMS[form], ((), ())), preferred_element_type=F32)


def _mm_exact_rhs(a, b_bf16):
    a3 = jnp.concatenate(_split_bf16(a, 3), axis=1)
    b3 = jnp.concatenate([b_bf16, b_bf16, b_bf16], axis=0)
    return _dot(a3, b3)


def _rwkv_kernel(zr_ref, zk_ref, zv_ref, zgl_ref, zwl_ref, zal_ref,
                 mur_ref, muk_ref, muv_ref, mugl_ref, muwl_ref, mual_ref,
                 w0_ref, a0_ref, kk_ref, ka_ref, rk_ref, lng_ref, lnb_ref,
                 w2_ref, a2_ref, g2_ref,
                 o_ref,
                 s_ref, pr_ref, pk_ref, pv_ref, pgl_ref, pwl_ref, pal_ref, *, L):
    R = zr_ref.shape[0]
    hd = RWKV_HEAD_DIM
    c = pl.program_id(2)

    @pl.when(c == 0)
    def _():
        s_ref[...] = jnp.zeros_like(s_ref)
        for p in (pr_ref, pk_ref, pv_ref, pgl_ref, pwl_ref, pal_ref):
            p[...] = jnp.zeros_like(p)

    def shift(z_ref, p_ref, mu_ref):
        z = z_ref[...]
        row = lax.broadcasted_iota(jnp.int32, z.shape, 0)
        zp = jnp.where(row == 0, p_ref[0:1, :], pltpu.roll(z, 1, axis=0))
        p_ref[0:1, :] = z[R - 1:R, :]
        return z + (zp - z) * mu_ref[...]

    r = shift(zr_ref, pr_ref, mur_ref)
    k = shift(zk_ref, pk_ref, muk_ref)
    v = shift(zv_ref, pv_ref, muv_ref)
    gl = shift(zgl_ref, pgl_ref, mugl_ref)
    wl = shift(zwl_ref, pwl_ref, muwl_ref)
    al = shift(zal_ref, pal_ref, mual_ref)

    WB = zr_ref.shape[1]
    lane_r = lax.broadcasted_iota(jnp.int32, (WB, WB), 0)
    lane_c = lax.broadcasted_iota(jnp.int32, (WB, WB), 1)
    same_head_b = (lane_r // hd) == (lane_c // hd)
    ones_bd = jnp.where(same_head_b, 1.0, 0.0).astype(BF16)
    mean_bd = jnp.where(same_head_b, 1.0 / hd, 0.0).astype(BF16)
    same_head = same_head_b[0:LANES, 0:LANES]
    head0 = lax.broadcasted_iota(jnp.int32, (1, LANES), 1) < hd
    not_head0 = jnp.logical_not(head0)

    wpre = w0_ref[...] + _mm(jnp.tanh(wl), w2_ref[...], "nn", P_LORA_W)
    nw = -wpre
    softplus = jnp.maximum(nw, 0.0) + jnp.log(1.0 + jnp.exp(-jnp.abs(nw)))
    wlog = -softplus - 0.5
    a = jax.nn.sigmoid(a0_ref[...] + _mm(al, a2_ref[...], "nn", P_LORA))
    g = _mm(jax.nn.sigmoid(gl), g2_ref[...], "nn", P_LORA)
    kkv = k * kk_ref[...]
    kkn = kkv / jnp.maximum(jnp.sqrt(_mm_exact_rhs(kkv * kkv, ones_bd)), 1e-12)
    k2 = k * (1.0 + (a - 1.0) * ka_ref[...])
    ld = -jnp.exp(wlog)
    bv = kkn * a

    rr = lax.broadcasted_iota(jnp.int32, (R, R), 0)
    rc = lax.broadcasted_iota(jnp.int32, (R, R), 1)
    ltri = jnp.where((rr // L == rc // L) & (rr >= rc), 1.0, 0.0).astype(BF16)
    cum3 = _dot(ltri, jnp.concatenate(_split_bf16(ld, 3), axis=1))
    cum = cum3[:, 0:WB] + cum3[:, WB:2 * WB] + cum3[:, 2 * WB:3 * WB]

    S2 = 2 * L
    srow = lax.broadcasted_iota(jnp.int32, (S2, S2), 0)
    scol = lax.broadcasted_iota(jnp.int32, (S2, S2), 1)
    same_blk = (srow // L) == (scol // L)
    strict_bd = same_blk & (srow > scol)
    incl_bd = same_blk & (srow >= scol)
    tri_masks = []
    s = 1
    while s < L:
        tri_masks.append((srow // (2 * s) == scol // (2 * s)) & ((srow // s) % 2 == 1) & ((scol // s) % 2 == 0))
        s *= 2
    mid = L // 2 - 1
    zeros_l = jnp.zeros((L, LANES), F32)

    npb = WB // LANES
    nch = R // L
    streams = [(pp, j) for pp in range(npb) for j in range(nch)]
    eye = jnp.where(srow == scol, 1.0, 0.0)

    def piece(x, pp, j):
        return x[j * L:(j + 1) * L, pp * LANES:(pp + 1) * LANES]

    at_l, bt_l, kt_l, rt_l, v_l, rho_l, gam_l = [], [], [], [], [], [], []
    for pp, j in streams:
        cumj = piece(cum, pp, j)
        cmid = cumj[mid:mid + 1, :]
        cc = cumj - cmid
        e_neg = jnp.exp(-cc)
        at_l.append(-piece(kkn, pp, j) * jnp.exp(cc - piece(ld, pp, j)))
        bt_l.append(piece(bv, pp, j) * e_neg)
        kt_l.append(piece(k2, pp, j) * e_neg)
        rt_l.append(piece(r, pp, j) * jnp.exp(cc))
        v_l.append(piece(v, pp, j))
        rho_l.append(jnp.exp(cmid))
        gam_l.append(jnp.exp(cc[L - 1:L, :]))

    def stack_heads(x):
        return jnp.concatenate([jnp.where(head0, x, 0.0), jnp.where(not_head0, x, 0.0)], axis=0)

    ats_l = [stack_heads(x) for x in at_l]
    rts_l = [stack_heads(x) for x in rt_l]
    vs_l = [jnp.concatenate([x, x], axis=0) for x in v_l]
    sc_l = [_mm(jnp.concatenate([a_, r_], axis=0), jnp.concatenate([b_, b_, k_, k_], axis=0), "nt", P_SCORE)
            for a_, r_, b_, k_ in zip(ats_l, rts_l, bt_l, kt_l)]
    nab_l = [jnp.where(strict_bd, s_[0:S2, 0:S2], 0.0) for s_ in sc_l]
    nak_l = [jnp.where(strict_bd, s_[0:S2, S2:2 * S2], 0.0) for s_ in sc_l]
    mrb_l = [jnp.where(incl_bd, s_[S2:2 * S2, 0:S2], 0.0) for s_ in sc_l]
    mrk_l = [jnp.where(incl_bd, s_[S2:2 * S2, S2:2 * S2], 0.0) for s_ in sc_l]
    t_l = [eye + jnp.where(tri_masks[0], n_, 0.0) for n_ in nab_l]
    for m in tri_masks[1:]:
        ot_l = [_mm(jnp.where(m, n_, 0.0), t_, "nn", P_TRI) for n_, t_ in zip(nab_l, t_l)]
        t_l = [t_ + _mm(t_, ot_, "nn", P_TRI) for t_, ot_ in zip(t_l, ot_l)]
    nv_l = [_mm(n_, v_, "nn", P_MAIN) for n_, v_ in zip(nak_l, vs_l)]
    au_l = [_mm(t_, jnp.concatenate([a_, nv_], axis=1), "nn", P_MAIN) for t_, a_, nv_ in zip(t_l, ats_l, nv_l)]
    ry_l = [_mm(jnp.concatenate([mb_, mk_], axis=1),
                jnp.concatenate([au_, jnp.concatenate([jnp.zeros_like(v_), v_], axis=1)], axis=0), "nn", P_MAIN)
            for mb_, mk_, au_, v_ in zip(mrb_l, mrk_l, au_l, vs_l)]
    rbar_l = [rt_ + ry_[0:L, 0:LANES] + ry_[L:S2, 0:LANES] for rt_, ry_ in zip(rt_l, ry_l)]
    ybase_l = [jnp.where(head0, ry_[0:L, LANES:2 * LANES], ry_[L:S2, LANES:2 * LANES]) for ry_ in ry_l]
    gq_l = [_mm(jnp.concatenate(
                [jnp.concatenate([au_[0:L, 0:LANES] + au_[L:S2, 0:LANES],
                                  jnp.where(head0, au_[0:L, LANES:2 * LANES], au_[L:S2, LANES:2 * LANES])], axis=1),
                 jnp.concatenate([zeros_l, v_], axis=1)], axis=0),
                jnp.concatenate([b_, k_], axis=0), "tn", P_MAIN)
            for au_, v_, b_, k_ in zip(au_l, v_l, bt_l, kt_l)]
    gmat_l = [jnp.where(same_head, gq_[0:LANES], 0.0) for gq_ in gq_l]
    q0_l = [jnp.where(same_head, gq_[LANES:2 * LANES], 0.0) for gq_ in gq_l]
    st = [s_ref[pp] for pp in range(npb)]
    ys = [[None] * nch for _ in range(npb)]
    for j in range(nch):
        for pp in range(npb):
            i = pp * nch + j
            sr = st[pp] * rho_l[i]
            ys[pp][j] = _mm(rbar_l[i], sr, "nt", P_STATE) + ybase_l[i]
            st[pp] = (sr + _mm(sr, gmat_l[i], "nn", P_STATE) + q0_l[i]) * gam_l[i]
    for pp in range(npb):
        s_ref[pp] = st[pp]
    y = jnp.concatenate([jnp.concatenate(ys[pp], axis=0) for pp in range(npb)], axis=1)

    mu = _mm_exact_rhs(y, mean_bd)
    d = y - mu
    var = _mm_exact_rhs(d * d, mean_bd)
    yn = d * lax.rsqrt(var + RWKV_LN_EPS) * lng_ref[...] + lnb_ref[...]
    bonus = _mm_exact_rhs(r * k2 * rk_ref[...], ones_bd) * v
    o_ref[...] = ((yn + bonus) * g).astype(o_ref.dtype)


def rwkv_time_mix(z, mu_p, w0, w2, a0, a2, g2p, k_k, k_a, r_k, lnx_g, lnx_b, B, T, RW):
    L = RWKV_CHUNK
    R = L * RWKV_BLOCK_CHUNKS
    W = LANES
    WB = W * RWKV_BLOCK_PAIRS
    ngrp = RW // WB
    c_gl = 3 * RW // GATE_LORA_PAD
    c_wl = (3 * RW + GATE_LORA_PAD) // W
    c_al = c_wl + 1

    def zcol(base):
        return pl.BlockSpec((None, R, WB), lambda b, p, c: (b, c, base + p))

    def zfix(idx, width):
        return pl.BlockSpec((None, R, width), lambda b, p, c: (b, c, idx))

    def mucol(base):
        return pl.BlockSpec((1, WB), lambda b, p, c: (0, base + p))

    def mufix(idx, width):
        return pl.BlockSpec((1, width), lambda b, p, c: (0, idx))

    pvec = pl.BlockSpec((1, WB), lambda b, p, c: (0, p))

    def lora(rank):
        return pl.BlockSpec((rank, WB), lambda b, p, c: (0, p))

    row = lambda a: a.reshape(1, -1)
    return pl.pallas_call(
        functools.partial(_rwkv_kernel, L=L),
        grid=(B, ngrp, T // R),
        in_specs=[zcol(0), zcol(ngrp), zcol(2 * ngrp),
                  zfix(c_gl, GATE_LORA_PAD), zfix(c_wl, W), zfix(c_al, W),
                  mucol(0), mucol(ngrp), mucol(2 * ngrp),
                  mufix(c_gl, GATE_LORA_PAD), mufix(c_wl, W), mufix(c_al, W),
                  pvec, pvec, pvec, pvec, pvec, pvec, pvec,
                  lora(DECAY_LORA), lora(ICLR_LORA), lora(GATE_LORA_PAD)],
        out_specs=pl.BlockSpec((None, R, WB), lambda b, p, c: (b, c, p)),
        out_shape=jax.ShapeDtypeStruct((B, T, RW), BF16),
        scratch_shapes=[pltpu.VMEM((RWKV_BLOCK_PAIRS, W, W), F32)] + [pltpu.VMEM((8, WB), F32)] * 3
        + [pltpu.VMEM((8, GATE_LORA_PAD), F32), pltpu.VMEM((8, W), F32), pltpu.VMEM((8, W), F32)],
        compiler_params=_cparams(("parallel", "parallel", "arbitrary")),
        name="rwkv7",
    )(z, z, z, z, z, z, mu_p, mu_p, mu_p, mu_p, mu_p, mu_p,
      row(w0), row(a0), row(k_k), row(k_a), row(r_k), row(lnx_g), row(lnx_b), w2, a2, g2p)


def _pad_cols(w, n):
    return jnp.pad(w, ((0, 0), (0, n - w.shape[1])))


def kernel(x, c, ada_w, ada_b, ada_table, ln_g, ln_b, mix_w_in, mix_mu, rwkv_w0, rwkv_w2, rwkv_a0, rwkv_a2,
           rwkv_g2, rwkv_kk, rwkv_ka, rwkv_rk, rwkv_lnx_g, rwkv_lnx_b, mix_w_out, ffn_w_gate, ffn_w_up,
           ffn_w_down, pool_w, pool_scale, moe_router_w, moe_router_b, moe_w_gate, moe_w_up, moe_w_down):
    B, T, D = x.shape
    M = B * T
    depth = ada_table.shape[0]
    RW = rwkv_w0.shape[1]
    MW = D - RW
    n_moba = MW // MOBA_HEAD_DIM
    gate_lora = rwkv_g2.shape[1]
    alpha = (2 * depth) ** 0.25

    mods = ada_mod(c, ada_w, ada_b, ada_table)
    xf = x.reshape(M, D)
    h = modulate(xf, mods[0, :, 1], mods[0, :, 0], T)
    lnrow = lambda a: a.reshape(1, D)

    za = 3 * RW + GATE_LORA_PAD + DECAY_LORA + ICLR_LORA
    n_in = za + 3 * MW
    n_pad = -(-n_in // 1024) * 1024
    o_wl = 3 * RW
    o_al = o_wl + DECAY_LORA
    o_gl = o_al + ICLR_LORA
    o_q = o_gl + gate_lora

    def permute_cols(w):
        return jnp.concatenate(
            [w[..., :3 * RW], _pad_cols(w[..., o_gl:o_q], GATE_LORA_PAD), w[..., o_wl:o_al], w[..., o_al:o_gl],
             _pad_cols(w[..., o_q:], n_pad - za)], axis=-1)

    for l in range(depth):
        i = l // 2
        m = mods[l]
        g1, sh2, sc2, g2 = m[:, 2], m[:, 3], m[:, 4], m[:, 5]
        if l + 1 < depth:
            sh_n, sc_n = mods[l + 1, :, 0], mods[l + 1, :, 1]
        else:
            sh_n, sc_n = jnp.zeros_like(sh2), jnp.zeros_like(sc2)
        if l % 2 == 0:
            w_in = permute_cols(mix_w_in[i]).astype(BF16)
            mu_p = permute_cols(jnp.pad(mix_mu[i], (0, mix_w_in.shape[2] - mix_mu.shape[1]))[None, :])
            z = matmul(h, w_in, 1024, 1024).reshape(B, T, n_pad)
            g2p = jnp.pad(rwkv_g2[i], ((0, GATE_LORA_PAD - gate_lora), (0, 0)))
            ya = rwkv_time_mix(z, mu_p, rwkv_w0[i], rwkv_w2[i], rwkv_a0[i], rwkv_a2[i], g2p, rwkv_kk[i],
                               rwkv_ka[i], rwkv_rk[i], rwkv_lnx_g[i], rwkv_lnx_b[i], B, T, RW)
            yb = moba_attention(z, za // LANES, n_moba, B, T)
            ycat = jnp.concatenate([ya, yb], axis=-1).reshape(M, D)
            xf, h = matmul_resid_ln(ycat, cast_pad_bf16(mix_w_out, i), xf, g1, lnrow(ln_g[l, 0]), lnrow(ln_b[l, 0]),
                                    sc2, sh2, T, alpha, 512, 1024)
            F = ffn_w_gate.shape[2]
            fp = -(-F // 1024) * 1024
            act = swiglu_up(h, cast_pad_bf16(ffn_w_gate, i, cols_p=fp, br=512, bc=fp // 4),
                            cast_pad_bf16(ffn_w_up, i, cols_p=fp, br=512, bc=fp // 4),
                            1024, 512)
            wd = cast_pad_bf16(ffn_w_down, i, rows_p=fp)
            xf, h = matmul_resid_ln(act, wd, xf, g2, lnrow(ln_g[l, 1]), lnrow(ln_b[l, 1]), sc_n, sh_n, T, alpha,
                                    512, 1024)
        else:
            xf, h = pool_mixer_ln(xf, m[:, 1], m[:, 0], pool_w[i].astype(BF16), lnrow(pool_scale[i]), g1,
                                  lnrow(ln_g[l, 0]), lnrow(ln_b[l, 0]), sc2, sh2, T, alpha)
            E, _, NE = moe_w_gate.shape[1:]
            idx, gates = moe_router(h, moe_router_w[i], moe_router_b[i])
            row_token, tile_e, n_used, pos0, pos1 = moe_group_metadata(idx, E, MOE_TILE)
            xs = moe_gather_rows(h, row_token, MOE_TILE)
            cast3 = lambda w, bc: cast_pad_bf16(w.reshape(w.shape[0], -1, w.shape[-1]), i, bc=bc).reshape(w.shape[1:])
            act = moe_up_grouped(xs, tile_e, n_used, cast3(moe_w_gate, NE // 2), cast3(moe_w_up, NE // 2),
                                 MOE_TILE, NE // 2)
            ys = moe_down_grouped(act, tile_e, n_used, cast3(moe_w_down, 1024), MOE_TILE)
            xf, h = moe_combine_ln(ys, pos0, pos1, gates, xf, g2, lnrow(ln_g[l, 1]), lnrow(ln_b[l, 1]), sc_n, sh_n,
                                   T, alpha)
    return xf.reshape(B, T, D)
```

```python
import functools
import math

import jax
import jax.numpy as jnp
from jax import lax
from jax.experimental import pallas as pl
from jax.experimental.pallas import tpu as pltpu

F32 = jnp.float32
BF16 = jnp.bfloat16
HI = lax.Precision.HIGHEST

RWKV_HEAD_DIM = 64
DECAY_LORA = 128
ICLR_LORA = 128
GATE_LORA_PAD = 512
RWKV_LN_EPS = 64e-5
MOBA_HEAD_DIM = 128
MOBA_BLOCK = 256
MOBA_TOPK = 3
ROPE_THETA = 10000.0
POOL_WINDOWS = (2, 4, 8, 16)
POOL_HALO = 16
MOE_TOPK = 2
N_MOD = 6
LN_EPS = 1e-5
NEG_INF = -1e30

LANES = 128
LN_ROWS = 64
MOE_TILE = 256
DMA_UNROLL = 8
RWKV_CHUNK = 64
RWKV_BLOCK_CHUNKS = 4
RWKV_BLOCK_PAIRS = 4
P_LORA_W, P_LORA, P_SCORE, P_TRI, P_MAIN, P_STATE = 1, 1, 1, 1, 1, 1
VMEM_LIMIT = 56 * 1024 * 1024
VMEM_LIMIT_DOWN = 60 * 1024 * 1024


def _cparams(sem, vmem_limit=VMEM_LIMIT):
    return pltpu.CompilerParams(dimension_semantics=sem, vmem_limit_bytes=vmem_limit)


def _dot(a, b, precision=None):
    return jnp.dot(a, b, preferred_element_type=F32, precision=precision)


def _dot_nt(a, b, precision=None):
    return lax.dot_general(a, b, (((1,), (1,)), ((), ())), preferred_element_type=F32, precision=precision)


def _dot_tn(a, b, precision=None):
    return lax.dot_general(a, b, (((0,), (0,)), ((), ())), preferred_element_type=F32, precision=precision)


def _ada_kernel(c_ref, w_ref, b_ref, tab_ref, o_ref):
    c = c_ref[...]
    sct = (c * jax.nn.sigmoid(c)).T
    w = w_ref[...]
    rows = [jnp.sum(w * sct[:, b:b + 1], axis=0, keepdims=True) for b in range(4)]
    acc = jnp.concatenate(rows + [jnp.zeros((4, w.shape[1]), F32)], axis=0) + b_ref[...]
    o_ref[...] = acc[None, :, :] + tab_ref[...][:, None, :]


def ada_mod(c, ada_w, ada_b, ada_table):
    B, D = c.shape
    N = ada_w.shape[1]
    depth = ada_table.shape[0]
    BP = 8
    cp = jnp.zeros((BP, D), F32).at[:B].set(c)
    tn = 1024
    out = pl.pallas_call(
        _ada_kernel,
        grid=(N // tn,),
        in_specs=[
            pl.BlockSpec((BP, D), lambda j: (0, 0)),
            pl.BlockSpec((D, tn), lambda j: (0, j)),
            pl.BlockSpec((1, tn), lambda j: (0, j)),
            pl.BlockSpec((depth, tn), lambda j: (0, j)),
        ],
        out_specs=pl.BlockSpec((depth, BP, tn), lambda j: (0, 0, j)),
        out_shape=jax.ShapeDtypeStruct((depth, BP, N), F32),
        compiler_params=_cparams(("parallel",)),
        name="ada_mod",
    )(cp, ada_w, ada_b.reshape(1, N), ada_table.reshape(depth, N))
    return out[:, :B].reshape(depth, B, N_MOD, 1, D)


def _cast_pad_kernel(x_ref, o_ref, *, rows, cols):
    br, bc = o_ref.shape
    i, j = pl.program_id(0), pl.program_id(1)
    inside = ((i + 1) * br <= rows) & ((j + 1) * bc <= cols)

    @pl.when(inside)
    def _():
        o_ref[...] = x_ref[...].astype(o_ref.dtype)

    @pl.when(jnp.logical_not(inside))
    def _():
        r = i * br + lax.broadcasted_iota(jnp.int32, (br, bc), 0)
        c = j * bc + lax.broadcasted_iota(jnp.int32, (br, bc), 1)
        o_ref[...] = jnp.where((r < rows) & (c < cols), x_ref[...], 0.0).astype(o_ref.dtype)


def cast_pad_bf16(w_stack, layer, rows_p=None, cols_p=None, br=1024, bc=1024):
    _, rows, cols = w_stack.shape
    rows_p = rows_p or rows
    cols_p = cols_p or cols
    bc = min(bc, cols_p)
    last_r = (rows - 1) // br
    last_c = (cols - 1) // bc
    return pl.pallas_call(
        functools.partial(_cast_pad_kernel, rows=rows, cols=cols),
        grid=(rows_p // br, cols_p // bc),
        in_specs=[pl.BlockSpec((None, br, bc),
                               lambda i, j: (layer, jnp.minimum(i, last_r), jnp.minimum(j, last_c)))],
        out_specs=pl.BlockSpec((br, bc), lambda i, j: (i, j)),
        out_shape=jax.ShapeDtypeStruct((rows_p, cols_p), BF16),
        compiler_params=_cparams(("parallel", "parallel")),
        name="cast_pad",
    )(w_stack)


def _modulate_kernel(x_ref, sc_ref, sh_ref, o_ref):
    o_ref[...] = (x_ref[...] * (1.0 + sc_ref[...]) + sh_ref[...]).astype(o_ref.dtype)


def modulate(x, sc, sh, T):
    M, D = x.shape
    tm = 512
    tpb = T // tm
    vec = pl.BlockSpec((None, 1, D), lambda i: (i // tpb, 0, 0))
    return pl.pallas_call(
        _modulate_kernel,
        grid=(M // tm,),
        in_specs=[pl.BlockSpec((tm, D), lambda i: (i, 0)), vec, vec],
        out_specs=pl.BlockSpec((tm, D), lambda i: (i, 0)),
        out_shape=jax.ShapeDtypeStruct((M, D), BF16),
        compiler_params=_cparams(("parallel",)),
        name="modulate",
    )(x, sc, sh)


def _mm_kernel(a_ref, w_ref, o_ref):
    o_ref[...] = _dot(a_ref[...], w_ref[...]).astype(o_ref.dtype)


def matmul(a, w, tm, tn, out_dtype=F32):
    M, K = a.shape
    N = w.shape[1]
    return pl.pallas_call(
        _mm_kernel,
        grid=(N // tn, M // tm),
        in_specs=[pl.BlockSpec((tm, K), lambda j, i: (i, 0)),
                  pl.BlockSpec((K, tn), lambda j, i: (0, j))],
        out_specs=pl.BlockSpec((tm, tn), lambda j, i: (i, j)),
        out_shape=jax.ShapeDtypeStruct((M, N), out_dtype),
        compiler_params=_cparams(("parallel", "parallel")),
        name="matmul",
    )(a, w)


def _resid_ln_mod(x, y, gate, lng, lnb, sc, sh, alpha):
    r = alpha * x + gate * y
    mu = jnp.mean(r, axis=-1, keepdims=True)
    d = r - mu
    var = jnp.mean(d * d, axis=-1, keepdims=True)
    xn = d * lax.rsqrt(var + LN_EPS) * lng + lnb
    return xn, xn * (1.0 + sc) + sh


def _mm_ln_kernel(a_ref, w_ref, x_ref, gate_ref, lng_ref, lnb_ref, sc_ref, sh_ref,
                  xo_ref, ho_ref, *, alpha):
    k = pl.program_id(1)

    @pl.when(k == 0)
    def _():
        xo_ref[...] = _dot(a_ref[...], w_ref[...])

    @pl.when(k > 0)
    def _():
        xo_ref[...] += _dot(a_ref[...], w_ref[...])

    @pl.when(k == pl.num_programs(1) - 1)
    def _():
        for r0 in range(0, xo_ref.shape[0], LN_ROWS):
            rows = pl.ds(r0, LN_ROWS)
            xn, h = _resid_ln_mod(x_ref[rows, :], xo_ref[rows, :], gate_ref[...], lng_ref[...], lnb_ref[...],
                                  sc_ref[...], sh_ref[...], alpha)
            xo_ref[rows, :] = xn
            ho_ref[rows, :] = h.astype(ho_ref.dtype)


def matmul_resid_ln(a, w, x, gate, lng, lnb, sc, sh, T, alpha, tm, tk):
    M, K = a.shape
    D = w.shape[1]
    tpb = T // tm
    vec_b = pl.BlockSpec((None, 1, D), lambda i, k: (i // tpb, 0, 0))
    vec = pl.BlockSpec((1, D), lambda i, k: (0, 0))
    row = pl.BlockSpec((tm, D), lambda i, k: (i, 0))
    return pl.pallas_call(
        functools.partial(_mm_ln_kernel, alpha=alpha),
        grid=(M // tm, K // tk),
        in_specs=[pl.BlockSpec((tm, tk), lambda i, k: (i, k)),
                  pl.BlockSpec((tk, D), lambda i, k: (k, 0)),
                  pl.BlockSpec((tm, D), lambda i, k: (i, 0), pipeline_mode=pl.Buffered(1)),
                  vec_b, vec, vec, vec_b, vec_b],
        out_specs=[row, row],
        out_shape=[jax.ShapeDtypeStruct((M, D), F32), jax.ShapeDtypeStruct((M, D), BF16)],
        compiler_params=_cparams(("parallel", "arbitrary"), VMEM_LIMIT_DOWN),
        name="matmul_resid_ln",
    )(a, w, x, gate, lng, lnb, sc, sh)


def _swiglu_up_kernel(a_ref, wg_ref, wu_ref, o_ref):
    a = a_ref[...]
    g = _dot(a, wg_ref[...])
    u = _dot(a, wu_ref[...])
    o_ref[...] = (g * jax.nn.sigmoid(g) * u).astype(o_ref.dtype)


def swiglu_up(a, wg, wu, tm, tn):
    M, K = a.shape
    N = wg.shape[1]
    wspec = pl.BlockSpec((K, tn), lambda i, j: (0, j))
    return pl.pallas_call(
        _swiglu_up_kernel,
        grid=(M // tm, N // tn),
        in_specs=[pl.BlockSpec((tm, K), lambda i, j: (i, 0)), wspec, wspec],
        out_specs=pl.BlockSpec((tm, tn), lambda i, j: (i, j)),
        out_shape=jax.ShapeDtypeStruct((M, N), BF16),
        compiler_params=_cparams(("parallel", "parallel")),
        name="swiglu_up",
    )(a, wg, wu)


def _router_kernel(h_ref, rw_ref, rb_ref, idx_ref, gate_ref, *, n_experts):
    logits = _dot(h_ref[...], rw_ref[...], HI) + rb_ref[...]
    lane = lax.broadcasted_iota(jnp.int32, logits.shape, 1)
    logits = jnp.where(lane < n_experts, logits, -jnp.inf)
    m1 = jnp.max(logits, axis=-1, keepdims=True)
    i1 = jnp.min(jnp.where(logits == m1, lane, LANES), axis=-1, keepdims=True)
    oh1 = lane == i1
    rest = jnp.where(oh1, -jnp.inf, logits)
    m2 = jnp.max(rest, axis=-1, keepdims=True)
    i2 = jnp.min(jnp.where(rest == m2, lane, LANES), axis=-1, keepdims=True)
    e = jnp.exp(m2 - m1)
    g1 = 1.0 / (1.0 + e)
    g2 = e / (1.0 + e)
    idx_ref[...] = jnp.where(lane == 0, i1, jnp.where(lane == 1, i2, 0))
    gate_ref[...] = jnp.where(lane == 0, g1, jnp.where(lane == 1, g2, 0.0))


def moe_router(h, router_w, router_b):
    M, D = h.shape
    E = router_w.shape[1]
    tm = 512
    rw = jnp.zeros((D, LANES), F32).at[:, :E].set(router_w)
    rb = jnp.zeros((1, LANES), F32).at[0, :E].set(router_b)
    out = pl.BlockSpec((tm, LANES), lambda i: (i, 0))
    return pl.pallas_call(
        functools.partial(_router_kernel, n_experts=E),
        grid=(M // tm,),
        in_specs=[pl.BlockSpec((tm, D), lambda i: (i, 0)),
                  pl.BlockSpec((D, LANES), lambda i: (0, 0)),
                  pl.BlockSpec((1, LANES), lambda i: (0, 0))],
        out_specs=[out, out],
        out_shape=[jax.ShapeDtypeStruct((M, LANES), jnp.int32), jax.ShapeDtypeStruct((M, LANES), F32)],
        compiler_params=_cparams(("parallel",)),
        name="moe_router",
    )(h, rw, rb)


def _gather_rows_kernel(tok_ref, h_hbm, o_ref, buf_ref, sem):
    i = pl.program_id(0)
    tm = o_ref.shape[0]
    slot = i % 2

    def start_tile(tile, s):
        def body(r0, c):
            for u in range(DMA_UNROLL):
                r = r0 * DMA_UNROLL + u
                pltpu.make_async_copy(h_hbm.at[pl.ds(tok_ref[tile * tm + r], 1), :], buf_ref.at[s, pl.ds(r, 1), :],
                                      sem.at[s]).start()
            return c

        lax.fori_loop(0, tm // DMA_UNROLL, body, 0)

    @pl.when(i == 0)
    def _():
        start_tile(i, slot)

    @pl.when(i + 1 < pl.num_programs(0))
    def _():
        start_tile(i + 1, 1 - slot)

    pltpu.make_async_copy(h_hbm.at[pl.ds(0, tm), :], buf_ref.at[slot], sem.at[slot]).wait()
    o_ref[...] = buf_ref[slot].astype(o_ref.dtype)


def moe_gather_rows(h, row_token, tm):
    D = h.shape[1]
    NR = row_token.shape[0]
    return pl.pallas_call(
        _gather_rows_kernel,
        grid_spec=pltpu.PrefetchScalarGridSpec(
            num_scalar_prefetch=1,
            grid=(NR // tm,),
            in_specs=[pl.BlockSpec(memory_space=pl.ANY)],
            out_specs=pl.BlockSpec((tm, D), lambda i, tok: (i, 0)),
            scratch_shapes=[pltpu.VMEM((2, tm, D), F32), pltpu.SemaphoreType.DMA((2,))]),
        out_shape=jax.ShapeDtypeStruct((NR, D), BF16),
        compiler_params=_cparams(("arbitrary",)),
        name="moe_gather",
    )(row_token, h)


def _moe_up_kernel(te_ref, nu_ref, a_ref, wg_ref, wu_ref, o_ref):
    i = pl.program_id(1)

    @pl.when(i < nu_ref[0])
    def _():
        a = a_ref[...]
        g = _dot(a, wg_ref[...])
        u = _dot(a, wu_ref[...])
        o_ref[...] = (g * jax.nn.sigmoid(g) * u).astype(o_ref.dtype)

    @pl.when(i >= nu_ref[0])
    def _():
        o_ref[...] = jnp.zeros_like(o_ref)


def moe_up_grouped(xs, tile_e, n_used, wg, wu, tm, tn):
    NR, K = xs.shape
    E, _, NE = wg.shape
    wspec = pl.BlockSpec((None, K, tn), lambda j, i, te, nu: (te[i], 0, j))
    return pl.pallas_call(
        _moe_up_kernel,
        grid_spec=pltpu.PrefetchScalarGridSpec(
            num_scalar_prefetch=2,
            grid=(NE // tn, NR // tm),
            in_specs=[pl.BlockSpec((tm, K), lambda j, i, te, nu: (i, 0)), wspec, wspec],
            out_specs=pl.BlockSpec((tm, tn), lambda j, i, te, nu: (i, j))),
        out_shape=jax.ShapeDtypeStruct((NR, NE), BF16),
        compiler_params=_cparams(("parallel", "parallel")),
        name="moe_up",
    )(tile_e, n_used, xs, wg, wu)


def _moe_down_kernel(te_ref, nu_ref, a_ref, w_ref, o_ref):
    i = pl.program_id(0)

    @pl.when(i < nu_ref[0])
    def _():
        o_ref[...] = _dot(a_ref[...], w_ref[...])

    @pl.when(i >= nu_ref[0])
    def _():
        o_ref[...] = jnp.zeros_like(o_ref)


def moe_down_grouped(act, tile_e, n_used, wd, tm):
    NR, NE = act.shape
    D = wd.shape[2]
    return pl.pallas_call(
        _moe_down_kernel,
        grid_spec=pltpu.PrefetchScalarGridSpec(
            num_scalar_prefetch=2,
            grid=(NR // tm,),
            in_specs=[pl.BlockSpec((tm, NE), lambda i, te, nu: (i, 0)),
                      pl.BlockSpec((None, NE, D), lambda i, te, nu: (te[i], 0, 0))],
            out_specs=pl.BlockSpec((tm, D), lambda i, te, nu: (i, 0))),
        out_shape=jax.ShapeDtypeStruct((NR, D), F32),
        compiler_params=_cparams(("parallel",)),
        name="moe_down",
    )(tile_e, n_used, act, wd)


def _moe_combine_ln_kernel(p0_ref, p1_ref, ys_hbm, x_ref, rg_ref, gate_ref, lng_ref, lnb_ref, sc_ref, sh_ref,
                           xo_ref, ho_ref, b0_ref, b1_ref, sem, *, alpha):
    i = pl.program_id(0)
    tm = x_ref.shape[0]
    slot = i % 2

    def start_tile(tile, s):
        def body(r0, c):
            for u in range(DMA_UNROLL):
                r = r0 * DMA_UNROLL + u
                t = tile * tm + r
                pltpu.make_async_copy(ys_hbm.at[pl.ds(p0_ref[t], 1), :], b0_ref.at[s, pl.ds(r, 1), :],
                                      sem.at[0, s]).start()
                pltpu.make_async_copy(ys_hbm.at[pl.ds(p1_ref[t], 1), :], b1_ref.at[s, pl.ds(r, 1), :],
                                      sem.at[1, s]).start()
            return c

        lax.fori_loop(0, tm // DMA_UNROLL, body, 0)

    @pl.when(i == 0)
    def _():
        start_tile(i, slot)

    @pl.when(i + 1 < pl.num_programs(0))
    def _():
        start_tile(i + 1, 1 - slot)

    pltpu.make_async_copy(ys_hbm.at[pl.ds(0, tm), :], b0_ref.at[slot], sem.at[0, slot]).wait()
    pltpu.make_async_copy(ys_hbm.at[pl.ds(0, tm), :], b1_ref.at[slot], sem.at[1, slot]).wait()
    for r0 in range(0, tm, LN_ROWS):
        rows = pl.ds(r0, LN_ROWS)
        rg = rg_ref[rows, :]
        y = rg[:, 0:1] * b0_ref[slot, rows, :] + rg[:, 1:2] * b1_ref[slot, rows, :]
        xn, h = _resid_ln_mod(x_ref[rows, :], y, gate_ref[...], lng_ref[...], lnb_ref[...],
                              sc_ref[...], sh_ref[...], alpha)
        xo_ref[rows, :] = xn
        ho_ref[rows, :] = h.astype(ho_ref.dtype)


def moe_combine_ln(ys, pos0, pos1, rgates, x, gate, lng, lnb, sc, sh, T, alpha):
    M, D = x.shape
    tm = 256
    tpb = T // tm
    vec_b = pl.BlockSpec((None, 1, D), lambda i, p0, p1: (i // tpb, 0, 0))
    vec = pl.BlockSpec((1, D), lambda i, p0, p1: (0, 0))
    row = pl.BlockSpec((tm, D), lambda i, p0, p1: (i, 0))
    return pl.pallas_call(
        functools.partial(_moe_combine_ln_kernel, alpha=alpha),
        grid_spec=pltpu.PrefetchScalarGridSpec(
            num_scalar_prefetch=2,
            grid=(M // tm,),
            in_specs=[pl.BlockSpec(memory_space=pl.ANY), row,
                      pl.BlockSpec((tm, LANES), lambda i, p0, p1: (i, 0)), vec_b, vec, vec, vec_b, vec_b],
            out_specs=[row, row],
            scratch_shapes=[pltpu.VMEM((2, tm, D), F32), pltpu.VMEM((2, tm, D), F32),
                            pltpu.SemaphoreType.DMA((2, 2))]),
        out_shape=[jax.ShapeDtypeStruct((M, D), F32), jax.ShapeDtypeStruct((M, D), BF16)],
        compiler_params=_cparams(("arbitrary",)),
        name="moe_combine_ln",
    )(pos0, pos1, ys, x, rgates, gate, lng, lnb, sc, sh)


def moe_group_metadata(idx, n_experts, tm):
    M = idx.shape[0]
    n_pairs = MOE_TOPK * M
    e_flat = idx[:, :MOE_TOPK].reshape(-1)
    onehot = (e_flat[:, None] == jnp.arange(n_experts)[None, :]).astype(F32)
    blk = LANES
    oh3 = onehot.reshape(n_pairs // blk, blk, n_experts)
    tri = (jnp.arange(blk)[:, None] > jnp.arange(blk)[None, :]).astype(F32)
    within = jnp.einsum("ts,bse->bte", tri, oh3, precision=HI)
    tot = oh3.sum(axis=1)
    before = jnp.cumsum(tot, axis=0) - tot
    excl = (within + before[:, None, :]).reshape(n_pairs, n_experts)
    rank = jnp.sum(excl * onehot, axis=1).astype(jnp.int32)
    counts = tot.sum(axis=0).astype(jnp.int32)
    padded = -(-counts // tm) * tm
    ends = jnp.cumsum(padded)
    starts = ends - padded
    pos = jnp.sum(onehot * starts[None, :].astype(F32), axis=1).astype(jnp.int32) + rank
    n_rows = n_pairs + n_experts * tm
    row_token = jnp.zeros((n_rows,), jnp.int32).at[pos].set(jnp.arange(n_pairs, dtype=jnp.int32) // MOE_TOPK)
    tile_start = jnp.arange(n_rows // tm, dtype=jnp.int32) * tm
    tile_e = jnp.minimum(jnp.sum(tile_start[:, None] >= ends[None, :], axis=1), n_experts - 1).astype(jnp.int32)
    n_used = (ends[-1:] // tm).astype(jnp.int32)
    pos = pos.reshape(M, MOE_TOPK)
    return row_token, tile_e, n_used, pos[:, 0], pos[:, 1]


def _pool_kernel(x_ref, xh_ref, sc_ref, sh_ref, w_ref, ps_ref, gate_ref, lng_ref, lnb_ref,
                 sc2_ref, sh2_ref, xo_ref, ho_ref, hh_ref, y_ref, *, tm, tpb, alpha):
    i = pl.program_id(0)
    D = x_ref.shape[1]
    dg = D // len(POOL_WINDOWS)
    sc = sc_ref[...]
    sh = sh_ref[...]
    x = x_ref[...]
    first = (i % tpb) == 0
    halo = xh_ref[...] * (1.0 + sc) + sh
    hh_ref[0:POOL_HALO, :] = jnp.where(first, 0.0, halo)
    hh_ref[POOL_HALO:, :] = x * (1.0 + sc) + sh
    t = (i % tpb) * tm + lax.broadcasted_iota(jnp.int32, (tm, 1), 0)
    tcnt = (t + 1).astype(F32)
    for g, win in enumerate(POOL_WINDOWS):
        cols = slice(g * dg, (g + 1) * dg)
        h = hh_ref[POOL_HALO:, cols]
        s = h
        for d in range(1, win):
            s = s + hh_ref[pl.ds(POOL_HALO - d, tm), cols]
        inv = 1.0 / jnp.minimum(tcnt, float(win))
        p = (s * inv - h).astype(BF16)
        y_ref[:, cols] = _dot(p, w_ref[g])
    y = y_ref[...] * ps_ref[...]
    xn, hn = _resid_ln_mod(x, y, gate_ref[...], lng_ref[...], lnb_ref[...], sc2_ref[...], sh2_ref[...], alpha)
    xo_ref[...] = xn
    ho_ref[...] = hn.astype(ho_ref.dtype)


def pool_mixer_ln(x, sc, sh, pool_w, pool_scale, gate, lng, lnb, sc2, sh2, T, alpha):
    M, D = x.shape
    G, dg, _ = pool_w.shape
    tm = 256
    tpb = T // tm
    hb = tm // POOL_HALO
    vec_b = pl.BlockSpec((None, 1, D), lambda i: (i // tpb, 0, 0))
    vec = pl.BlockSpec((1, D), lambda i: (0, 0))
    row = pl.BlockSpec((tm, D), lambda i: (i, 0))
    return pl.pallas_call(
        functools.partial(_pool_kernel, tm=tm, tpb=tpb, alpha=alpha),
        grid=(M // tm,),
        in_specs=[row,
                  pl.BlockSpec((POOL_HALO, D), lambda i: (jnp.maximum(i * hb - 1, 0), 0)),
                  vec_b, vec_b,
                  pl.BlockSpec((G, dg, dg), lambda i: (0, 0, 0)),
                  vec, vec_b, vec, vec, vec_b, vec_b],
        out_specs=[row, row],
        out_shape=[jax.ShapeDtypeStruct((M, D), F32), jax.ShapeDtypeStruct((M, D), F32)],
        scratch_shapes=[pltpu.VMEM((tm + POOL_HALO, D), F32), pltpu.VMEM((tm, D), F32)],
        compiler_params=_cparams(("parallel",)),
        name="pool_mixer",
    )(x, x, sc, sh, pool_w, pool_scale, gate, lng, lnb, sc2, sh2)


def _moba_kernel(q_ref, k_ref, v_ref, cos_ref, sin_ref, o_ref, *, nb):
    bs = MOBA_BLOCK
    cos = cos_ref[...]
    sin = sin_ref[...]

    def rope(t):
        return t * cos + pltpu.roll(t, MOBA_HEAD_DIM // 2, axis=1) * sin

    qr = rope(q_ref[...])
    kr = rope(k_ref[...])
    T = qr.shape[0]
    prow = lax.broadcasted_iota(jnp.int32, (LANES, T), 0)
    pcol = lax.broadcasted_iota(jnp.int32, (LANES, T), 1)
    pool = jnp.where(pcol // bs == prow, 1.0 / bs, 0.0).astype(F32)
    kmean = _dot(pool, kr, HI)
    qb = qr.astype(BF16)
    kb = kr.astype(BF16)
    vb = v_ref[...].astype(BF16)
    scale = MOBA_HEAD_DIM ** -0.5
    lane = lax.broadcasted_iota(jnp.int32, (bs, LANES), 1)
    rowi = lax.broadcasted_iota(jnp.int32, (bs, bs), 0)
    coli = lax.broadcasted_iota(jnp.int32, (bs, bs), 1)
    for i in range(nb):
        rows = slice(i * bs, (i + 1) * bs)
        qi = qb[rows]
        s_list = [jnp.where(coli <= rowi, _dot_nt(qi, kb[rows]) * scale, NEG_INF)]
        if i > 0:
            gate = _dot_nt(qr[rows], kmean, HI)
            valid = lane < i
            gate = jnp.where(valid, gate, -jnp.inf)
            rank = jnp.zeros((bs, LANES), jnp.int32)
            for j in range(i):
                gj = gate[:, j:j + 1]
                ahead = (gj > gate) | ((gj == gate) & (j < lane))
                rank = rank + ahead.astype(jnp.int32)
            sel = valid & (rank < MOBA_TOPK)
            for j in range(i):
                sj = _dot_nt(qi, kb[j * bs:(j + 1) * bs]) * scale
                s_list.append(jnp.where(sel[:, j:j + 1], sj, NEG_INF))
        smax = s_list[0]
        for s in s_list[1:]:
            smax = jnp.maximum(smax, s)
        m = smax.max(axis=-1, keepdims=True)
        psum = jnp.zeros((bs, bs), F32)
        acc = jnp.zeros((bs, MOBA_HEAD_DIM), F32)
        for jj, s in enumerate(s_list):
            j = i if jj == 0 else jj - 1
            p = jnp.exp(s - m)
            psum = psum + p
            acc = acc + _dot(p.astype(BF16), vb[j * bs:(j + 1) * bs])
        l = psum.sum(axis=-1, keepdims=True)
        o_ref[rows, :] = (acc / l).astype(o_ref.dtype)


def moba_attention(z, q_col, n_heads, B, T):
    assert T % MOBA_BLOCK == 0
    nb = T // MOBA_BLOCK
    dh = MOBA_HEAD_DIM
    inv = jnp.power(ROPE_THETA, -jnp.arange(0, dh, 2, dtype=F32) / dh)
    ang = jnp.arange(T, dtype=F32)[:, None] * inv[None, :]
    cos = jnp.concatenate([jnp.cos(ang), jnp.cos(ang)], axis=-1)
    sin = jnp.concatenate([-jnp.sin(ang), jnp.sin(ang)], axis=-1)

    def col(off):
        return pl.BlockSpec((None, T, dh), lambda b, h: (b, 0, q_col + off * n_heads + h))

    tab = pl.BlockSpec((T, dh), lambda b, h: (0, 0))
    return pl.pallas_call(
        functools.partial(_moba_kernel, nb=nb),
        grid=(B, n_heads),
        in_specs=[col(0), col(1), col(2), tab, tab],
        out_specs=pl.BlockSpec((None, T, dh), lambda b, h: (b, 0, h)),
        out_shape=jax.ShapeDtypeStruct((B, T, n_heads * dh), BF16),
        compiler_params=_cparams(("parallel", "parallel")),
        name="moba",
    )(z, z, z, cos, sin)


def _split_bf16(x, n):
    parts, rem = [], x
    for i in range(n):
        p = rem.astype(BF16)
        parts.append(p)
        if i + 1 < n:
            rem = rem - p.astype(F32)
    return parts


_DIMS = {"nn": ((1,), (0,)), "nt": ((1,), (1,)), "tn": ((0,), (0,))}


def _mm(a, b, form="nn", passes=1):
    if passes == 1:
        a3, b3 = a.astype(BF16), b.astype(BF16)
    else:
        ah, al = _split_bf16(a, 2)
        bh, bl = _split_bf16(b, 2)
        a3 = jnp.concatenate([ah, ah, al], axis=0 if form == "tn" else 1)
        b3 = jnp.concatenate([bh, bl, bh], axis=1 if form == "nt" else 0)
    return lax.dot_general(a3, b3, (_DIMS[form], ((), ())), preferred_element_type=F32)


def _mm_exact_rhs(a, b_bf16):
    a3 = jnp.concatenate(_split_bf16(a, 3), axis=1)
    b3 = jnp.concatenate([b_bf16, b_bf16, b_bf16], axis=0)
    return _dot(a3, b3)


def _rwkv_kernel(zr_ref, zk_ref, zv_ref, zgl_ref, zwl_ref, zal_ref,
                 mur_ref, muk_ref, muv_ref, mugl_ref, muwl_ref, mual_ref,
                 w0_ref, a0_ref, kk_ref, ka_ref, rk_ref, lng_ref, lnb_ref,
                 w2_ref, a2_ref, g2_ref,
                 o_ref,
                 s_ref, pr_ref, pk_ref, pv_ref, pgl_ref, pwl_ref, pal_ref, *, L):
    R = zr_ref.shape[0]
    hd = RWKV_HEAD_DIM
    c = pl.program_id(2)

    @pl.when(c == 0)
    def _():
        s_ref[...] = jnp.zeros_like(s_ref)
        for p in (pr_ref, pk_ref, pv_ref, pgl_ref, pwl_ref, pal_ref):
            p[...] = jnp.zeros_like(p)

    def shift(z_ref, p_ref, mu_ref):
        z = z_ref[...]
        row = lax.broadcasted_iota(jnp.int32, z.shape, 0)
        zp = jnp.where(row == 0, p_ref[0:1, :], pltpu.roll(z, 1, axis=0))
        p_ref[0:1, :] = z[R - 1:R, :]
        return z + (zp - z) * mu_ref[...]

    r = shift(zr_ref, pr_ref, mur_ref)
    k = shift(zk_ref, pk_ref, muk_ref)
    v = shift(zv_ref, pv_ref, muv_ref)
    gl = shift(zgl_ref, pgl_ref, mugl_ref)
    wl = shift(zwl_ref, pwl_ref, muwl_ref)
    al = shift(zal_ref, pal_ref, mual_ref)

    WB = zr_ref.shape[1]
    lane_r = lax.broadcasted_iota(jnp.int32, (WB, WB), 0)
    lane_c = lax.broadcasted_iota(jnp.int32, (WB, WB), 1)
    same_head_b = (lane_r // hd) == (lane_c // hd)
    ones_bd = jnp.where(same_head_b, 1.0, 0.0).astype(BF16)
    mean_bd = jnp.where(same_head_b, 1.0 / hd, 0.0).astype(BF16)
    same_head = same_head_b[0:LANES, 0:LANES]
    head0 = lax.broadcasted_iota(jnp.int32, (1, LANES), 1) < hd
    not_head0 = jnp.logical_not(head0)

    wpre = w0_ref[...] + _mm(jnp.tanh(wl), w2_ref[...], "nn", P_LORA_W)
    nw = -wpre
    softplus = jnp.maximum(nw, 0.0) + jnp.log(1.0 + jnp.exp(-jnp.abs(nw)))
    wlog = -softplus - 0.5
    a = jax.nn.sigmoid(a0_ref[...] + _mm(al, a2_ref[...], "nn", P_LORA))
    g = _mm(jax.nn.sigmoid(gl), g2_ref[...], "nn", P_LORA)
    kkv = k * kk_ref[...]
    kkn = kkv / jnp.maximum(jnp.sqrt(_mm_exact_rhs(kkv * kkv, ones_bd)), 1e-12)
    k2 = k * (1.0 + (a - 1.0) * ka_ref[...])
    ld = -jnp.exp(wlog)
    bv = kkn * a

    rr = lax.broadcasted_iota(jnp.int32, (R, R), 0)
    rc = lax.broadcasted_iota(jnp.int32, (R, R), 1)
    ltri = jnp.where((rr // L == rc // L) & (rr >= rc), 1.0, 0.0).astype(BF16)
    cum3 = _dot(ltri, jnp.concatenate(_split_bf16(ld, 3), axis=1))
    cum = cum3[:, 0:WB] + cum3[:, WB:2 * WB] + cum3[:, 2 * WB:3 * WB]

    S2 = 2 * L
    srow = lax.broadcasted_iota(jnp.int32, (S2, S2), 0)
    scol = lax.broadcasted_iota(jnp.int32, (S2, S2), 1)
    same_blk = (srow // L) == (scol // L)
    strict_bd = same_blk & (srow > scol)
    incl_bd = same_blk & (srow >= scol)
    tri_masks = []
    s = 1
    while s < L:
        tri_masks.append((srow // (2 * s) == scol // (2 * s)) & ((srow // s) % 2 == 1) & ((scol // s) % 2 == 0))
        s *= 2
    mid = L // 2 - 1
    zeros_l = jnp.zeros((L, LANES), F32)

    npb = WB // LANES
    nch = R // L
    streams = [(pp, j) for pp in range(npb) for j in range(nch)]
    eye = jnp.where(srow == scol, 1.0, 0.0)

    def piece(x, pp, j):
        return x[j * L:(j + 1) * L, pp * LANES:(pp + 1) * LANES]

    at_l, bt_l, kt_l, rt_l, v_l, rho_l, gam_l = [], [], [], [], [], [], []
    for pp, j in streams:
        cumj = piece(cum, pp, j)
        cmid = cumj[mid:mid + 1, :]
        cc = cumj - cmid
        e_neg = jnp.exp(-cc)
        at_l.append(-piece(kkn, pp, j) * jnp.exp(cc - piece(ld, pp, j)))
        bt_l.append(piece(bv, pp, j) * e_neg)
        kt_l.append(piece(k2, pp, j) * e_neg)
        rt_l.append(piece(r, pp, j) * jnp.exp(cc))
        v_l.append(piece(v, pp, j))
        rho_l.append(jnp.exp(cmid))
        gam_l.append(jnp.exp(cc[L - 1:L, :]))

    def stack_heads(x):
        return jnp.concatenate([jnp.where(head0, x, 0.0), jnp.where(not_head0, x, 0.0)], axis=0)

    ats_l = [stack_heads(x) for x in at_l]
    rts_l = [stack_heads(x) for x in rt_l]
    vs_l = [jnp.concatenate([x, x], axis=0) for x in v_l]
    sc_l = [_mm(jnp.concatenate([a_, r_], axis=0), jnp.concatenate([b_, b_, k_, k_], axis=0), "nt", P_SCORE)
            for a_, r_, b_, k_ in zip(ats_l, rts_l, bt_l, kt_l)]
    nab_l = [jnp.where(strict_bd, s_[0:S2, 0:S2], 0.0) for s_ in sc_l]
    nak_l = [jnp.where(strict_bd, s_[0:S2, S2:2 * S2], 0.0) for s_ in sc_l]
    mrb_l = [jnp.where(incl_bd, s_[S2:2 * S2, 0:S2], 0.0) for s_ in sc_l]
    mrk_l = [jnp.where(incl_bd, s_[S2:2 * S2, S2:2 * S2], 0.0) for s_ in sc_l]
    t_l = [eye + jnp.where(tri_masks[0], n_, 0.0) for n_ in nab_l]
    for m in tri_masks[1:]:
        ot_l = [_mm(jnp.where(m, n_, 0.0), t_, "nn", P_TRI) for n_, t_ in zip(nab_l, t_l)]
        t_l = [t_ + _mm(t_, ot_, "nn", P_TRI) for t_, ot_ in zip(t_l, ot_l)]
    nv_l = [_mm(n_, v_, "nn", P_MAIN) for n_, v_ in zip(nak_l, vs_l)]
    au_l = [_mm(t_, jnp.concatenate([a_, nv_], axis=1), "nn", P_MAIN) for t_, a_, nv_ in zip(t_l, ats_l, nv_l)]
    ry_l = [_mm(jnp.concatenate([mb_, mk_], axis=1),
                jnp.concatenate([au_, jnp.concatenate([jnp.zeros_like(v_), v_], axis=1)], axis=0), "nn", P_MAIN)
            for mb_, mk_, au_, v_ in zip(mrb_l, mrk_l, au_l, vs_l)]
    rbar_l = [rt_ + ry_[0:L, 0:LANES] + ry_[L:S2, 0:LANES] for rt_, ry_ in zip(rt_l, ry_l)]
    ybase_l = [jnp.where(head0, ry_[0:L, LANES:2 * LANES], ry_[L:S2, LANES:2 * LANES]) for ry_ in ry_l]
    gq_l = [_mm(jnp.concatenate(
                [jnp.concatenate([au_[0:L, 0:LANES] + au_[L:S2, 0:LANES],
                                  jnp.where(head0, au_[0:L, LANES:2 * LANES], au_[L:S2, LANES:2 * LANES])], axis=1),
                 jnp.concatenate([zeros_l, v_], axis=1)], axis=0),
                jnp.concatenate([b_, k_], axis=0), "tn", P_MAIN)
            for au_, v_, b_, k_ in zip(au_l, v_l, bt_l, kt_l)]
    gmat_l = [jnp.where(same_head, gq_[0:LANES], 0.0) for gq_ in gq_l]
    q0_l = [jnp.where(same_head, gq_[LANES:2 * LANES], 0.0) for gq_ in gq_l]
    st = [s_ref[pp] for pp in range(npb)]
    ys = [[None] * nch for _ in range(npb)]
    for j in range(nch):
        for pp in range(npb):
            i = pp * nch + j
            sr = st[pp] * rho_l[i]
            ys[pp][j] = _mm(rbar_l[i], sr, "nt", P_STATE) + ybase_l[i]
            st[pp] = (sr + _mm(sr, gmat_l[i], "nn", P_STATE) + q0_l[i]) * gam_l[i]
    for pp in range(npb):
        s_ref[pp] = st[pp]
    y = jnp.concatenate([jnp.concatenate(ys[pp], axis=0) for pp in range(npb)], axis=1)

    mu = _mm_exact_rhs(y, mean_bd)
    d = y - mu
    var = _mm_exact_rhs(d * d, mean_bd)
    yn = d * lax.rsqrt(var + RWKV_LN_EPS) * lng_ref[...] + lnb_ref[...]
    bonus = _mm_exact_rhs(r * k2 * rk_ref[...], ones_bd) * v
    o_ref[...] = ((yn + bonus) * g).astype(o_ref.dtype)


def rwkv_time_mix(z, mu_p, w0, w2, a0, a2, g2p, k_k, k_a, r_k, lnx_g, lnx_b, B, T, RW):
    L = RWKV_CHUNK
    R = L * RWKV_BLOCK_CHUNKS
    W = LANES
    WB = W * RWKV_BLOCK_PAIRS
    ngrp = RW // WB
    c_gl = 3 * RW // GATE_LORA_PAD
    c_wl = (3 * RW + GATE_LORA_PAD) // W
    c_al = c_wl + 1

    def zcol(base):
        return pl.BlockSpec((None, R, WB), lambda b, p, c: (b, c, base + p))

    def zfix(idx, width):
        return pl.BlockSpec((None, R, width), lambda b, p, c: (b, c, idx))

    def mucol(base):
        return pl.BlockSpec((1, WB), lambda b, p, c: (0, base + p))

    def mufix(idx, width):
        return pl.BlockSpec((1, width), lambda b, p, c: (0, idx))

    pvec = pl.BlockSpec((1, WB), lambda b, p, c: (0, p))

    def lora(rank):
        return pl.BlockSpec((rank, WB), lambda b, p, c: (0, p))

    row = lambda a: a.reshape(1, -1)
    return pl.pallas_call(
        functools.partial(_rwkv_kernel, L=L),
        grid=(B, ngrp, T // R),
        in_specs=[zcol(0), zcol(ngrp), zcol(2 * ngrp),
                  zfix(c_gl, GATE_LORA_PAD), zfix(c_wl, W), zfix(c_al, W),
                  mucol(0), mucol(ngrp), mucol(2 * ngrp),
                  mufix(c_gl, GATE_LORA_PAD), mufix(c_wl, W), mufix(c_al, W),
                  pvec, pvec, pvec, pvec, pvec, pvec, pvec,
                  lora(DECAY_LORA), lora(ICLR_LORA), lora(GATE_LORA_PAD)],
        out_specs=pl.BlockSpec((None, R, WB), lambda b, p, c: (b, c, p)),
        out_shape=jax.ShapeDtypeStruct((B, T, RW), BF16),
        scratch_shapes=[pltpu.VMEM((RWKV_BLOCK_PAIRS, W, W), F32)] + [pltpu.VMEM((8, WB), F32)] * 3
        + [pltpu.VMEM((8, GATE_LORA_PAD), F32), pltpu.VMEM((8, W), F32), pltpu.VMEM((8, W), F32)],
        compiler_params=_cparams(("parallel", "parallel", "arbitrary")),
        name="rwkv7",
    )(z, z, z, z, z, z, mu_p, mu_p, mu_p, mu_p, mu_p, mu_p,
      row(w0), row(a0), row(k_k), row(k_a), row(r_k), row(lnx_g), row(lnx_b), w2, a2, g2p)


def _pad_cols(w, n):
    return jnp.pad(w, ((0, 0), (0, n - w.shape[1])))


def kernel(x, c, ada_w, ada_b, ada_table, ln_g, ln_b, mix_w_in, mix_mu, rwkv_w0, rwkv_w2, rwkv_a0, rwkv_a2,
           rwkv_g2, rwkv_kk, rwkv_ka, rwkv_rk, rwkv_lnx_g, rwkv_lnx_b, mix_w_out, ffn_w_gate, ffn_w_up,
           ffn_w_down, pool_w, pool_scale, moe_router_w, moe_router_b, moe_w_gate, moe_w_up, moe_w_down):
    B, T, D = x.shape
    M = B * T
    depth = ada_table.shape[0]
    RW = rwkv_w0.shape[1]
    MW = D - RW
    n_moba = MW // MOBA_HEAD_DIM
    gate_lora = rwkv_g2.shape[1]
    alpha = (2 * depth) ** 0.25

    mods = ada_mod(c, ada_w, ada_b, ada_table)
    xf = x.reshape(M, D)
    h = modulate(xf, mods[0, :, 1], mods[0, :, 0], T)
    lnrow = lambda a: a.reshape(1, D)

    za = 3 * RW + GATE_LORA_PAD + DECAY_LORA + ICLR_LORA
    n_in = za + 3 * MW
    n_pad = -(-n_in // 1024) * 1024
    o_wl = 3 * RW
    o_al = o_wl + DECAY_LORA
    o_gl = o_al + ICLR_LORA
    o_q = o_gl + gate_lora

    def permute_cols(w):
        return jnp.concatenate(
            [w[..., :3 * RW], _pad_cols(w[..., o_gl:o_q], GATE_LORA_PAD), w[..., o_wl:o_al], w[..., o_al:o_gl],
             _pad_cols(w[..., o_q:], n_pad - za)], axis=-1)

    for l in range(depth):
        i = l // 2
        m = mods[l]
        g1, sh2, sc2, g2 = m[:, 2], m[:, 3], m[:, 4], m[:, 5]
        if l + 1 < depth:
            sh_n, sc_n = mods[l + 1, :, 0], mods[l + 1, :, 1]
        else:
            sh_n, sc_n = jnp.zeros_like(sh2), jnp.zeros_like(sc2)
        if l % 2 == 0:
            w_in = permute_cols(mix_w_in[i]).astype(BF16)
            mu_p = permute_cols(jnp.pad(mix_mu[i], (0, mix_w_in.shape[2] - mix_mu.shape[1]))[None, :])
            z = matmul(h, w_in, 1024, 1024).reshape(B, T, n_pad)
            g2p = jnp.pad(rwkv_g2[i], ((0, GATE_LORA_PAD - gate_lora), (0, 0)))
            ya = rwkv_time_mix(z, mu_p, rwkv_w0[i], rwkv_w2[i], rwkv_a0[i], rwkv_a2[i], g2p, rwkv_kk[i],
                               rwkv_ka[i], rwkv_rk[i], rwkv_lnx_g[i], rwkv_lnx_b[i], B, T, RW)
            yb = moba_attention(z, za // LANES, n_moba, B, T)
            ycat = jnp.concatenate([ya, yb], axis=-1).reshape(M, D)
            xf, h = matmul_resid_ln(ycat, cast_pad_bf16(mix_w_out, i), xf, g1, lnrow(ln_g[l, 0]), lnrow(ln_b[l, 0]),
                                    sc2, sh2, T, alpha, 512, 1024)
            F = ffn_w_gate.shape[2]
            fp = -(-F // 1024) * 1024
            act = swiglu_up(h, cast_pad_bf16(ffn_w_gate, i, cols_p=fp, br=512, bc=fp // 4),
                            cast_pad_bf16(ffn_w_up, i, cols_p=fp, br=512, bc=fp // 4),
                            1024, 512)
            wd = cast_pad_bf16(ffn_w_down, i, rows_p=fp)
            xf, h = matmul_resid_ln(act, wd, xf, g2, lnrow(ln_g[l, 1]), lnrow(ln_b[l, 1]), sc_n, sh_n, T, alpha,
                                    512, 1024)
        else:
            xf, h = pool_mixer_ln(xf, m[:, 1], m[:, 0], pool_w[i].astype(BF16), lnrow(pool_scale[i]), g1,
                                  lnrow(ln_g[l, 0]), lnrow(ln_b[l, 0]), sc2, sh2, T, alpha)
            E, _, NE = moe_w_gate.shape[1:]
            idx, gates = moe_router(h, moe_router_w[i], moe_router_b[i])
            row_token, tile_e, n_used, pos0, pos1 = moe_group_metadata(idx, E, MOE_TILE)
            xs = moe_gather_rows(h, row_token, MOE_TILE)
            cast3 = lambda w, bc: cast_pad_bf16(w.reshape(w.shape[0], -1, w.shape[-1]), i, bc=bc).reshape(w.shape[1:])
            act = moe_up_grouped(xs, tile_e, n_used, cast3(moe_w_gate, NE // 2), cast3(moe_w_up, NE // 2),
                                 MOE_TILE, NE // 2)
            ys = moe_down_grouped(act, tile_e, n_used, cast3(moe_w_down, 1024), MOE_TILE)
            xf, h = moe_combine_ln(ys, pos0, pos1, gates, xf, g2, lnrow(ln_g[l, 1]), lnrow(ln_b[l, 1]), sc_n, sh_n,
                                   T, alpha)
    return xf.reshape(B, T, D)
```
